```python
import math
import jax
import jax.numpy as jnp
from jax import lax
import numpy as np

D_MODEL = 1024
BATCH = 4
SEQ = 4096
DEPTH = 2
DEC_BATCH = 32
DEC_SEQ = 1
PAST_LEN = 8192
PAGE_SIZE = 128

HEAD_DIM = 64
NSA_HEADS = 8
NSA_KV = 2
NSA_REP = NSA_HEADS // NSA_KV
NSA_WIDTH = NSA_HEADS * HEAD_DIM
NSA_KV_WIDTH = NSA_KV * HEAD_DIM
CMP_LEN = 32
CMP_STRIDE = 16
SLC_BLOCK = 64
SLC_TOPK = 16
WINDOW = 512
MOBA_HEADS = 8
MOBA_WIDTH = MOBA_HEADS * HEAD_DIM
MOBA_BLOCK = 256
MOBA_TOPK = 3
N_BUCKETS = 32
MAX_EXACT = N_BUCKETS // 2
MAX_DISTANCE = 128
NSA_QBLOCK = 32
MOBA_QBLOCK = 16
LN_EPS = 1e-5
DEEPNORM_ALPHA = (2 * DEPTH) ** 0.25
DEEPNORM_BETA = (8 * DEPTH) ** -0.25
IN_WIDTHS = (NSA_WIDTH, NSA_KV_WIDTH, NSA_KV_WIDTH, NSA_KV_WIDTH, NSA_KV_WIDTH, NSA_KV_WIDTH, NSA_KV_WIDTH,
             3 * NSA_HEADS, NSA_WIDTH, MOBA_WIDTH, MOBA_WIDTH, MOBA_WIDTH, MOBA_WIDTH, D_MODEL, D_MODEL)

kernel_name = 'nsa_moba_gated_hybrid_step'


def layer_norm(x, g, b):
    xf = x.astype(jnp.float32)
    mu = jnp.mean(xf, axis=-1, keepdims=True)
    var = jnp.mean(jnp.square(xf - mu), axis=-1, keepdims=True)
    y = (xf - mu) * lax.rsqrt(var + LN_EPS)
    return (y * g.astype(jnp.float32) + b.astype(jnp.float32)).astype(x.dtype)


def masked_softmax(s, valid):
    s = jnp.where(valid, s.astype(jnp.float32), -jnp.inf)
    m = jnp.max(s, axis=-1, keepdims=True)
    m = jnp.where(jnp.isfinite(m), m, 0.0)
    e = jnp.where(valid, jnp.exp(s - m), 0.0)
    return e / jnp.maximum(jnp.sum(e, axis=-1, keepdims=True), 1e-30)


def t5_bucket(dist):
    n = jnp.maximum(dist, 0)
    nf = jnp.maximum(n, 1).astype(jnp.float32)
    large = MAX_EXACT + (jnp.log(nf / MAX_EXACT) / math.log(MAX_DISTANCE / MAX_EXACT)
                         * (N_BUCKETS - MAX_EXACT)).astype(jnp.int32)
    return jnp.where(n < MAX_EXACT, n, jnp.minimum(large, N_BUCKETS - 1))


def map_query_blocks(fn, qblock, q_pos, *xs):
    t = q_pos.shape[0]
    qb = min(qblock, t)
    n_blk = t // qb
    chunks = tuple(jnp.moveaxis(x.reshape((x.shape[0], n_blk, qb) + x.shape[2:]), 1, 0) for x in xs)
    out = lax.map(fn, chunks + (q_pos.reshape(n_blk, qb),))
    out = jnp.moveaxis(out, 0, 1)
    return out.reshape((out.shape[0], t) + out.shape[3:])


def compress(kseq, pos_emb, w1, b1, w2, b2):
    bsz, tk, g, dh = kseq.shape
    n = (tk - CMP_LEN) // CMP_STRIDE + 1
    idx = np.arange(n)[:, None] * CMP_STRIDE + np.arange(CMP_LEN)[None, :]
    blk = kseq[:, idx] + pos_emb[None, None, :, None, :]
    blk = blk.transpose(0, 1, 3, 2, 4).reshape(bsz, n, g, CMP_LEN * dh)
    return jax.nn.gelu(blk @ w1 + b1) @ w2 + b2


def overlap_matrix(n_cmp, n_slc):
    i = np.arange(n_cmp)[:, None]
    j = np.arange(n_slc)[None, :]
    units = SLC_BLOCK // CMP_STRIDE
    m = sum(((i + u) // units == j).astype(np.float32) for u in range(CMP_LEN // CMP_STRIDE))
    return jnp.asarray(m, dtype=jnp.float32)


def nsa_block(qc, gc, posc, ck, cv, sk_b, sv_b, wk, wv, win_pos0, overlap, tab_g, k_sel):
    bsz, qb, g, r, dh = qc.shape
    scale = dh ** -0.5
    n_cmp = ck.shape[1]
    n_slc = sk_b.shape[2]
    cend = jnp.arange(n_cmp, dtype=jnp.int32) * CMP_STRIDE + (CMP_LEN - 1)
    dist_c = posc[:, None] - cend[None, :]
    valid_c = (dist_c >= 0)[None, :, None, None, :]
    bias_c = tab_g[:, t5_bucket(dist_c), :].transpose(1, 0, 3, 2)
    s_c = jnp.einsum('bqgrd,bngd->bqgrn', qc, ck) * scale + bias_c
    p_c = masked_softmax(s_c, valid_c)
    o_c = jnp.einsum('bqgrn,bngd->bqgrd', p_c.astype(cv.dtype), cv)
    imp = jnp.einsum('bqgrn,nj->bqgj', p_c, overlap)
    j = jnp.arange(n_slc, dtype=jnp.int32)[None, :]
    cur = (posc // SLC_BLOCK)[:, None]
    forced = (j == 0) | (j == cur) | (j == cur - 1)
    imp = jnp.where(forced[None, :, None, :], jnp.inf, imp)
    imp = jnp.where((j <= cur)[None, :, None, :], imp, -jnp.inf)
    top_val, top_idx = lax.top_k(imp, k_sel)
    sel_idx = top_idx.transpose(0, 2, 1, 3)
    sel_ok = (top_val > -jnp.inf).transpose(0, 2, 1, 3)
    flat = sel_idx.reshape(bsz, g, qb * k_sel, 1)
    kg = jnp.take_along_axis(sk_b, flat, axis=2).reshape(bsz, g, qb, k_sel, SLC_BLOCK, dh)
    vg = jnp.take_along_axis(sv_b, flat, axis=2).reshape(bsz, g, qb, k_sel * SLC_BLOCK, dh)
    kpos = sel_idx[..., None] * SLC_BLOCK + jnp.arange(SLC_BLOCK, dtype=jnp.int32)
    dist_s = posc[None, None, :, None, None] - kpos
    valid_s = ((dist_s >= 0) & sel_ok[..., None]).reshape(bsz, g, qb, 1, k_sel * SLC_BLOCK)
    g_ix = jnp.arange(g)[None, :, None, None, None]
    bias_s = tab_g[g_ix, t5_bucket(dist_s)].transpose(0, 1, 2, 5, 3, 4)
    s_s = jnp.einsum('bqgrd,bgqksd->bgqrks', qc, kg) * scale + bias_s
    p_s = masked_softmax(s_s.reshape(bsz, g, qb, r, k_sel * SLC_BLOCK), valid_s)
    o_s = jnp.einsum('bgqrn,bgqnd->bqgrd', p_s.astype(vg.dtype), vg)
    lw = qb + WINDOW - 1
    first = posc[0] - (WINDOW - 1)
    kw = lax.dynamic_slice_in_dim(wk, first - win_pos0, lw, axis=1)
    vw = lax.dynamic_slice_in_dim(wv, first - win_pos0, lw, axis=1)
    wpos = first + jnp.arange(lw, dtype=jnp.int32)
    dist_w = posc[:, None] - wpos[None, :]
    valid_w = ((dist_w >= 0) & (dist_w < WINDOW) & (wpos[None, :] >= 0))[None, :, None, None, :]
    bias_w = tab_g[:, t5_bucket(dist_w), :].transpose(1, 0, 3, 2)
    s_w = jnp.einsum('bqgrd,bkgd->bqgrk', qc, kw) * scale + bias_w
    p_w = masked_softmax(s_w, valid_w)
    o_w = jnp.einsum('bqgrk,bkgd->bqgrd', p_w.astype(vw.dtype), vw)
    return gc[..., 0:1] * o_c + gc[..., 1:2] * o_s + gc[..., 2:3] * o_w


def nsa_attention(q, gates, q_pos, cmp_seq, slc_seq, win_pad, win_pos0, phi_pos, phi_w1, phi_b1, phi_w2, phi_b2, tab):
    bsz = q.shape[0]
    tk = cmp_seq.shape[1]
    ck = compress(cmp_seq[:, :, 0], phi_pos[0], phi_w1[0], phi_b1[0], phi_w2[0], phi_b2[0])
    cv = compress(cmp_seq[:, :, 1], phi_pos[1], phi_w1[1], phi_b1[1], phi_w2[1], phi_b2[1])
    n_slc = -(-tk // SLC_BLOCK)
    slc_pad = jnp.pad(slc_seq, ((0, 0), (0, n_slc * SLC_BLOCK - tk), (0, 0), (0, 0), (0, 0)))
    slc_blk = slc_pad.reshape(bsz, n_slc, SLC_BLOCK, 2, NSA_KV, HEAD_DIM).transpose(3, 0, 4, 1, 2, 5)
    slc_blk = slc_blk.reshape(2, bsz, NSA_KV, n_slc, SLC_BLOCK * HEAD_DIM)
    sk_b, sv_b = slc_blk[0], slc_blk[1]
    wk, wv = win_pad[:, :, 0], win_pad[:, :, 1]
    overlap = overlap_matrix(ck.shape[1], n_slc)
    tab_g = tab.reshape(N_BUCKETS, NSA_KV, NSA_REP).transpose(1, 0, 2)
    k_sel = min(SLC_TOPK, n_slc)

    def block_fn(args):
        qc, gc, posc = args
        return nsa_block(qc, gc, posc, ck, cv, sk_b, sv_b, wk, wv, win_pos0, overlap, tab_g, k_sel)

    return map_query_blocks(block_fn, NSA_QBLOCK, q_pos, q, gates)


def moba_block(qc, posc, kmean, kb, vb, tab_h, k_m):
    bsz, qb, h, dh = qc.shape
    scale = dh ** -0.5
    blk = posc // MOBA_BLOCK
    own = blk[0]
    kown = lax.dynamic_index_in_dim(kb, own, axis=2, keepdims=False)
    vown = lax.dynamic_index_in_dim(vb, own, axis=2, keepdims=False)
    opos = own * MOBA_BLOCK + jnp.arange(MOBA_BLOCK, dtype=jnp.int32)
    dist_o = posc[:, None] - opos[None, :]
    s_own = jnp.einsum('bqhd,bhld->bhql', qc, kown) * scale + tab_h[:, t5_bucket(dist_o)]
    valid_own = jnp.broadcast_to((dist_o >= 0)[None, None], s_own.shape)
    if k_m == 0:
        p = masked_softmax(s_own, valid_own).astype(vb.dtype)
        return jnp.einsum('bhql,bhld->bqhd', p, vown)
    nblk = kmean.shape[1]
    gs = jnp.einsum('bqhd,bnhd->bqhn', qc.astype(jnp.float32), kmean)
    past_ok = (jnp.arange(nblk, dtype=jnp.int32)[None, :] < blk[:, None])[None, :, None, :]
    gs = jnp.where(past_ok, gs, -jnp.inf)
    top_val, top_idx = lax.top_k(gs, k_m)
    sel_idx = top_idx.transpose(0, 2, 1, 3)
    sel_ok = (top_val > -jnp.inf).transpose(0, 2, 1, 3)
    flat = sel_idx.reshape(bsz, h, qb * k_m, 1)
    nb = kb.shape[2]
    ksel = jnp.take_along_axis(kb.reshape(bsz, h, nb, MOBA_BLOCK * dh), flat, axis=2)
    ksel = ksel.reshape(bsz, h, qb, k_m, MOBA_BLOCK, dh)
    vsel = jnp.take_along_axis(vb.reshape(bsz, h, nb, MOBA_BLOCK * dh), flat, axis=2)
    vsel = vsel.reshape(bsz, h, qb, k_m * MOBA_BLOCK, dh)
    kpos = sel_idx[..., None] * MOBA_BLOCK + jnp.arange(MOBA_BLOCK, dtype=jnp.int32)
    dist = posc[None, None, :, None, None] - kpos
    h_ix = jnp.arange(h)[None, :, None, None, None]
    s_sel = jnp.einsum('bqhd,bhqkld->bhqkl', qc, ksel) * scale + tab_h[h_ix, t5_bucket(dist)]
    valid_sel = jnp.broadcast_to(sel_ok[..., None], s_sel.shape)
    s = jnp.concatenate([s_sel.reshape(bsz, h, qb, k_m * MOBA_BLOCK), s_own], axis=-1)
    valid = jnp.concatenate([valid_sel.reshape(bsz, h, qb, k_m * MOBA_BLOCK), valid_own], axis=-1)
    p = masked_softmax(s, valid).astype(vb.dtype)
    n_sel = k_m * MOBA_BLOCK
    return (jnp.einsum('bhqn,bhqnd->bqhd', p[..., :n_sel], vsel)
            + jnp.einsum('bhql,bhld->bqhd', p[..., n_sel:], vown))


def moba_attention(q, q_pos, kv_seq, tab):
    bsz, _, h, dh = q.shape
    tk = kv_seq.shape[1]
    nblk = -(-tk // MOBA_BLOCK)
    kv_pad = jnp.pad(kv_seq, ((0, 0), (0, nblk * MOBA_BLOCK - tk), (0, 0), (0, 0), (0, 0)))
    kv_blk = kv_pad.reshape(bsz, nblk, MOBA_BLOCK, 2, h, dh)
    kmean = jnp.mean(kv_blk[:, :, :, 0].astype(jnp.float32), axis=2)
    kb = kv_blk[:, :, :, 0].transpose(0, 3, 1, 2, 4)
    vb = kv_blk[:, :, :, 1].transpose(0, 3, 1, 2, 4)
    tab_h = tab.T
    k_m = min(MOBA_TOPK, nblk - 1)

    def block_fn(args):
        qc, posc = args
        return moba_block(qc, posc, kmean, kb, vb, tab_h, k_m)

    return map_query_blocks(block_fn, MOBA_QBLOCK, q_pos, q)


def trunk_layer(x, c, offset, past, rel_bias, w_ada, b_ada, w_in, phi_pos, phi_w1, phi_b1, phi_w2, phi_b2,
                w_up_a, w_up_b, w_out, ln_g, ln_b):
    bsz, t, _ = x.shape
    shift, scale, gate = jnp.split(jax.nn.silu(c) @ w_ada + b_ada, 3, axis=-1)
    h = x * (1.0 + scale[:, None, :]) + shift[:, None, :]
    parts = jnp.split(h @ w_in, np.cumsum(IN_WIDTHS)[:-1], axis=-1)
    (a_q, a_kc, a_vc, a_ks, a_vs, a_kw, a_vw, a_g, a_z, b_q, b_k, b_v, b_z, m_a, m_b) = parts

    def kv_rows(k, v, heads):
        return jnp.stack([k.reshape(bsz, t, heads, HEAD_DIM), v.reshape(bsz, t, heads, HEAD_DIM)], axis=2)

    cmp_new = kv_rows(a_kc, a_vc, NSA_KV)
    slc_new = kv_rows(a_ks, a_vs, NSA_KV)
    win_new = kv_rows(a_kw, a_vw, NSA_KV)
    moba_new = kv_rows(b_k, b_v, MOBA_HEADS)
    if past is None:
        cmp_seq, slc_seq, moba_seq, win_seq = cmp_new, slc_new, moba_new, win_new
        win_past = 0
    else:
        cmp_rows, slc_rows, moba_rows, win_rows = past
        cmp_seq = jnp.concatenate([cmp_rows, cmp_new], axis=1)
        slc_seq = jnp.concatenate([slc_rows, slc_new], axis=1)
        moba_seq = jnp.concatenate([moba_rows, moba_new], axis=1)
        win_seq = jnp.concatenate([win_rows, win_new], axis=1)
        win_past = win_rows.shape[1]
    win_pad = jnp.concatenate([jnp.zeros((bsz, WINDOW) + win_seq.shape[2:], win_seq.dtype), win_seq], axis=1)
    q_pos = offset + jnp.arange(t, dtype=jnp.int32)

    o_a = nsa_attention(a_q.reshape(bsz, t, NSA_KV, NSA_REP, HEAD_DIM),
                        jax.nn.sigmoid(a_g.reshape(bsz, t, NSA_KV, NSA_REP, 3)),
                        q_pos, cmp_seq, slc_seq, win_pad, offset - win_past - WINDOW,
                        phi_pos, phi_w1, phi_b1, phi_w2, phi_b2, rel_bias[:, :NSA_HEADS])
    o_b = moba_attention(b_q.reshape(bsz, t, MOBA_HEADS, HEAD_DIM), q_pos, moba_seq, rel_bias[:, NSA_HEADS:])
    y_a = (o_a.reshape(bsz, t, NSA_WIDTH) * jax.nn.silu(a_z)) @ w_up_a
    y_b = (o_b.reshape(bsz, t, MOBA_WIDTH) * jax.nn.silu(b_z)) @ w_up_b
    mixed = (jax.nn.sigmoid(m_a) * y_a + jax.nn.sigmoid(m_b) * y_b) @ w_out
    x_new = layer_norm(DEEPNORM_ALPHA * x + gate[:, None, :] * mixed, ln_g, ln_b)
    win_out = win_new[:, t - min(WINDOW, t):] if past is None else win_new
    return x_new, cmp_new, slc_new, moba_new, win_out


def gather_rows(pool, page_table, layer):
    rows = pool[page_table, layer]
    return rows.reshape((rows.shape[0], rows.shape[1] * rows.shape[2]) + rows.shape[3:])


def setup_inputs(seed: int = 0) -> dict:
    key = jax.random.key(seed)
    ks = jax.random.split(key, 26)
    f32 = jnp.float32
    n_pages = PAST_LEN // PAGE_SIZE
    n_used = DEC_BATCH * n_pages
    n_phys = n_used + (n_used + 3) // 4
    win_rows = min(WINDOW, PAST_LEN)
    n_in = sum(IN_WIDTHS)

    def nrm(k, shape, s):
        return jax.random.normal(k, shape, f32) * s

    page_table = jax.random.permutation(ks[0], n_phys)[:n_used].reshape(DEC_BATCH, n_pages).astype(jnp.int32)
    return {
        'x_prompt': nrm(ks[1], (BATCH, SEQ, D_MODEL), 1.0),
        'x_sample': nrm(ks[2], (DEC_BATCH, DEC_SEQ, D_MODEL), 1.0),
        'cache_nsa_cmp': nrm(ks[3], (n_phys, DEPTH, PAGE_SIZE, 2, NSA_KV, HEAD_DIM), 1.0),
        'cache_nsa_slc': nrm(ks[4], (n_phys, DEPTH, PAGE_SIZE, 2, NSA_KV, HEAD_DIM), 1.0),
        'cache_moba': nrm(ks[5], (n_phys, DEPTH, PAGE_SIZE, 2, MOBA_HEADS, HEAD_DIM), 1.0),
        'state_nsa_win': nrm(ks[6], (DEPTH, DEC_BATCH, win_rows, 2, NSA_KV, HEAD_DIM), 1.0),
        'page_table': page_table,
        'c_prompt': nrm(ks[7], (BATCH, D_MODEL), 1.0),
        'c_sample': nrm(ks[8], (DEC_BATCH, D_MODEL), 1.0),
        'rel_bias': nrm(ks[9], (N_BUCKETS, NSA_HEADS + MOBA_HEADS), 0.5),
        'w_ada': nrm(ks[10], (DEPTH, D_MODEL, 3 * D_MODEL), 0.5 * D_MODEL ** -0.5),
        'b_ada': nrm(ks[11], (DEPTH, 3 * D_MODEL), 0.02),
        'w_in': nrm(ks[12], (DEPTH, D_MODEL, n_in), D_MODEL ** -0.5),
        'phi_pos': nrm(ks[13], (DEPTH, 2, CMP_LEN, HEAD_DIM), 0.02),
        'phi_w1': nrm(ks[14], (DEPTH, 2, CMP_LEN * HEAD_DIM, HEAD_DIM), (CMP_LEN * HEAD_DIM) ** -0.5),
        'phi_b1': nrm(ks[15], (DEPTH, 2, HEAD_DIM), 0.02),
        'phi_w2': nrm(ks[16], (DEPTH, 2, HEAD_DIM, HEAD_DIM), HEAD_DIM ** -0.5),
        'phi_b2': nrm(ks[17], (DEPTH, 2, HEAD_DIM), 0.02),
        'w_up_a': nrm(ks[18], (DEPTH, NSA_WIDTH, D_MODEL), NSA_WIDTH ** -0.5 * DEEPNORM_BETA),
        'w_up_b': nrm(ks[19], (DEPTH, MOBA_WIDTH, D_MODEL), MOBA_WIDTH ** -0.5 * DEEPNORM_BETA),
        'w_out': nrm(ks[20], (DEPTH, D_MODEL, D_MODEL), D_MODEL ** -0.5 * DEEPNORM_BETA),
        'ln_g': 1.0 + nrm(ks[21], (DEPTH, D_MODEL), 0.02),
        'ln_b': nrm(ks[22], (DEPTH, D_MODEL), 0.02),
    }


def reference(x_prompt, x_sample, cache_nsa_cmp, cache_nsa_slc, cache_moba, state_nsa_win, page_table,
              c_prompt, c_sample, rel_bias, w_ada, b_ada, w_in, phi_pos, phi_w1, phi_b1, phi_w2, phi_b2,
              w_up_a, w_up_b, w_out, ln_g, ln_b):
    past_len = page_table.shape[1] * PAGE_SIZE
    yp, ys = x_prompt, x_sample
    p_cmp, s_cmp, p_slc, s_slc, p_moba, s_moba, p_win, s_win = [], [], [], [], [], [], [], []
    for l in range(DEPTH):
        wl = (w_ada[l], b_ada[l], w_in[l], phi_pos[l], phi_w1[l], phi_b1[l], phi_w2[l], phi_b2[l],
              w_up_a[l], w_up_b[l], w_out[l], ln_g[l], ln_b[l])
        yp, pc, ps, pm, pw = trunk_layer(yp, c_prompt, 0, None, rel_bias, *wl)
        past = (gather_rows(cache_nsa_cmp, page_table, l), gather_rows(cache_nsa_slc, page_table, l),
                gather_rows(cache_moba, page_table, l), state_nsa_win[l])
        ys, sc, ss, sm, sw = trunk_layer(ys, c_sample, past_len, past, rel_bias, *wl)
        p_cmp.append(pc); s_cmp.append(sc); p_slc.append(ps); s_slc.append(ss)
        p_moba.append(pm); s_moba.append(sm); p_win.append(pw); s_win.append(sw)
    return (yp, ys,
            jnp.stack(p_cmp, axis=1), jnp.stack(s_cmp, axis=1),
            jnp.stack(p_slc, axis=1), jnp.stack(s_slc, axis=1),
            jnp.stack(p_moba, axis=1), jnp.stack(s_moba, axis=1),
            jnp.stack(p_win, axis=0), jnp.stack(s_win, axis=0))
```

```python
import functools
import math

import numpy as np
import jax
import jax.numpy as jnp
from jax import lax
from jax.experimental import pallas as pl
from jax.experimental.pallas import tpu as pltpu

F32 = jnp.float32
BF16 = jnp.bfloat16
NEG_INF = float("-inf")

HEAD_DIM = 64
NSA_HEADS = 8
NSA_KV = 2
NSA_REP = NSA_HEADS // NSA_KV
CMP_LEN = 32
CMP_STRIDE = 16
SLC_BLOCK = 64
SLC_TOPK = 16
WINDOW = 512
MOBA_HEADS = 8
MOBA_BLOCK = 256
MOBA_TOPK = 3
N_BUCKETS = 32
MAX_EXACT = N_BUCKETS // 2
MAX_DISTANCE = 128
LN_EPS = 1e-5
PAGE_SIZE = 128

LANES = 128
SUBLANES = 8
TQ = 256
PAGES_PER_STEP = 8

C_AQ, C_AZ, C_BQ, C_BZ, C_BK, C_BV, C_MA, C_MB, C_CMP, C_SLC, C_WIN, C_AG = (
    0, 512, 1024, 1536, 2048, 2560, 3072, 4096, 5120, 5376, 5632, 5888)
AG_PAD = 256
N_PROJ = C_AG + AG_PAD


def _cparams(n_axes, vmem_mb=None):
    kw = dict(dimension_semantics=("arbitrary",) * n_axes)
    if vmem_mb is not None:
        kw["vmem_limit_bytes"] = vmem_mb * 1024 * 1024
    return pltpu.CompilerParams(**kw)


def _dot(a, b):
    return jnp.dot(a, b, preferred_element_type=F32)


def _dot_nt(a, b):
    return lax.dot_general(a, b, (((1,), (1,)), ((), ())), preferred_element_type=F32)


def _sigmoid(x):
    return 1.0 / (1.0 + jnp.exp(-x))


def _silu(x):
    return x * _sigmoid(x)


def _bucket(dist):
    n = jnp.maximum(dist, 0)
    nf = jnp.maximum(n, 1).astype(F32)
    large = MAX_EXACT + (jnp.log(nf / MAX_EXACT) / math.log(MAX_DISTANCE / MAX_EXACT)
                         * (N_BUCKETS - MAX_EXACT)).astype(jnp.int32)
    return jnp.where(n < MAX_EXACT, n, jnp.minimum(large, N_BUCKETS - 1))


def _rank_rows(blocks, n):
    nb = len(blocks)
    cnt = [jnp.zeros(blocks[0].shape, F32) for _ in range(nb)]
    sub = lax.broadcasted_iota(jnp.int32, blocks[0].shape, 0)
    for jp in range(n):
        rb0, r0 = divmod(jp, SUBLANES)
        row = blocks[rb0][r0:r0 + 1, :]
        for rb in range(nb):
            a = blocks[rb]
            if rb < rb0:
                ahead = jnp.where(row > a, 1.0, 0.0)
            elif rb > rb0:
                ahead = jnp.where(row >= a, 1.0, 0.0)
            else:
                ahead = jnp.where(sub > r0, jnp.where(row >= a, 1.0, 0.0), jnp.where(row > a, 1.0, 0.0))
            cnt[rb] = cnt[rb] + ahead
    return cnt


def _softmax_tile_update(s, v, m_scr, l_scr, acc_scr):
    n = s.shape[1]
    m_prev = m_scr[...]
    m_next = jnp.maximum(m_prev, jnp.max(s, axis=1, keepdims=True))
    m_safe = jnp.where(m_next == NEG_INF, 0.0, m_next)
    alpha = jnp.exp(m_prev - m_safe)
    p = jnp.exp(s - jnp.concatenate([m_safe] * (n // LANES), axis=1))
    l_scr[...] = alpha * l_scr[...] + jnp.sum(p, axis=1, keepdims=True)
    acc_scr[...] = acc_scr[...] * alpha[:, :HEAD_DIM] + _dot(p.astype(BF16), v)
    m_scr[...] = m_next


def _softmax_init(m_scr, l_scr, acc_scr):
    m_scr[...] = jnp.full(m_scr.shape, NEG_INF, F32)
    l_scr[...] = jnp.zeros(l_scr.shape, F32)
    acc_scr[...] = jnp.zeros(acc_scr.shape, F32)


def _softmax_finish(l_scr, acc_scr):
    return acc_scr[...] / jnp.maximum(l_scr[...], 1e-30)[:, :HEAD_DIM]


def _ada_kernel(c_ref, w_ref, b_ref, o_ref):
    a = _silu(c_ref[...]).astype(BF16)
    o_ref[...] = _dot(a, w_ref[...].astype(BF16)) + b_ref[...]


def _ada(c_all, w_ada, b_ada):
    depth, d, n3 = w_ada.shape
    mc = c_all.shape[0]
    tn = 1024
    return pl.pallas_call(
        _ada_kernel,
        grid=(depth, n3 // tn),
        in_specs=[pl.BlockSpec((mc, d), lambda l, j: (0, 0)),
                  pl.BlockSpec((None, d, tn), lambda l, j: (l, 0, j)),
                  pl.BlockSpec((None, 1, tn), lambda l, j: (l, 0, j))],
        out_specs=pl.BlockSpec((None, mc, tn), lambda l, j: (l, 0, j)),
        out_shape=jax.ShapeDtypeStruct((depth, mc, n3), F32),
        compiler_params=_cparams(2, 40),
        name="ada_mod",
    )(c_all, w_ada, b_ada.reshape(depth, 1, n3))


def _inproj_kernel(x_ref, sc_ref, sh_ref, w_ref, o_ref, h_scr):
    @pl.when(pl.program_id(1) == 0)
    def _():
        h_scr[...] = (x_ref[...] * (1.0 + sc_ref[...]) + sh_ref[...]).astype(BF16)

    o_ref[...] = _dot(h_scr[...], w_ref[...])


def _mod_spec(mod, tm, nt):
    if mod.shape[1] == 1:
        return pl.BlockSpec((None, 1, mod.shape[2]), lambda m, *_: (m // nt, 0, 0))
    return pl.BlockSpec((None, tm, mod.shape[2]), lambda m, *_: (m // nt, m % nt, 0))


def _inproj(x, scale, shift, w):
    bx, t, d = x.shape
    tm = min(t, 1024)
    tn = 1536
    nt = t // tm
    return pl.pallas_call(
        _inproj_kernel,
        grid=(bx * nt, N_PROJ // tn),
        in_specs=[pl.BlockSpec((None, tm, d), lambda m, j: (m // nt, m % nt, 0)),
                  _mod_spec(scale, tm, nt), _mod_spec(shift, tm, nt),
                  pl.BlockSpec((d, tn), lambda m, j: (0, j))],
        out_specs=pl.BlockSpec((None, tm, tn), lambda m, j: (m // nt, m % nt, j)),
        out_shape=jax.ShapeDtypeStruct((bx, t, N_PROJ), F32),
        scratch_shapes=[pltpu.VMEM((tm, d), BF16)],
        compiler_params=_cparams(2, 48),
        name="in_proj",
    )(x, scale, shift, w)


def _cmp_proj_kernel(*refs, n_x, n_prefetch=0):
    refs = refs[n_prefetch:]
    wk_ref, wv_ref, abk_ref, abv_ref = refs[2 * n_x:]
    for x_refs, w_ref, ab_ref in ((refs[:n_x], wk_ref, abk_ref), (refs[n_x:2 * n_x], wv_ref, abv_ref)):
        for pi, x_ref in enumerate(x_refs):
            m = x_ref.shape[0] // CMP_STRIDE
            xr = jnp.concatenate([x_ref[pl.ds(l, m, stride=CMP_STRIDE), :] for l in range(CMP_STRIDE)], axis=1)
            ab_ref[pi * m:(pi + 1) * m, :] = _dot(xr.astype(BF16), w_ref[...])


def _cmp_proj_prompt(proj, wk, wv):
    bx, t, _ = proj.shape
    r = min(t, 1024)
    nt = t // r
    m = r // CMP_STRIDE
    wspec = pl.BlockSpec(wk.shape, lambda b, i: (0, 0))
    ospec = pl.BlockSpec((None, m, 256), lambda b, i: (b, i, 0))
    oshape = jax.ShapeDtypeStruct((bx, t // CMP_STRIDE, 256), F32)
    return pl.pallas_call(
        functools.partial(_cmp_proj_kernel, n_x=1),
        grid=(bx, nt),
        in_specs=[pl.BlockSpec((None, r, LANES), lambda b, i: (b, i, C_CMP // LANES)),
                  pl.BlockSpec((None, r, LANES), lambda b, i: (b, i, C_CMP // LANES + 1)), wspec, wspec],
        out_specs=[ospec, ospec],
        out_shape=[oshape, oshape],
        compiler_params=_cparams(2),
        name="cmp_proj_prompt",
    )(proj, proj, wk, wv)


def _cmp_proj_paged(cache, page_table, layer, wk, wv):
    db, n_pages = page_table.shape
    pps = PAGES_PER_STEP
    m = PAGE_SIZE // CMP_STRIDE

    def page_spec(k, kv):
        return pl.BlockSpec((None, None, PAGE_SIZE, LANES),
                            lambda b, i, pt: (pt[b * n_pages + i * pps + k], layer, 0, kv))

    wspec = pl.BlockSpec(wk.shape, lambda b, i, pt: (0, 0))
    ospec = pl.BlockSpec((None, pps * m, 256), lambda b, i, pt: (b, i, 0))
    oshape = jax.ShapeDtypeStruct((db, n_pages * m, 256), F32)
    return pl.pallas_call(
        functools.partial(_cmp_proj_kernel, n_x=pps, n_prefetch=1),
        grid_spec=pltpu.PrefetchScalarGridSpec(
            num_scalar_prefetch=1,
            grid=(db, n_pages // pps),
            in_specs=[page_spec(k, kv) for kv in range(2) for k in range(pps)] + [wspec, wspec],
            out_specs=[ospec, ospec]),
        out_shape=[oshape, oshape],
        compiler_params=_cparams(2),
        name="cmp_proj_paged",
    )(page_table.reshape(-1), *([cache] * (2 * pps)), wk, wv)


def _cmp_mlp_kernel(abk_ref, abv_ref, pos_ref, w1_ref, b1_ref, w2_ref, b2_ref, ck_ref, cv_ref):
    m = abk_ref.shape[0]
    row = lax.broadcasted_iota(jnp.int32, (m, HEAD_DIM), 0)
    for kv, (ab_ref, o_ref) in enumerate(((abk_ref, ck_ref), (abv_ref, cv_ref))):
        pos = jnp.broadcast_to(pos_ref[kv], (SUBLANES, CMP_LEN * HEAD_DIM)).astype(BF16)
        c0 = _dot(pos, w1_ref[kv].astype(BF16))[0:1, :] + b1_ref[kv]
        c0 = jnp.concatenate([c0] * NSA_KV, axis=1)
        ab = ab_ref[...]
        nxt = pltpu.roll(ab[:, LANES:], m - 1, 0)
        h = jax.nn.gelu(ab[:, :LANES] + nxt + c0).astype(BF16)
        for g in range(NSA_KV):
            y = _dot(h, w2_ref[kv, g]) + b2_ref[kv]
            o_ref[g] = jnp.where(row < m - 1, y, 0.0).astype(BF16)


def _cmp_mlp(abk, abv, pos_flat, w1, b1, w2e, b2):
    bx, m, _ = abk.shape
    abspec = pl.BlockSpec((None, m, 256), lambda b: (b, 0, 0))
    ospec = pl.BlockSpec((None, NSA_KV, m, HEAD_DIM), lambda b: (b, 0, 0, 0))
    oshape = jax.ShapeDtypeStruct((bx, NSA_KV, m, HEAD_DIM), BF16)

    def full(a):
        return pl.BlockSpec(a.shape, lambda b: (0,) * a.ndim)

    return pl.pallas_call(
        _cmp_mlp_kernel,
        grid=(bx,),
        in_specs=[abspec, abspec, full(pos_flat), full(w1), full(b1), full(w2e), full(b2)],
        out_specs=[ospec, ospec],
        out_shape=[oshape, oshape],
        compiler_params=_cparams(1),
        name="cmp_mlp",
    )(abk, abv, pos_flat, w1, b1, w2e, b2)


def _bias_tiles_kernel(tab_ref, o_ref):
    h = pl.program_id(0)
    i = lax.broadcasted_iota(jnp.int32, (TQ, TQ), 0)
    j = lax.broadcasted_iota(jnp.int32, (TQ, TQ), 1)
    far = tab_ref[N_BUCKETS - 1, h]
    for kind in range(2):
        dist = i - j + kind * TQ
        bkt = _bucket(dist)
        bias = jnp.zeros((TQ, TQ), F32)
        for b in range(N_BUCKETS - 1):
            bias = jnp.where(bkt == b, tab_ref[b, h] - far, bias)
        o_ref[kind] = jnp.where(dist >= 0, bias, NEG_INF)


def _bias_tiles(rel_bias):
    nh = rel_bias.shape[1]
    return pl.pallas_call(
        _bias_tiles_kernel,
        grid=(nh,),
        in_specs=[pl.BlockSpec(memory_space=pltpu.SMEM)],
        out_specs=pl.BlockSpec((2, None, TQ, TQ), lambda h: (0, h, 0, 0)),
        out_shape=jax.ShapeDtypeStruct((2, nh, TQ, TQ), F32),
        compiler_params=_cparams(1),
        name="bias_tiles",
    )(rel_bias)


def _select_mask(imp, pos, n_slc, k_sel):
    q, jl = imp.shape
    j = lax.broadcasted_iota(jnp.int32, (q, jl), 1)
    cur = pos // SLC_BLOCK
    forced = (j == 0) | (j == cur) | (j == cur - 1)
    imp = jnp.where(forced, jnp.inf, imp)
    imp = jnp.where(j <= cur, imp, NEG_INF)
    imp_t = imp.T
    nb = -(-n_slc // SUBLANES)
    blocks = [imp_t[rb * SUBLANES:(rb + 1) * SUBLANES, :] for rb in range(nb)]
    cnt = _rank_rows(blocks, n_slc)
    sel = [jnp.where(c < k_sel, jnp.where(a > NEG_INF, 1.0, 0.0), 0.0) for c, a in zip(cnt, blocks)]
    if nb * SUBLANES < jl:
        sel.append(jnp.zeros((jl - nb * SUBLANES, q), F32))
    return jnp.concatenate(sel, axis=0).T


def _nsa_cmp_kernel(tab_ref, q_ref, ck_ref, cv_ref, ov_ref, oc_ref, sel_ref, *, n_slc, k_sel):
    i = pl.program_id(1)
    ncp = ck_ref.shape[1]
    qpos = i * TQ + lax.broadcasted_iota(jnp.int32, (TQ, ncp), 0)
    cend = lax.broadcasted_iota(jnp.int32, (TQ, ncp), 1) * CMP_STRIDE + (CMP_LEN - 1)
    dist = qpos - cend
    valid = dist >= 0
    bkt = _bucket(dist)
    pos_j = i * TQ + lax.broadcasted_iota(jnp.int32, (TQ, ov_ref.shape[1]), 0)
    for g in range(NSA_KV):
        heads = range(g * NSA_REP, (g + 1) * NSA_REP)
        bias = [jnp.zeros((TQ, ncp), F32) for _ in heads]
        for b in range(N_BUCKETS - 1):
            hit = bkt == b
            bias = [jnp.where(hit, tab_ref[b, h] - tab_ref[N_BUCKETS - 1, h], bb) for h, bb in zip(heads, bias)]
        bias = jnp.concatenate([jnp.where(valid, bb, NEG_INF) for bb in bias], axis=0)
        q4 = q_ref[g * NSA_REP:(g + 1) * NSA_REP].reshape(NSA_REP * TQ, HEAD_DIM)
        s = _dot_nt(q4, ck_ref[g]) + bias
        m = jnp.max(s, axis=1, keepdims=True)
        m = jnp.where(m == NEG_INF, 0.0, m)
        e = jnp.exp(s - m)
        p = (e / jnp.maximum(jnp.sum(e, axis=1, keepdims=True), 1e-30)).astype(BF16)
        o = _dot(p, cv_ref[g])
        oc_ref[:, g * NSA_REP * HEAD_DIM:(g + 1) * NSA_REP * HEAD_DIM] = jnp.concatenate(
            [o[r * TQ:(r + 1) * TQ] for r in range(NSA_REP)], axis=1)
        imp4 = _dot(p, ov_ref[...])
        imp = imp4[0:TQ]
        for r in range(1, NSA_REP):
            imp = imp + imp4[r * TQ:(r + 1) * TQ]
        sel_ref[g] = _select_mask(imp, pos_j, n_slc, k_sel).astype(BF16)


def _nsa_cmp_prompt(rel_bias, q, ck, cv, overlap, n_slc, k_sel):
    b, _, t, _ = q.shape
    ncp = ck.shape[2]
    jl = overlap.shape[1]
    return pl.pallas_call(
        functools.partial(_nsa_cmp_kernel, n_slc=n_slc, k_sel=k_sel),
        grid=(b, t // TQ),
        in_specs=[pl.BlockSpec(memory_space=pltpu.SMEM),
                  pl.BlockSpec((None, NSA_HEADS, TQ, HEAD_DIM), lambda bb, i: (bb, 0, i, 0)),
                  pl.BlockSpec((None, NSA_KV, ncp, HEAD_DIM), lambda bb, i: (bb, 0, 0, 0)),
                  pl.BlockSpec((None, NSA_KV, ncp, HEAD_DIM), lambda bb, i: (bb, 0, 0, 0)),
                  pl.BlockSpec((ncp, jl), lambda bb, i: (0, 0))],
        out_specs=[pl.BlockSpec((None, TQ, NSA_HEADS * HEAD_DIM), lambda bb, i: (bb, i, 0)),
                   pl.BlockSpec((None, NSA_KV, TQ, jl), lambda bb, i: (bb, 0, i, 0))],
        out_shape=[jax.ShapeDtypeStruct((b, t, NSA_HEADS * HEAD_DIM), F32),
                   jax.ShapeDtypeStruct((b, NSA_KV, t, jl), BF16)],
        compiler_params=_cparams(2, 40),
        name="nsa_cmp_prompt",
    )(rel_bias, q, ck, cv, overlap)


def _nsa_slc_kernel(q_ref, k_ref, v_ref, sel_ref, e_ref, bt_ref, o_ref, m_scr, l_scr, acc_scr):
    i = pl.program_id(2)
    q4 = q_ref[...].reshape(NSA_REP * TQ, HEAD_DIM)
    sel = sel_ref[...]
    _softmax_init(m_scr, l_scr, acc_scr)

    def tile(kt, bias):
        start = pl.multiple_of(kt * TQ, TQ)
        s = _dot_nt(q4, k_ref[pl.ds(start, TQ), :])
        if bias is not None:
            s = s + bias
        picked = _dot(sel, e_ref[kt])
        picked = jnp.concatenate([picked] * NSA_REP, axis=0)
        s = jnp.where(picked > 0.5, s, NEG_INF)
        _softmax_tile_update(s, v_ref[pl.ds(start, TQ), :], m_scr, l_scr, acc_scr)

    def far_body(kt, c):
        tile(kt, None)
        return c

    lax.fori_loop(0, jnp.maximum(i - 1, 0), far_body, 0)

    @pl.when(i >= 1)
    def _():
        tile(i - 1, bt_ref[1].reshape(NSA_REP * TQ, TQ))

    tile(i, bt_ref[0].reshape(NSA_REP * TQ, TQ))
    o = _softmax_finish(l_scr, acc_scr)
    o_ref[...] = jnp.concatenate([o[r * TQ:(r + 1) * TQ] for r in range(NSA_REP)], axis=1)


def _nsa_slc_prompt(q, kv, sel, emat, btiles):
    b, _, t, _ = q.shape
    jl = sel.shape[3]
    nq = t // TQ
    m4 = NSA_REP * TQ
    return pl.pallas_call(
        _nsa_slc_kernel,
        grid=(b, NSA_KV, nq),
        in_specs=[pl.BlockSpec((None, NSA_REP, TQ, HEAD_DIM), lambda bb, g, i: (bb, g, i, 0)),
                  pl.BlockSpec((None, None, None, t, HEAD_DIM), lambda bb, g, i: (bb, 0, g, 0, 0)),
                  pl.BlockSpec((None, None, None, t, HEAD_DIM), lambda bb, g, i: (bb, 1, g, 0, 0)),
                  pl.BlockSpec((None, None, TQ, jl), lambda bb, g, i: (bb, g, i, 0)),
                  pl.BlockSpec((nq, jl, TQ), lambda bb, g, i: (0, 0, 0)),
                  pl.BlockSpec((2, NSA_REP, TQ, TQ), lambda bb, g, i: (0, g, 0, 0))],
        out_specs=pl.BlockSpec((None, TQ, NSA_REP * HEAD_DIM), lambda bb, g, i: (bb, i, g)),
        out_shape=jax.ShapeDtypeStruct((b, t, NSA_HEADS * HEAD_DIM), F32),
        scratch_shapes=[pltpu.VMEM((m4, LANES), F32), pltpu.VMEM((m4, LANES), F32),
                        pltpu.VMEM((m4, HEAD_DIM), F32)],
        compiler_params=_cparams(3, 40),
        name="nsa_slc_prompt",
    )(q, kv, kv, sel, emat, btiles)


def _nsa_win_kernel(q_ref, k_ref, v_ref, bt_ref, o_ref, m_scr, l_scr, acc_scr):
    i = pl.program_id(2)
    q4 = q_ref[...].reshape(NSA_REP * TQ, HEAD_DIM)
    _softmax_init(m_scr, l_scr, acc_scr)

    def tile(kt, bias):
        start = pl.multiple_of(kt * TQ, TQ)
        s = _dot_nt(q4, k_ref[pl.ds(start, TQ), :]) + bias
        _softmax_tile_update(s, v_ref[pl.ds(start, TQ), :], m_scr, l_scr, acc_scr)

    @pl.when(i >= WINDOW // TQ)
    def _():
        r = lax.broadcasted_iota(jnp.int32, (NSA_REP * TQ, TQ), 0) % TQ
        c = lax.broadcasted_iota(jnp.int32, (NSA_REP * TQ, TQ), 1)
        tile(i - WINDOW // TQ, jnp.where(c > r, 0.0, NEG_INF))

    @pl.when(i >= 1)
    def _():
        tile(i - 1, bt_ref[1].reshape(NSA_REP * TQ, TQ))

    tile(i, bt_ref[0].reshape(NSA_REP * TQ, TQ))
    o = _softmax_finish(l_scr, acc_scr)
    o_ref[...] = jnp.concatenate([o[r * TQ:(r + 1) * TQ] for r in range(NSA_REP)], axis=1)


def _nsa_win_prompt(q, kv, btiles):
    b, _, t, _ = q.shape
    m4 = NSA_REP * TQ
    return pl.pallas_call(
        _nsa_win_kernel,
        grid=(b, NSA_KV, t // TQ),
        in_specs=[pl.BlockSpec((None, NSA_REP, TQ, HEAD_DIM), lambda bb, g, i: (bb, g, i, 0)),
                  pl.BlockSpec((None, None, None, t, HEAD_DIM), lambda bb, g, i: (bb, 0, g, 0, 0)),
                  pl.BlockSpec((None, None, None, t, HEAD_DIM), lambda bb, g, i: (bb, 1, g, 0, 0)),
                  pl.BlockSpec((2, NSA_REP, TQ, TQ), lambda bb, g, i: (0, g, 0, 0))],
        out_specs=pl.BlockSpec((None, TQ, NSA_REP * HEAD_DIM), lambda bb, g, i: (bb, i, g)),
        out_shape=jax.ShapeDtypeStruct((b, t, NSA_HEADS * HEAD_DIM), F32),
        scratch_shapes=[pltpu.VMEM((m4, LANES), F32), pltpu.VMEM((m4, LANES), F32),
                        pltpu.VMEM((m4, HEAD_DIM), F32)],
        compiler_params=_cparams(3, 40),
        name="nsa_win_prompt",
    )(q, kv, kv, btiles)


def _kmean_kernel(x_ref, o_ref):
    o_ref[...] = jnp.sum(x_ref[...], axis=0, keepdims=True) / MOBA_BLOCK


def _kmean_prompt(proj):
    bx, t, _ = proj.shape
    nblk = t // MOBA_BLOCK
    w = MOBA_HEADS * HEAD_DIM
    out = pl.pallas_call(
        _kmean_kernel,
        grid=(bx, nblk),
        in_specs=[pl.BlockSpec((None, MOBA_BLOCK, w), lambda b, i: (b, i, C_BK // w))],
        out_specs=pl.BlockSpec((None, None, 1, w), lambda b, i: (b, i, 0, 0)),
        out_shape=jax.ShapeDtypeStruct((bx, nblk, 1, w), F32),
        compiler_params=_cparams(2),
        name="moba_kmean_prompt",
    )(proj)
    return out.reshape(bx, nblk, MOBA_HEADS, HEAD_DIM)


def _moba_kernel(q_ref, k_ref, v_ref, km_ref, e_ref, bt_ref, o_ref, m_scr, l_scr, acc_scr, *, nblk, k_m, hb):
    i = pl.program_id(2)
    outs = []
    for hh in range(hb):
        q = q_ref[hh]
        gs = _dot_nt(km_ref[hh], q)
        blk = lax.broadcasted_iota(jnp.int32, gs.shape, 0)
        gs = jnp.where(blk < i, gs, NEG_INF)
        nb = -(-nblk // SUBLANES)
        blocks = [gs[rb * SUBLANES:(rb + 1) * SUBLANES, :] for rb in range(nb)]
        cnt = _rank_rows(blocks, nblk)
        sel = [jnp.where(c < k_m, jnp.where(a > NEG_INF, 1.0, 0.0), 0.0) for c, a in zip(cnt, blocks)]
        sel.append(jnp.zeros((LANES - nb * SUBLANES, TQ), F32))
        sel = jnp.concatenate(sel, axis=0).T.astype(BF16)
        _softmax_init(m_scr, l_scr, acc_scr)

        def tile(kt, bias, masked, hh=hh, q=q, sel=sel):
            start = pl.multiple_of(kt * TQ, TQ)
            s = _dot_nt(q, k_ref[hh, pl.ds(start, TQ), :])
            if bias is not None:
                s = s + bias
            if masked:
                s = jnp.where(_dot(sel, e_ref[kt]) > 0.5, s, NEG_INF)
            _softmax_tile_update(s, v_ref[hh, pl.ds(start, TQ), :], m_scr, l_scr, acc_scr)

        def far_body(kt, c, tile=tile):
            tile(kt, None, True)
            return c

        lax.fori_loop(0, jnp.maximum(i - 1, 0), far_body, 0)

        @pl.when(i >= 1)
        def _(tile=tile, hh=hh):
            tile(i - 1, bt_ref[1, hh], True)

        tile(i, bt_ref[0, hh], False)
        outs.append(_softmax_finish(l_scr, acc_scr))
    o_ref[...] = jnp.concatenate(outs, axis=1)


def _moba_prompt(q, kv, kmean, emat, btiles, k_m):
    b, h, t, _ = q.shape
    hb = 2
    nq = t // TQ
    return pl.pallas_call(
        functools.partial(_moba_kernel, nblk=t // MOBA_BLOCK, k_m=k_m, hb=hb),
        grid=(b, h // hb, nq),
        in_specs=[pl.BlockSpec((None, hb, TQ, HEAD_DIM), lambda bb, hp, i: (bb, hp, i, 0)),
                  pl.BlockSpec((None, None, hb, t, HEAD_DIM), lambda bb, hp, i: (bb, 0, hp, 0, 0)),
                  pl.BlockSpec((None, None, hb, t, HEAD_DIM), lambda bb, hp, i: (bb, 1, hp, 0, 0)),
                  pl.BlockSpec((None, hb, LANES, HEAD_DIM), lambda bb, hp, i: (bb, hp, 0, 0)),
                  pl.BlockSpec((nq, LANES, TQ), lambda bb, hp, i: (0, 0, 0)),
                  pl.BlockSpec((2, hb, TQ, TQ), lambda bb, hp, i: (0, NSA_HEADS // hb + hp, 0, 0))],
        out_specs=pl.BlockSpec((None, TQ, hb * HEAD_DIM), lambda bb, hp, i: (bb, i, hp)),
        out_shape=jax.ShapeDtypeStruct((b, t, h * HEAD_DIM), F32),
        scratch_shapes=[pltpu.VMEM((TQ, LANES), F32), pltpu.VMEM((TQ, LANES), F32),
                        pltpu.VMEM((TQ, HEAD_DIM), F32)],
        compiler_params=_cparams(3, 40),
        name="moba_prompt",
    )(q, kv, kv, kmean, emat, btiles)


def _outproj_kernel(x_ref, gate_ref, az_ref, bz_ref, ma_ref, mb_ref, ag_ref, oc_ref, os_ref, ow_ref, ob_ref,
                    eg_ref, wua_ref, wub_ref, wo_ref, lng_ref, lnb_ref, y_ref, *, alpha):
    w = NSA_HEADS * HEAD_DIM
    g = _sigmoid(ag_ref[...])
    g_hi = g.astype(BF16)
    g_lo = (g - g_hi.astype(F32)).astype(BF16)
    ge = _dot(g_hi, eg_ref[...]) + _dot(g_lo, eg_ref[...])
    o_a = ge[:, 0:w] * oc_ref[...] + ge[:, w:2 * w] * os_ref[...] + ge[:, 2 * w:3 * w] * ow_ref[...]
    y_a = _dot((o_a * _silu(az_ref[...])).astype(BF16), wua_ref[...])
    y_b = _dot((ob_ref[...] * _silu(bz_ref[...])).astype(BF16), wub_ref[...])
    mixed = _dot((_sigmoid(ma_ref[...]) * y_a + _sigmoid(mb_ref[...]) * y_b).astype(BF16), wo_ref[...])
    z = alpha * x_ref[...] + gate_ref[...] * mixed
    mu = jnp.mean(z, axis=-1, keepdims=True)
    var = jnp.mean(jnp.square(z - mu), axis=-1, keepdims=True)
    y_ref[...] = (z - mu) * lax.rsqrt(var + LN_EPS) * lng_ref[...] + lnb_ref[...]


def _outproj(x, gate, proj, o_c, o_s, o_w, o_b, eg, wua, wub, wo, ln_g, ln_b, alpha):
    bx, t, d = x.shape
    tm = min(t, 256)
    nt = t // tm
    w = NSA_HEADS * HEAD_DIM

    def tok(width, col):
        return pl.BlockSpec((None, tm, width), lambda m: (m // nt, m % nt, col))

    def full(a):
        return pl.BlockSpec(a.shape, lambda m: (0,) * a.ndim)

    return pl.pallas_call(
        functools.partial(_outproj_kernel, alpha=alpha),
        grid=(bx * nt,),
        in_specs=[tok(d, 0), _mod_spec(gate, tm, nt),
                  tok(w, C_AZ // w), tok(w, C_BZ // w), tok(d, C_MA // d), tok(d, C_MB // d),
                  tok(AG_PAD, C_AG // AG_PAD), tok(w, 0), tok(w, 0), tok(w, 0), tok(w, 0),
                  full(eg), full(wua), full(wub), full(wo), full(ln_g), full(ln_b)],
        out_specs=tok(d, 0),
        out_shape=jax.ShapeDtypeStruct((bx, t, d), F32),
        compiler_params=_cparams(1, 48),
        name="out_proj",
    )(x, gate, proj, proj, proj, proj, proj, o_c, o_s, o_w, o_b, eg, wua, wub, wo, ln_g, ln_b)


def _dec_cmp_kernel(tab_ref, q_ref, ck_ref, cv_ref, ov_ref, oc_ref, imp_ref, *, pos):
    ncp = ck_ref.shape[1]
    jl = ov_ref.shape[1]
    cend = lax.broadcasted_iota(jnp.int32, (SUBLANES, ncp), 1) * CMP_STRIDE + (CMP_LEN - 1)
    dist = pos - cend
    valid = dist >= 0
    bkt = _bucket(dist)
    j = lax.broadcasted_iota(jnp.int32, (SUBLANES, jl), 1)
    cur = pos // SLC_BLOCK
    for g in range(NSA_KV):
        tab = tab_ref[g]
        bias = jnp.zeros((SUBLANES, ncp), F32)
        for b in range(N_BUCKETS):
            bias = jnp.where(bkt == b, tab[:, b:b + 1], bias)
        s = jnp.where(valid, _dot_nt(q_ref[g], ck_ref[g]) + bias, NEG_INF)
        m = jnp.max(s, axis=1, keepdims=True)
        m = jnp.where(m == NEG_INF, 0.0, m)
        e = jnp.exp(s - m)
        p = (e / jnp.maximum(jnp.sum(e, axis=1, keepdims=True), 1e-30)).astype(BF16)
        oc_ref[g] = _dot(p, cv_ref[g])
        imp4 = _dot(p, ov_ref[...])
        imp = imp4[0:1]
        for r in range(1, NSA_REP):
            imp = imp + imp4[r:r + 1]
        imp = jnp.broadcast_to(imp, (SUBLANES, jl))
        imp = jnp.where((j == 0) | (j == cur) | (j == cur - 1), jnp.inf, imp)
        imp_ref[g] = jnp.where(j <= cur, imp, NEG_INF)


def _dec_cmp(tab_g, q8, ck, cv, overlap, pos):
    db = q8.shape[0]
    ncp = ck.shape[2]
    jl = overlap.shape[1]
    return pl.pallas_call(
        functools.partial(_dec_cmp_kernel, pos=pos),
        grid=(db,),
        in_specs=[pl.BlockSpec(tab_g.shape, lambda b: (0, 0, 0)),
                  pl.BlockSpec((None, NSA_KV, SUBLANES, HEAD_DIM), lambda b: (b, 0, 0, 0)),
                  pl.BlockSpec((None, NSA_KV, ncp, HEAD_DIM), lambda b: (b, 0, 0, 0)),
                  pl.BlockSpec((None, NSA_KV, ncp, HEAD_DIM), lambda b: (b, 0, 0, 0)),
                  pl.BlockSpec((ncp, jl), lambda b: (0, 0))],
        out_specs=[pl.BlockSpec((None, NSA_KV, SUBLANES, HEAD_DIM), lambda b: (b, 0, 0, 0)),
                   pl.BlockSpec((None, NSA_KV, SUBLANES, jl), lambda b: (b, 0, 0, 0))],
        out_shape=[jax.ShapeDtypeStruct((db, NSA_KV, SUBLANES, HEAD_DIM), F32),
                   jax.ShapeDtypeStruct((db, NSA_KV, SUBLANES, jl), F32)],
        compiler_params=_cparams(1),
        name="dec_nsa_cmp",
    )(tab_g, q8, ck, cv, overlap)


def _topk_idx_kernel(s_ref, idx_ref, *, n, k):
    st = s_ref[...].T
    nb = -(-n // SUBLANES)
    blocks = [st[rb * SUBLANES:(rb + 1) * SUBLANES, :] for rb in range(nb)]
    cnt = _rank_rows(blocks, n)
    sub = lax.broadcasted_iota(jnp.int32, blocks[0].shape, 0)
    rows = []
    for r in range(idx_ref.shape[0]):
        if r >= k:
            rows.append(jnp.full((1, st.shape[1]), -1, jnp.int32))
            continue
        acc = jnp.zeros(blocks[0].shape, jnp.int32)
        for rb in range(nb):
            hit = jnp.where(cnt[rb] == float(r), jnp.where(blocks[rb] > NEG_INF, 1, 0), 0)
            acc = acc + hit * (sub + (rb * SUBLANES + 1))
        rows.append(jnp.sum(acc, axis=0, keepdims=True) - 1)
    idx_ref[...] = jnp.concatenate(rows, axis=0)


def _topk_idx(scores, n, k):
    nrow, jl = scores.shape
    kp = -(-k // SUBLANES) * SUBLANES
    return pl.pallas_call(
        functools.partial(_topk_idx_kernel, n=n, k=k),
        grid=(1,),
        in_specs=[pl.BlockSpec((nrow, jl), lambda i: (0, 0))],
        out_specs=pl.BlockSpec((kp, nrow), lambda i: (0, 0)),
        out_shape=jax.ShapeDtypeStruct((kp, nrow), jnp.int32),
        compiler_params=_cparams(1),
        name="topk_idx",
    )(scores)


def _dec_attend(s, v, s_self, v_self):
    m = jnp.maximum(jnp.max(s, axis=1, keepdims=True), s_self)
    e = jnp.exp(s - m)
    e_self = jnp.exp(s_self - m)
    den = jnp.maximum(jnp.sum(e, axis=1, keepdims=True) + e_self, 1e-30)
    return (_dot(e.astype(BF16), v) + e_self * v_self) / den


def _dec_slc_kernel(sel_ref, pt_ref, *refs, pos, n_slc, k_sel):
    blk_refs = refs[:k_sel]
    tab_ref, q_ref, kn_ref, vn_ref, o_ref = refs[k_sel:]
    b, g = pl.program_id(0), pl.program_id(1)
    n_keys = k_sel * SLC_BLOCK
    lane = lax.broadcasted_iota(jnp.int32, (SUBLANES, n_keys), 1)
    kpos = jnp.zeros((SUBLANES, n_keys), jnp.int32)
    for k in range(k_sel):
        jk = sel_ref[(b * NSA_KV + g) * k_sel + k]
        jk = jnp.where((jk >= 0) & (jk < n_slc - 1), jk, -1)
        kpos = jnp.where(lane // SLC_BLOCK == k, jnp.where(jk >= 0, jk * SLC_BLOCK + lane % SLC_BLOCK, pos + 1), kpos)
    dist = pos - kpos
    bkt = _bucket(dist)
    tab = tab_ref[g]
    bias = jnp.zeros((SUBLANES, n_keys), F32)
    for bb in range(N_BUCKETS):
        bias = jnp.where(bkt == bb, tab[:, bb:bb + 1], bias)
    kall = jnp.concatenate([r[:, 0:LANES] for r in blk_refs], axis=0).astype(BF16)
    vall = jnp.concatenate([r[:, LANES:2 * LANES] for r in blk_refs], axis=0).astype(BF16)
    q = q_ref[...]
    s = jnp.where(dist >= 0, _dot_nt(q, kall) + bias, NEG_INF)
    s_self = jnp.sum(q.astype(F32) * kn_ref[...], axis=1, keepdims=True) + tab[:, 0:1]
    o_ref[...] = _dec_attend(s, vall, s_self, vn_ref[...])


def _dec_slc(sel_flat, pt_flat, cache, layer, tab_g, qx, k_new, v_new, pos, n_slc, k_sel, n_pages):
    db = qx.shape[0]
    half = PAGE_SIZE // SLC_BLOCK

    def blk_spec(k):
        def imap(b, g, sel, pt):
            j = jnp.clip(sel[(b * NSA_KV + g) * k_sel + k], 0, n_slc - 2)
            return (pt[b * n_pages + j // half], layer, j % half, 0)
        return pl.BlockSpec((None, None, SLC_BLOCK, 256), imap)

    return pl.pallas_call(
        functools.partial(_dec_slc_kernel, pos=pos, n_slc=n_slc, k_sel=k_sel),
        grid_spec=pltpu.PrefetchScalarGridSpec(
            num_scalar_prefetch=2,
            grid=(db, NSA_KV),
            in_specs=[blk_spec(k) for k in range(k_sel)] + [
                pl.BlockSpec(tab_g.shape, lambda b, g, sel, pt: (0, 0, 0)),
                pl.BlockSpec((None, None, SUBLANES, LANES), lambda b, g, sel, pt: (b, g, 0, 0)),
                pl.BlockSpec((None, 1, LANES), lambda b, g, sel, pt: (b, 0, 0)),
                pl.BlockSpec((None, 1, LANES), lambda b, g, sel, pt: (b, 0, 0))],
            out_specs=pl.BlockSpec((None, None, SUBLANES, LANES), lambda b, g, sel, pt: (b, g, 0, 0))),
        out_shape=jax.ShapeDtypeStruct((db, NSA_KV, SUBLANES, LANES), F32),
        compiler_params=_cparams(2),
        name="dec_nsa_slc",
    )(sel_flat, pt_flat, *([cache] * k_sel), tab_g, qx, k_new, v_new)


def _dec_win_kernel(tab_ref, q_ref, st_ref, kn_ref, vn_ref, o_ref):
    n = st_ref.shape[0]
    dist = n - lax.broadcasted_iota(jnp.int32, (SUBLANES, n), 1)
    bkt = _bucket(dist)
    tab = tab_ref[...]
    bias = jnp.zeros((SUBLANES, n), F32)
    for bb in range(N_BUCKETS):
        bias = jnp.where(bkt == bb, tab[:, bb:bb + 1], bias)
    q = q_ref[...]
    s = jnp.where(dist < WINDOW, _dot_nt(q, st_ref[:, 0:LANES].astype(BF16)) + bias, NEG_INF)
    s_self = jnp.sum(q.astype(F32) * kn_ref[...], axis=1, keepdims=True) + tab[:, 0:1]
    o_ref[...] = _dec_attend(s, st_ref[:, LANES:2 * LANES].astype(BF16), s_self, vn_ref[...])


def _dec_win(tab8, qx, state, layer, k_new, v_new):
    db = qx.shape[0]
    n = state.shape[2]
    return pl.pallas_call(
        _dec_win_kernel,
        grid=(db,),
        in_specs=[pl.BlockSpec(tab8.shape, lambda b: (0, 0)),
                  pl.BlockSpec((None, NSA_HEADS, LANES), lambda b: (b, 0, 0)),
                  pl.BlockSpec((None, None, n, 256), lambda b: (layer, b, 0, 0)),
                  pl.BlockSpec((None, 1, LANES), lambda b: (b, 0, 0)),
                  pl.BlockSpec((None, 1, LANES), lambda b: (b, 0, 0))],
        out_specs=pl.BlockSpec((None, NSA_HEADS, LANES), lambda b: (b, 0, 0)),
        out_shape=jax.ShapeDtypeStruct((db, NSA_HEADS, LANES), F32),
        compiler_params=_cparams(1),
        name="dec_nsa_win",
    )(tab8, qx, state, k_new, v_new)


def _page_sum_kernel(pt_ref, *refs, n_x):
    o_ref = refs[n_x]
    for k in range(n_x):
        o_ref[k:k + 1, :] = jnp.sum(refs[k][...], axis=0, keepdims=True)


def _dec_moba_pagesums(cache, page_table, layer):
    db, n_pages = page_table.shape
    pps = PAGES_PER_STEP
    w = MOBA_HEADS * HEAD_DIM

    def page_spec(k):
        return pl.BlockSpec((None, None, PAGE_SIZE, w), lambda b, i, pt: (pt[b * n_pages + i * pps + k], layer, 0, 0))

    return pl.pallas_call(
        functools.partial(_page_sum_kernel, n_x=pps),
        grid_spec=pltpu.PrefetchScalarGridSpec(
            num_scalar_prefetch=1,
            grid=(db, n_pages // pps),
            in_specs=[page_spec(k) for k in range(pps)],
            out_specs=pl.BlockSpec((None, pps, w), lambda b, i, pt: (b, i, 0))),
        out_shape=jax.ShapeDtypeStruct((db, n_pages, w), F32),
        compiler_params=_cparams(2),
        name="dec_moba_pagesums",
    )(page_table.reshape(-1), *([cache] * pps))


def _dec_moba_gate_kernel(q_ref, ps_ref, gs_ref, *, nblk):
    ppb = MOBA_BLOCK // PAGE_SIZE
    w = MOBA_HEADS * HEAD_DIM
    km = ps_ref[:, 0:w]
    for k in range(1, ppb):
        km = km + ps_ref[:, k * w:(k + 1) * w]
    km = (km / MOBA_BLOCK).astype(BF16)
    gs = _dot_nt(q_ref[...], km)
    gs_ref[...] = jnp.full(gs_ref.shape, NEG_INF, F32)
    gs_ref[:, 0:nblk] = gs


def _dec_moba_gate(qbd, pagesums, nblk):
    db, n_pages, w = pagesums.shape
    ppb = n_pages // nblk
    pagesums = pagesums.reshape(db, nblk, ppb * w)
    return pl.pallas_call(
        functools.partial(_dec_moba_gate_kernel, nblk=nblk),
        grid=(db,),
        in_specs=[pl.BlockSpec((None, MOBA_HEADS, w), lambda b: (b, 0, 0)),
                  pl.BlockSpec((None, nblk, ppb * w), lambda b: (b, 0, 0))],
        out_specs=pl.BlockSpec((None, MOBA_HEADS, LANES), lambda b: (b, 0, 0)),
        out_shape=jax.ShapeDtypeStruct((db, MOBA_HEADS, LANES), F32),
        compiler_params=_cparams(1),
        name="dec_moba_gate",
    )(qbd, pagesums)


def _dec_moba_kernel(sel_ref, pt_ref, *refs, pos, k_m):
    ppb = MOBA_BLOCK // PAGE_SIZE
    n_pg = k_m * ppb
    k_refs, v_refs = refs[:n_pg], refs[n_pg:2 * n_pg]
    tab_ref, q_ref, kn_ref, vn_ref, o_ref = refs[2 * n_pg:]
    b, h = pl.program_id(0), pl.program_id(1)
    n_keys = k_m * MOBA_BLOCK
    lane = lax.broadcasted_iota(jnp.int32, (SUBLANES, n_keys), 1)
    kpos = jnp.zeros((SUBLANES, n_keys), jnp.int32)
    for k in range(k_m):
        jk = sel_ref[(b * MOBA_HEADS + h) * k_m + k]
        kpos = jnp.where(lane // MOBA_BLOCK == k, jnp.where(jk >= 0, jk * MOBA_BLOCK + lane % MOBA_BLOCK, pos + 1),
                         kpos)
    dist = pos - kpos
    bkt = _bucket(dist)
    tab = tab_ref[h]
    bias = jnp.zeros((SUBLANES, n_keys), F32)
    for bb in range(N_BUCKETS):
        bias = jnp.where(bkt == bb, tab[:, bb:bb + 1], bias)
    kall = jnp.concatenate([r[...] for r in k_refs], axis=0).astype(BF16)
    vall = jnp.concatenate([r[...] for r in v_refs], axis=0).astype(BF16)
    q = q_ref[...]
    s = jnp.where(dist >= 0, _dot_nt(q, kall) + bias, NEG_INF)
    s_self = jnp.sum(q.astype(F32) * kn_ref[...], axis=1, keepdims=True) + tab[:, 0:1]
    o_ref[...] = _dec_attend(s, vall, s_self, vn_ref[...])


def _dec_moba(sel_flat, pt_flat, cache, layer, tab_h, qx, k_new, v_new, pos, k_m, n_pages, nblk):
    db = qx.shape[0]
    ppb = MOBA_BLOCK // PAGE_SIZE
    hpl = LANES // HEAD_DIM
    kcols = MOBA_HEADS // hpl

    def pg_spec(k, pg, is_v):
        def imap(b, h, sel, pt):
            j = jnp.clip(sel[(b * MOBA_HEADS + h) * k_m + k], 0, nblk - 1)
            return (pt[b * n_pages + j * ppb + pg], layer, 0, h // hpl + (kcols if is_v else 0))
        return pl.BlockSpec((None, None, PAGE_SIZE, LANES), imap)

    kspecs = [pg_spec(k, pg, False) for k in range(k_m) for pg in range(ppb)]
    vspecs = [pg_spec(k, pg, True) for k in range(k_m) for pg in range(ppb)]
    return pl.pallas_call(
        functools.partial(_dec_moba_kernel, pos=pos, k_m=k_m),
        grid_spec=pltpu.PrefetchScalarGridSpec(
            num_scalar_prefetch=2,
            grid=(db, MOBA_HEADS),
            in_specs=kspecs + vspecs + [
                pl.BlockSpec(tab_h.shape, lambda b, h, sel, pt: (0, 0, 0)),
                pl.BlockSpec((None, None, SUBLANES, LANES), lambda b, h, sel, pt: (b, h, 0, 0)),
                pl.BlockSpec((None, None, 1, LANES), lambda b, h, sel, pt: (b, h // hpl, 0, 0)),
                pl.BlockSpec((None, None, 1, LANES), lambda b, h, sel, pt: (b, h // hpl, 0, 0))],
            out_specs=pl.BlockSpec((None, None, SUBLANES, LANES), lambda b, h, sel, pt: (b, h, 0, 0))),
        out_shape=jax.ShapeDtypeStruct((db, MOBA_HEADS, SUBLANES, LANES), F32),
        compiler_params=_cparams(2),
        name="dec_moba",
    )(sel_flat, pt_flat, *([cache] * (2 * k_m * ppb)), tab_h, qx, k_new, v_new)


def _overlap_matrix(n_cmp, n_slc, rows, cols):
    i = np.arange(n_cmp)[:, None]
    j = np.arange(n_slc)[None, :]
    units = SLC_BLOCK // CMP_STRIDE
    m = sum(((i + u) // units == j).astype(np.float32) for u in range(CMP_LEN // CMP_STRIDE))
    out = np.zeros((rows, cols), np.float32)
    out[:n_cmp, :n_slc] = m
    return jnp.asarray(out, dtype=BF16)


def _block_membership(t, block, rows):
    key = np.arange(t).reshape(t // TQ, 1, TQ)
    j = np.arange(rows).reshape(1, rows, 1)
    return jnp.asarray((key // block == j).astype(np.float32), dtype=BF16)


def _gate_expand_matrix():
    w = NSA_HEADS * HEAD_DIM
    e = np.zeros((AG_PAD, 3 * w), np.float32)
    for h in range(NSA_HEADS):
        for br in range(3):
            e[h * 3 + br, br * w + h * HEAD_DIM: br * w + (h + 1) * HEAD_DIM] = 1.0
    return jnp.asarray(e, dtype=BF16)


def _reorder_w_in(w_in):
    scale = HEAD_DIM ** -0.5
    o = np.cumsum([0, 512, 128, 128, 128, 128, 128, 128, 24, 512, 512, 512, 512, 512, 1024, 1024])
    a_q, kv3, a_g, a_z, b_q, b_kv, b_z, m_ab = (
        w_in[..., o[0]:o[1]], w_in[..., o[1]:o[7]], w_in[..., o[7]:o[8]], w_in[..., o[8]:o[9]],
        w_in[..., o[9]:o[10]], w_in[..., o[10]:o[12]], w_in[..., o[12]:o[13]], w_in[..., o[13]:o[15]])
    pad = jnp.zeros(w_in.shape[:-1] + (AG_PAD - a_g.shape[-1],), w_in.dtype)
    return jnp.concatenate([a_q * scale, a_z, b_q * scale, b_z, b_kv, m_ab, kv3, a_g, pad], axis=-1).astype(BF16)


def _cmp_stage1_weights(w1):
    w1r = w1.reshape(2, CMP_STRIDE, HEAD_DIM, HEAD_DIM)
    eye = jnp.eye(NSA_KV, dtype=w1.dtype)
    w = jnp.einsum("hlde,gf->lgdhfe", w1r, eye)
    return w.reshape(CMP_STRIDE * NSA_KV * HEAD_DIM, 2 * NSA_KV * HEAD_DIM).astype(BF16)


def _cmp_stage2_weights(w2):
    z = jnp.zeros_like(w2)
    return jnp.stack([jnp.concatenate([w2, z], axis=1), jnp.concatenate([z, w2], axis=1)], axis=1).astype(BF16)


def _head_major(cols, heads):
    b, t, _ = cols.shape
    return cols.reshape(b, t, 2, heads, HEAD_DIM).transpose(0, 2, 3, 1, 4).astype(BF16)


def _q_head_major(cols, heads):
    b, t, _ = cols.shape
    return cols.reshape(b, t, heads, HEAD_DIM).transpose(0, 2, 1, 3).astype(BF16)


def _layer_prompt(x, mod, proj_w, lw, consts):
    b, t, d = x.shape
    shift, scale, gate = mod
    proj = _inproj(x, scale, shift, proj_w)
    n_slc = t // SLC_BLOCK
    k_sel = min(SLC_TOPK, n_slc)
    abk, abv = _cmp_proj_prompt(proj, lw["cmp_wk"], lw["cmp_wv"])
    ck, cv = _cmp_mlp(abk, abv, lw["pos_flat"], lw["phi_w1"], lw["phi_b1"], lw["cmp_w2e"], lw["phi_b2"])
    q_a = _q_head_major(proj[..., C_AQ:C_AQ + 512], NSA_HEADS)
    o_c, sel = _nsa_cmp_prompt(consts["rel_bias"], q_a, ck, cv, consts["overlap_p"], n_slc, k_sel)
    slc_hm = _head_major(proj[..., C_SLC:C_SLC + 256], NSA_KV)
    o_s = _nsa_slc_prompt(q_a, slc_hm, sel, consts["e_slc"], consts["btiles"])
    win_hm = _head_major(proj[..., C_WIN:C_WIN + 256], NSA_KV)
    o_w = _nsa_win_prompt(q_a, win_hm, consts["btiles"])
    nblk = t // MOBA_BLOCK
    kmean = _kmean_prompt(proj).transpose(0, 2, 1, 3)
    kmean = jnp.pad(kmean, ((0, 0), (0, 0), (0, LANES - nblk), (0, 0))).astype(BF16)
    q_b = _q_head_major(proj[..., C_BQ:C_BQ + 512], MOBA_HEADS)
    moba_hm = _head_major(proj[..., C_BK:C_BK + 1024], MOBA_HEADS)
    o_b = _moba_prompt(q_b, moba_hm, kmean, consts["e_moba"], consts["btiles"], min(MOBA_TOPK, nblk - 1))
    y = _outproj(x, gate, proj, o_c, o_s, o_w, o_b, consts["eg"], lw["w_up_a"], lw["w_up_b"], lw["w_out"],
                 lw["ln_g"], lw["ln_b"], consts["alpha"])
    new = dict(
        cmp=proj[..., C_CMP:C_CMP + 256].reshape(b, t, 2, NSA_KV, HEAD_DIM),
        slc=proj[..., C_SLC:C_SLC + 256].reshape(b, t, 2, NSA_KV, HEAD_DIM),
        win=proj[:, t - min(WINDOW, t):, C_WIN:C_WIN + 256].reshape(b, min(WINDOW, t), 2, NSA_KV, HEAD_DIM),
        moba=proj[..., C_BK:C_BK + 1024].reshape(b, t, 2, MOBA_HEADS, HEAD_DIM))
    return y, new


def _layer_sample(x, mod, proj_w, lw, consts, layer, caches, page_table):
    _, db, d = x.shape
    shift, scale, gate = mod
    cache_cmp, cache_slc, cache_moba, state_win = caches
    n_pages = page_table.shape[1]
    pos = n_pages * PAGE_SIZE
    proj = _inproj(x, scale, shift, proj_w)
    p2 = proj[0]
    pt_flat = page_table.reshape(-1)
    abk, abv = _cmp_proj_paged(cache_cmp, page_table, layer, lw["cmp_wk"], lw["cmp_wv"])
    ck, cv = _cmp_mlp(abk, abv, lw["pos_flat"], lw["phi_w1"], lw["phi_b1"], lw["cmp_w2e"], lw["phi_b2"])
    q_a = p2[:, C_AQ:C_AQ + 512].reshape(db, NSA_KV, NSA_REP, HEAD_DIM)
    q8 = jnp.pad(q_a, ((0, 0), (0, 0), (0, SUBLANES - NSA_REP), (0, 0))).astype(BF16)
    n_slc = pos // SLC_BLOCK + 1
    k_sel = min(SLC_TOPK, n_slc)
    o_c8, imp = _dec_cmp(consts["tab_g"], q8, ck, cv, consts["overlap_s"], pos)
    o_c = o_c8[:, :, :NSA_REP].reshape(db, NSA_HEADS * HEAD_DIM)
    imp2 = imp[:, :, 0].reshape(db * NSA_KV, -1)
    imp2 = jnp.pad(imp2, ((0, LANES - db * NSA_KV), (0, 0)), constant_values=NEG_INF)
    sel = _topk_idx(imp2, n_slc, k_sel)[:k_sel, :db * NSA_KV].T.reshape(-1)
    lane_g = (jnp.arange(LANES) // HEAD_DIM)[None, None, None, :]
    qx = jnp.where(lane_g == jnp.arange(NSA_KV)[None, :, None, None], jnp.tile(q8.astype(F32), (1, 1, 1, NSA_KV)), 0.0)
    qx = qx.astype(BF16)
    slc_new = p2[:, C_SLC:C_SLC + 256].reshape(db, 2, 1, LANES)
    o_s8 = _dec_slc(sel, pt_flat, cache_slc, layer, consts["tab_g"], qx, slc_new[:, 0], slc_new[:, 1],
                    pos, n_slc, k_sel, n_pages)
    o_s = jnp.stack([o_s8[:, g, :NSA_REP, g * HEAD_DIM:(g + 1) * HEAD_DIM] for g in range(NSA_KV)], axis=1)
    o_s = o_s.reshape(db, NSA_HEADS * HEAD_DIM)
    qx_w = qx[:, :, :NSA_REP].reshape(db, NSA_HEADS, LANES)
    win_new = p2[:, C_WIN:C_WIN + 256].reshape(db, 2, 1, LANES)
    o_w8 = _dec_win(consts["tab8"], qx_w, state_win, layer, win_new[:, 0], win_new[:, 1])
    o_w = jnp.stack([o_w8[:, h, (h // NSA_REP) * HEAD_DIM:(h // NSA_REP + 1) * HEAD_DIM] for h in range(NSA_HEADS)],
                    axis=1).reshape(db, NSA_HEADS * HEAD_DIM)
    nblk = pos // MOBA_BLOCK
    k_m = min(MOBA_TOPK, nblk)
    q_b = p2[:, C_BQ:C_BQ + 512]
    w = MOBA_HEADS * HEAD_DIM
    head_of_lane = (jnp.arange(w) // HEAD_DIM)[None, None, :]
    qbd = jnp.where(head_of_lane == jnp.arange(MOBA_HEADS)[None, :, None], q_b[:, None, :], 0.0).astype(BF16)
    pagesums = _dec_moba_pagesums(cache_moba, page_table, layer)
    gs = _dec_moba_gate(qbd, pagesums, nblk).reshape(db * MOBA_HEADS, LANES)
    sel_m = _topk_idx(gs, nblk, k_m)[:k_m].T.reshape(-1)
    hpl = LANES // HEAD_DIM
    qh = q_b.reshape(db, MOBA_HEADS, 1, HEAD_DIM)
    lane_half = (jnp.arange(LANES) // HEAD_DIM)[None, None, None, :]
    qx_m = jnp.where(lane_half == (jnp.arange(MOBA_HEADS) % hpl)[None, :, None, None],
                     jnp.tile(qh, (1, 1, SUBLANES, hpl)), 0.0)
    qx_m = jnp.where(jnp.arange(SUBLANES)[None, None, :, None] == 0, qx_m, 0.0).astype(BF16)
    moba_new = p2[:, C_BK:C_BK + 1024].reshape(db, 2, MOBA_HEADS // hpl, 1, LANES)
    o_b8 = _dec_moba(sel_m, pt_flat, cache_moba, layer, consts["tab_h"], qx_m, moba_new[:, 0], moba_new[:, 1],
                     pos, k_m, n_pages, nblk)
    o_b = jnp.stack([o_b8[:, h, 0, (h % hpl) * HEAD_DIM:(h % hpl + 1) * HEAD_DIM] for h in range(MOBA_HEADS)],
                    axis=1).reshape(db, w)
    y = _outproj(x, gate, proj, o_c[None], o_s[None], o_w[None], o_b[None], consts["eg"], lw["w_up_a"],
                 lw["w_up_b"], lw["w_out"], lw["ln_g"], lw["ln_b"], consts["alpha"])
    new = dict(
        cmp=p2[:, C_CMP:C_CMP + 256].reshape(db, 1, 2, NSA_KV, HEAD_DIM),
        slc=p2[:, C_SLC:C_SLC + 256].reshape(db, 1, 2, NSA_KV, HEAD_DIM),
        win=p2[:, C_WIN:C_WIN + 256].reshape(db, 1, 2, NSA_KV, HEAD_DIM),
        moba=p2[:, C_BK:C_BK + 1024].reshape(db, 1, 2, MOBA_HEADS, HEAD_DIM))
    return y, new


def kernel(x_prompt, x_sample, cache_nsa_cmp, cache_nsa_slc, cache_moba, state_nsa_win, page_table, c_prompt, c_sample, rel_bias, w_ada, b_ada, w_in, phi_pos, phi_w1, phi_b1, phi_w2, phi_b2, w_up_a, w_up_b, w_out, ln_g, ln_b):
    b, t, d = x_prompt.shape
    db = x_sample.shape[0]
    depth = w_ada.shape[0]
    n_phys = cache_nsa_cmp.shape[0]
    n_pages = page_table.shape[1]
    pos = n_pages * PAGE_SIZE
    assert x_sample.shape[1] == 1 and t % TQ == 0 and t >= WINDOW and n_pages % PAGES_PER_STEP == 0
    assert db * NSA_KV <= LANES and state_nsa_win.shape[2] == WINDOW and pos // MOBA_BLOCK >= 1

    mc = -(-(b + db) // SUBLANES) * SUBLANES
    c_all = jnp.pad(jnp.concatenate([c_prompt, c_sample], axis=0), ((0, mc - b - db), (0, 0)))
    mod = _ada(c_all, w_ada, b_ada)

    proj_w = _reorder_w_in(w_in)
    far = rel_bias[N_BUCKETS - 1]
    tab_rel = (rel_bias - far[None, :]).T
    tab_g = jnp.pad(tab_rel[:NSA_HEADS].reshape(NSA_KV, NSA_REP, N_BUCKETS),
                    ((0, 0), (0, SUBLANES - NSA_REP), (0, 0)))
    n_slc_s = pos // SLC_BLOCK + 1
    consts = dict(
        rel_bias=rel_bias,
        alpha=float((2 * depth) ** 0.25),
        btiles=_bias_tiles(rel_bias),
        overlap_p=_overlap_matrix(t // CMP_STRIDE - 1, t // SLC_BLOCK, t // CMP_STRIDE, LANES),
        overlap_s=_overlap_matrix(pos // CMP_STRIDE - 1, n_slc_s, pos // CMP_STRIDE, -(-n_slc_s // LANES) * LANES),
        e_slc=_block_membership(t, SLC_BLOCK, LANES),
        e_moba=_block_membership(t, MOBA_BLOCK, LANES),
        eg=_gate_expand_matrix(),
        tab_g=tab_g,
        tab8=tab_rel[:NSA_HEADS],
        tab_h=jnp.broadcast_to(tab_rel[NSA_HEADS:, None, :], (MOBA_HEADS, SUBLANES, N_BUCKETS)),
    )
    caches = (cache_nsa_cmp.reshape(n_phys, depth, PAGE_SIZE, 256),
              cache_nsa_slc.reshape(n_phys, depth, PAGE_SIZE, 256),
              cache_moba.reshape(n_phys, depth, PAGE_SIZE, 2 * MOBA_HEADS * HEAD_DIM),
              state_nsa_win.reshape(depth, db, WINDOW, 256))

    yp, ys = x_prompt, x_sample.reshape(1, db, d)
    new_p, new_s = [], []
    for l in range(depth):
        lw = dict(
            cmp_wk=_cmp_stage1_weights(phi_w1[l, 0]), cmp_wv=_cmp_stage1_weights(phi_w1[l, 1]),
            pos_flat=phi_pos[l].reshape(2, 1, CMP_LEN * HEAD_DIM),
            phi_w1=phi_w1[l], phi_b1=phi_b1[l].reshape(2, 1, HEAD_DIM),
            cmp_w2e=_cmp_stage2_weights(phi_w2[l]), phi_b2=phi_b2[l].reshape(2, 1, HEAD_DIM),
            w_up_a=w_up_a[l].astype(BF16), w_up_b=w_up_b[l].astype(BF16), w_out=w_out[l].astype(BF16),
            ln_g=ln_g[l].reshape(1, d), ln_b=ln_b[l].reshape(1, d))
        shift, scale, gate = jnp.split(mod[l], 3, axis=-1)
        mod_p = tuple(a[:b, None, :] for a in (shift, scale, gate))
        mod_s = tuple(a[None, b:b + db, :] for a in (shift, scale, gate))
        yp, np_ = _layer_prompt(yp, mod_p, proj_w[l], lw, consts)
        ys, ns_ = _layer_sample(ys, mod_s, proj_w[l], lw, consts, l, caches, page_table)
        new_p.append(np_)
        new_s.append(ns_)

    def stack(items, key, axis):
        return jnp.stack([it[key] for it in items], axis=axis)

    return (yp, ys.reshape(db, 1, d),
            stack(new_p, "cmp", 1), stack(new_s, "cmp", 1),
            stack(new_p, "slc", 1), stack(new_s, "slc", 1),
            stack(new_p, "moba", 1), stack(new_s, "moba", 1),
            stack(new_p, "win", 0), stack(new_s, "win", 0))
```

```python
import functools
import math

import numpy as np
import jax
import jax.numpy as jnp
from jax import lax
from jax.experimental import pallas as pl
from jax.experimental.pallas import tpu as pltpu

F32 = jnp.float32
BF16 = jnp.bfloat16
NEG_INF = float("-inf")
MASK_BIG = 2.0 ** 127

HEAD_DIM = 64
NSA_HEADS = 8
NSA_KV = 2
NSA_REP = NSA_HEADS // NSA_KV
CMP_LEN = 32
CMP_STRIDE = 16
SLC_BLOCK = 64
SLC_TOPK = 16
WINDOW = 512
MOBA_HEADS = 8
MOBA_BLOCK = 256
MOBA_TOPK = 3
N_HEADS = NSA_HEADS + MOBA_HEADS
N_BUCKETS = 32
MAX_EXACT = N_BUCKETS // 2
MAX_DISTANCE = 128
LN_EPS = 1e-5
PAGE_SIZE = 128

LANES = 128
SUBLANES = 8
TQ = 256
AUG_W = 256
MOBA_HB = 4
PAGES_PER_STEP = 16

NSA_W = NSA_HEADS * HEAD_DIM
MOBA_W = MOBA_HEADS * HEAD_DIM
KV_W = 2 * NSA_KV * HEAD_DIM
MOBA_KV_W = 2 * MOBA_W
Z_AZ, Z_BZ, Z_MA, Z_MB, Z_AG = 0, 512, 1024, 2048, 3072
AG_PAD = 128
Z_W = Z_AG + AG_PAD


def _cparams(n_axes, vmem_mb=None):
    kw = dict(dimension_semantics=("arbitrary",) * n_axes)
    if vmem_mb is not None:
        kw["vmem_limit_bytes"] = vmem_mb * 1024 * 1024
    return pltpu.CompilerParams(**kw)


def _dot(a, b):
    return jnp.dot(a, b, preferred_element_type=F32)


def _dot_nt(a, b):
    return lax.dot_general(a, b, (((1,), (1,)), ((), ())), preferred_element_type=F32)


def _sigmoid(x):
    return 1.0 / (1.0 + jnp.exp(-x))


def _silu(x):
    return x * _sigmoid(x)


def _bucket(dist):
    n = jnp.maximum(dist, 0)
    nf = jnp.maximum(n, 1).astype(F32)
    large = MAX_EXACT + (jnp.log(nf / MAX_EXACT) / math.log(MAX_DISTANCE / MAX_EXACT)
                         * (N_BUCKETS - MAX_EXACT)).astype(jnp.int32)
    return jnp.where(n < MAX_EXACT, n, jnp.minimum(large, N_BUCKETS - 1))


def _bias_rows(bkt, tab):
    bias = jnp.zeros(bkt.shape, F32)
    for b in range(N_BUCKETS):
        bias = jnp.where(bkt == b, tab[:, b:b + 1], bias)
    return bias


def _rank_rows(blocks, n):
    nb = len(blocks)
    cnt = [jnp.zeros(blocks[0].shape, F32) for _ in range(nb)]
    sub = lax.broadcasted_iota(jnp.int32, blocks[0].shape, 0)
    for jp in range(n):
        rb0, r0 = divmod(jp, SUBLANES)
        row = blocks[rb0][r0:r0 + 1, :]
        for rb in range(nb):
            a = blocks[rb]
            if rb < rb0:
                ahead = jnp.where(row > a, 1.0, 0.0)
            elif rb > rb0:
                ahead = jnp.where(row >= a, 1.0, 0.0)
            else:
                ahead = jnp.where(sub > r0, jnp.where(row >= a, 1.0, 0.0), jnp.where(row > a, 1.0, 0.0))
            cnt[rb] = cnt[rb] + ahead
    return cnt


def _topk_drop_mask(score, n, k):
    q, jl = score.shape
    st = score.T
    nb = -(-n // SUBLANES)
    blocks = [st[rb * SUBLANES:(rb + 1) * SUBLANES, :] for rb in range(nb)]
    cnt = _rank_rows(blocks, n)
    drop = [jnp.where(c < k, jnp.where(a > NEG_INF, 0.0, 1.0), 1.0) for c, a in zip(cnt, blocks)]
    if nb * SUBLANES < jl:
        drop.append(jnp.ones((jl - nb * SUBLANES, q), F32))
    return jnp.concatenate(drop, axis=0).T


def _augment_query(q, drop):
    pad = jnp.zeros((q.shape[0], AUG_W - q.shape[1] - drop.shape[1]), F32)
    return jnp.concatenate([q.astype(F32), drop.astype(F32), pad], axis=1).astype(BF16)


def _softmax_init(m_scr, l_scr, acc_scr):
    m_scr[...] = jnp.full(m_scr.shape, -MASK_BIG, F32)
    l_scr[...] = jnp.zeros(l_scr.shape, F32)
    acc_scr[...] = jnp.zeros(acc_scr.shape, F32)


def _flash_tile(qs, k_ts, v_ts, states, adds=None):
    n_s = len(qs)
    ss = [_dot(qs[c], k_ts[c]) for c in range(n_s)]
    ps, alphas = [], []
    for c in range(n_s):
        s = ss[c]
        if adds is not None:
            s = s + adds[c]
        m_scr, l_scr, _ = states[c]
        m_prev = m_scr[...]
        m_next = jnp.maximum(m_prev, jnp.max(s, axis=1, keepdims=True))
        alpha = jnp.exp(m_prev - m_next)
        p = jnp.exp(s - jnp.concatenate([m_next] * (s.shape[1] // LANES), axis=1))
        l_scr[...] = alpha * l_scr[...] + jnp.sum(p, axis=1, keepdims=True)
        m_scr[...] = m_next
        ps.append(p.astype(BF16))
        alphas.append(alpha)
    for c in range(n_s):
        acc_scr = states[c][2]
        acc_scr[...] = acc_scr[...] * alphas[c][:, :HEAD_DIM] + _dot_nt(ps[c], v_ts[c])


def _flash_scratch(n_streams):
    per = [pltpu.VMEM((TQ, LANES), F32), pltpu.VMEM((TQ, LANES), F32), pltpu.VMEM((TQ, HEAD_DIM), F32)]
    return per * n_streams


def _flash_states(scr):
    return [tuple(scr[3 * c:3 * c + 3]) for c in range(len(scr) // 3)]


def _softmax_finish(l_scr, acc_scr):
    return acc_scr[...] / jnp.maximum(l_scr[...], 1e-30)[:, :HEAD_DIM]


def _full_spec(a, n_grid, single=True):
    kw = dict(pipeline_mode=pl.Buffered(1)) if single else {}
    return pl.BlockSpec(a.shape, lambda *_: (0,) * a.ndim, **kw)


def _ada_kernel(c_ref, w_ref, b_ref, o_ref):
    a = _silu(c_ref[...]).astype(BF16)
    o_ref[...] = _dot(a, w_ref[...].astype(BF16)) + b_ref[...]


def _ada(c_all, w_ada, b_ada):
    depth, d, n3 = w_ada.shape
    mc = c_all.shape[0]
    tn = 1024
    return pl.pallas_call(
        _ada_kernel,
        grid=(depth, n3 // tn),
        in_specs=[pl.BlockSpec((mc, d), lambda l, j: (0, 0)),
                  pl.BlockSpec((None, d, tn), lambda l, j: (l, 0, j)),
                  pl.BlockSpec((None, 1, tn), lambda l, j: (l, 0, j))],
        out_specs=pl.BlockSpec((None, mc, tn), lambda l, j: (l, 0, j)),
        out_shape=jax.ShapeDtypeStruct((depth, mc, n3), F32),
        compiler_params=_cparams(2, 40),
        name="ada_mod",
    )(c_all, w_ada, b_ada.reshape(depth, 1, n3))


def _inproj_kernel(x_ref, sc_ref, sh_ref, wq_ref, wz_ref, wkv_ref, q_ref, z_ref, cmp_ref, slc_ref, win_ref,
                   moba_ref):
    h = (x_ref[...] * (1.0 + sc_ref[...]) + sh_ref[...]).astype(BF16)
    q = _dot(h, wq_ref[...])
    for hd in range(N_HEADS):
        q_ref[hd] = q[:, hd * HEAD_DIM:(hd + 1) * HEAD_DIM].astype(BF16)
    z_ref[...] = _dot(h, wz_ref[...])
    r = 0
    for o_ref in (cmp_ref, slc_ref, win_ref, moba_ref):
        n = o_ref.shape[0]
        o_ref[...] = _dot_nt(wkv_ref[r:r + n, :], h)
        r += n


def _mod_spec(mod, tm, nt):
    if mod.shape[1] == 1:
        return pl.BlockSpec((None, 1, mod.shape[2]), lambda m: (m // nt, 0, 0))
    return pl.BlockSpec((None, tm, mod.shape[2]), lambda m: (m // nt, m % nt, 0))


def _inproj(x, scale, shift, wq, wz, wkv):
    bx, t, d = x.shape
    tm = min(t, 512)
    nt = t // tm

    def kv_spec(rows):
        return pl.BlockSpec((None, rows, tm), lambda m: (m // nt, 0, m % nt))

    def kv_shape(rows):
        return jax.ShapeDtypeStruct((bx, rows, t), F32)

    return pl.pallas_call(
        _inproj_kernel,
        grid=(bx * nt,),
        in_specs=[pl.BlockSpec((None, tm, d), lambda m: (m // nt, m % nt, 0)),
                  _mod_spec(scale, tm, nt), _mod_spec(shift, tm, nt),
                  _full_spec(wq, 1), _full_spec(wz, 1), _full_spec(wkv, 1)],
        out_specs=[pl.BlockSpec((None, N_HEADS, tm, HEAD_DIM), lambda m: (m // nt, 0, m % nt, 0)),
                   pl.BlockSpec((None, tm, Z_W), lambda m: (m // nt, m % nt, 0)),
                   kv_spec(KV_W), kv_spec(KV_W), kv_spec(KV_W), kv_spec(MOBA_KV_W)],
        out_shape=[jax.ShapeDtypeStruct((bx, N_HEADS, t, HEAD_DIM), BF16),
                   jax.ShapeDtypeStruct((bx, t, Z_W), F32),
                   kv_shape(KV_W), kv_shape(KV_W), kv_shape(KV_W), kv_shape(MOBA_KV_W)],
        compiler_params=_cparams(1, 56),
        name="in_proj",
    )(x, scale, shift, wq, wz, wkv)


def _cmp_proj_kernel(*refs, n_x, n_prefetch=0):
    refs = refs[n_prefetch:]
    perm_ref, wk_ref, wv_ref, abk_ref, abv_ref = refs[2 * n_x:]
    perm = perm_ref[...]
    for x_refs, w_ref, ab_ref in ((refs[:n_x], wk_ref, abk_ref), (refs[n_x:2 * n_x], wv_ref, abv_ref)):
        rows = []
        for x_ref in x_refs:
            for c in range(x_ref.shape[1] // LANES):
                x_t = x_ref[:, c * LANES:(c + 1) * LANES].astype(BF16)
                xp = _dot_nt(perm, x_t)
                rows.append(jnp.concatenate(
                    [xp[l * SUBLANES:(l + 1) * SUBLANES] for l in range(CMP_STRIDE)], axis=1))
        xr = jnp.concatenate(rows, axis=0).astype(BF16)
        ab_ref[...] = _dot(xr, w_ref[...])


def _cmp_proj_prompt(cmp_t, perm, wk, wv):
    bx, _, t = cmp_t.shape
    tc = min(t, PAGES_PER_STEP * LANES)
    half = KV_W // 2
    m = tc // CMP_STRIDE
    ospec = pl.BlockSpec((None, m, 256), lambda b, i: (b, i, 0))
    oshape = jax.ShapeDtypeStruct((bx, t // CMP_STRIDE, 256), F32)
    return pl.pallas_call(
        functools.partial(_cmp_proj_kernel, n_x=1),
        grid=(bx, t // tc),
        in_specs=[pl.BlockSpec((None, half, tc), lambda b, i: (b, 0, i)),
                  pl.BlockSpec((None, half, tc), lambda b, i: (b, 1, i)),
                  _full_spec(perm, 2), _full_spec(wk, 2), _full_spec(wv, 2)],
        out_specs=[ospec, ospec],
        out_shape=[oshape, oshape],
        compiler_params=_cparams(2),
        name="cmp_proj_prompt",
    )(cmp_t, cmp_t, perm, wk, wv)


def _cmp_proj_paged(cache_t, page_table, layer, perm, wk, wv):
    db, n_pages = page_table.shape
    pps = PAGES_PER_STEP
    m = PAGE_SIZE // CMP_STRIDE
    half = KV_W // 2

    def page_spec(k, kv):
        return pl.BlockSpec((None, None, None, half, PAGE_SIZE),
                            lambda b, i, pt: (pt[b * n_pages + i * pps + k], layer, kv, 0, 0))

    ospec = pl.BlockSpec((None, pps * m, 256), lambda b, i, pt: (b, i, 0))
    oshape = jax.ShapeDtypeStruct((db, n_pages * m, 256), F32)
    return pl.pallas_call(
        functools.partial(_cmp_proj_kernel, n_x=pps, n_prefetch=1),
        grid_spec=pltpu.PrefetchScalarGridSpec(
            num_scalar_prefetch=1,
            grid=(db, n_pages // pps),
            in_specs=[page_spec(k, kv) for kv in range(2) for k in range(pps)] + [
                _full_spec(perm, 3), _full_spec(wk, 3), _full_spec(wv, 3)],
            out_specs=[ospec, ospec]),
        out_shape=[oshape, oshape],
        compiler_params=_cparams(2),
        name="cmp_proj_paged",
    )(page_table.reshape(-1), *([cache_t] * (2 * pps)), perm, wk, wv)


def _cmp_mlp_kernel(abk_ref, abv_ref, pos_ref, w1_ref, b1_ref, w2_ref, b2_ref, ck_ref, cv_ref):
    m = abk_ref.shape[0]
    col = lax.broadcasted_iota(jnp.int32, (HEAD_DIM, m), 1)
    for kv, (ab_ref, o_ref) in enumerate(((abk_ref, ck_ref), (abv_ref, cv_ref))):
        pos = jnp.broadcast_to(pos_ref[kv], (SUBLANES, CMP_LEN * HEAD_DIM)).astype(BF16)
        c0 = _dot(pos, w1_ref[kv].astype(BF16))[0:1, :] + b1_ref[kv]
        c0 = jnp.concatenate([c0] * NSA_KV, axis=1)
        ab = ab_ref[...]
        nxt = pltpu.roll(ab[:, LANES:], m - 1, 0)
        h = jax.nn.gelu(ab[:, :LANES] + nxt + c0).astype(BF16)
        for g in range(NSA_KV):
            y_t = _dot_nt(w2_ref[kv, g], h) + b2_ref[kv]
            o_ref[g] = jnp.where(col < m - 1, y_t, 0.0).astype(BF16)


def _cmp_mlp(abk, abv, pos_flat, w1, b1, w2t, b2col):
    bx, m, _ = abk.shape
    abspec = pl.BlockSpec((None, m, 256), lambda b: (b, 0, 0))
    ospec = pl.BlockSpec((None, NSA_KV, HEAD_DIM, m), lambda b: (b, 0, 0, 0))
    oshape = jax.ShapeDtypeStruct((bx, NSA_KV, HEAD_DIM, m), BF16)
    return pl.pallas_call(
        _cmp_mlp_kernel,
        grid=(bx,),
        in_specs=[abspec, abspec, _full_spec(pos_flat, 1), _full_spec(w1, 1), _full_spec(b1, 1),
                  _full_spec(w2t, 1), _full_spec(b2col, 1)],
        out_specs=[ospec, ospec],
        out_shape=[oshape, oshape],
        compiler_params=_cparams(1),
        name="cmp_mlp",
    )(abk, abv, pos_flat, w1, b1, w2t, b2col)


def _bias_tiles_kernel(tab_ref, o_ref):
    h = pl.program_id(0)
    i = lax.broadcasted_iota(jnp.int32, (TQ, TQ), 0)
    j = lax.broadcasted_iota(jnp.int32, (TQ, TQ), 1)
    far = tab_ref[N_BUCKETS - 1, h]
    for kind in range(2):
        dist = i - j + kind * TQ
        bkt = _bucket(dist)
        bias = jnp.zeros((TQ, TQ), F32)
        for b in range(N_BUCKETS - 1):
            bias = jnp.where(bkt == b, tab_ref[b, h] - far, bias)
        o_ref[kind] = jnp.where(dist >= 0, bias, NEG_INF)


def _bias_tiles(rel_bias):
    nh = rel_bias.shape[1]
    return pl.pallas_call(
        _bias_tiles_kernel,
        grid=(nh,),
        in_specs=[pl.BlockSpec(memory_space=pltpu.SMEM)],
        out_specs=pl.BlockSpec((2, None, TQ, TQ), lambda h: (0, h, 0, 0)),
        out_shape=jax.ShapeDtypeStruct((2, nh, TQ, TQ), F32),
        compiler_params=_cparams(1),
        name="bias_tiles",
    )(rel_bias)


def _nsa_cmp_kernel(tab_ref, q_ref, ck_ref, cv_ref, ov_ref, oc_ref, sel_ref, *, n_slc, k_sel):
    i = pl.program_id(1)
    ncp = ck_ref.shape[2]
    jl = ov_ref.shape[1]
    qpos = i * TQ + lax.broadcasted_iota(jnp.int32, (TQ, ncp), 0)
    cend = lax.broadcasted_iota(jnp.int32, (TQ, ncp), 1) * CMP_STRIDE + (CMP_LEN - 1)
    dist = qpos - cend
    valid = dist >= 0
    bkt = _bucket(dist)
    j = lax.broadcasted_iota(jnp.int32, (TQ, jl), 1)
    cur = (i * TQ + lax.broadcasted_iota(jnp.int32, (TQ, jl), 0)) // SLC_BLOCK
    forced = (j == 0) | (j == cur) | (j == cur - 1)
    for g in range(NSA_KV):
        heads = range(g * NSA_REP, (g + 1) * NSA_REP)
        bias = [jnp.zeros((TQ, ncp), F32) for _ in heads]
        for b in range(N_BUCKETS - 1):
            hit = bkt == b
            bias = [jnp.where(hit, tab_ref[b, h] - tab_ref[N_BUCKETS - 1, h], bb) for h, bb in zip(heads, bias)]
        bias = jnp.concatenate([jnp.where(valid, bb, NEG_INF) for bb in bias], axis=0)
        q4 = q_ref[g * NSA_REP:(g + 1) * NSA_REP].reshape(NSA_REP * TQ, HEAD_DIM)
        s = _dot(q4, ck_ref[g]) + bias
        m = jnp.max(s, axis=1, keepdims=True)
        m = jnp.where(m == NEG_INF, 0.0, m)
        e = jnp.exp(s - m)
        p = (e / jnp.maximum(jnp.sum(e, axis=1, keepdims=True), 1e-30)).astype(BF16)
        o = _dot_nt(p, cv_ref[g])
        oc_ref[:, g * NSA_REP * HEAD_DIM:(g + 1) * NSA_REP * HEAD_DIM] = jnp.concatenate(
            [o[r * TQ:(r + 1) * TQ] for r in range(NSA_REP)], axis=1)
        imp4 = _dot(p, ov_ref[...])
        imp = imp4[0:TQ]
        for r in range(1, NSA_REP):
            imp = imp + imp4[r * TQ:(r + 1) * TQ]
        imp = jnp.where(forced, jnp.inf, imp)
        imp = jnp.where(j <= cur, imp, NEG_INF)
        sel_ref[g] = _topk_drop_mask(imp, n_slc, k_sel).astype(BF16)


def _nsa_cmp_prompt(rel_bias, q, ck_t, cv_t, overlap, n_slc, k_sel):
    b, _, t, _ = q.shape
    ncp = ck_t.shape[3]
    jl = overlap.shape[1]
    cspec = pl.BlockSpec((None, NSA_KV, HEAD_DIM, ncp), lambda bb, i: (bb, 0, 0, 0))
    return pl.pallas_call(
        functools.partial(_nsa_cmp_kernel, n_slc=n_slc, k_sel=k_sel),
        grid=(b, t // TQ),
        in_specs=[pl.BlockSpec(memory_space=pltpu.SMEM),
                  pl.BlockSpec((None, NSA_HEADS, TQ, HEAD_DIM), lambda bb, i: (bb, 0, i, 0)),
                  cspec, cspec, _full_spec(overlap, 2)],
        out_specs=[pl.BlockSpec((None, TQ, NSA_W), lambda bb, i: (bb, i, 0)),
                   pl.BlockSpec((None, NSA_KV, TQ, jl), lambda bb, i: (bb, 0, i, 0))],
        out_shape=[jax.ShapeDtypeStruct((b, t, NSA_W), F32),
                   jax.ShapeDtypeStruct((b, NSA_KV, t, jl), BF16)],
        compiler_params=_cparams(2, 40),
        name="nsa_cmp_prompt",
    )(rel_bias, q, ck_t, cv_t, overlap)


def _kv_tile(ref, kt):
    return ref[:, pl.ds(pl.multiple_of(kt * TQ, TQ), TQ)].astype(BF16)


def _nsa_slc_kernel(q_ref, k_ref, v_ref, sel_ref, e_ref, bt_ref, o_ref, *scr):
    i = pl.program_id(2)
    states = _flash_states(scr)
    drop = sel_ref[...]
    qs = [_augment_query(q_ref[r], drop) for r in range(NSA_REP)]
    for st in states:
        _softmax_init(*st)

    def tile(kt, kind):
        k_aug = jnp.concatenate([_kv_tile(k_ref, kt), e_ref[kt]], axis=0)
        adds = None if kind is None else [bt_ref[kind, r] for r in range(NSA_REP)]
        _flash_tile(qs, [k_aug] * NSA_REP, [_kv_tile(v_ref, kt)] * NSA_REP, states, adds=adds)

    def far_body(kt, c):
        tile(kt, None)
        return c

    lax.fori_loop(0, jnp.maximum(i - 1, 0), far_body, 0)

    @pl.when(i >= 1)
    def _():
        tile(i - 1, 1)

    tile(i, 0)
    o_ref[...] = jnp.concatenate([_softmax_finish(st[1], st[2]) for st in states], axis=1)


def _nsa_slc_prompt(q, kv_t, sel, emat, btiles):
    b, _, t, _ = q.shape
    jl = sel.shape[3]
    return pl.pallas_call(
        _nsa_slc_kernel,
        grid=(b, NSA_KV, t // TQ),
        in_specs=[pl.BlockSpec((None, NSA_REP, TQ, HEAD_DIM), lambda bb, g, i: (bb, g, i, 0)),
                  pl.BlockSpec((None, HEAD_DIM, t), lambda bb, g, i: (bb, g, 0)),
                  pl.BlockSpec((None, HEAD_DIM, t), lambda bb, g, i: (bb, NSA_KV + g, 0)),
                  pl.BlockSpec((None, None, TQ, jl), lambda bb, g, i: (bb, g, i, 0)),
                  _full_spec(emat, 3),
                  pl.BlockSpec((2, NSA_REP, TQ, TQ), lambda bb, g, i: (0, g, 0, 0))],
        out_specs=pl.BlockSpec((None, TQ, NSA_REP * HEAD_DIM), lambda bb, g, i: (bb, i, g)),
        out_shape=jax.ShapeDtypeStruct((b, t, NSA_W), F32),
        scratch_shapes=_flash_scratch(NSA_REP),
        compiler_params=_cparams(3, 40),
        name="nsa_slc_prompt",
    )(q, kv_t, kv_t, sel, emat, btiles)


def _nsa_win_kernel(q_ref, k_ref, v_ref, bt_ref, o_ref, *scr):
    i = pl.program_id(2)
    states = _flash_states(scr)
    qs = [q_ref[r] for r in range(NSA_REP)]
    for st in states:
        _softmax_init(*st)

    def tile(kt, adds):
        _flash_tile(qs, [_kv_tile(k_ref, kt)] * NSA_REP, [_kv_tile(v_ref, kt)] * NSA_REP, states, adds=adds)

    @pl.when(i >= WINDOW // TQ)
    def _():
        r = lax.broadcasted_iota(jnp.int32, (TQ, TQ), 0)
        c = lax.broadcasted_iota(jnp.int32, (TQ, TQ), 1)
        tile(i - WINDOW // TQ, [jnp.where(c > r, 0.0, NEG_INF)] * NSA_REP)

    @pl.when(i >= 1)
    def _():
        tile(i - 1, [bt_ref[1, r] for r in range(NSA_REP)])

    tile(i, [bt_ref[0, r] for r in range(NSA_REP)])
    o_ref[...] = jnp.concatenate([_softmax_finish(st[1], st[2]) for st in states], axis=1)


def _nsa_win_prompt(q, kv_t, btiles):
    b, _, t, _ = q.shape
    return pl.pallas_call(
        _nsa_win_kernel,
        grid=(b, NSA_KV, t // TQ),
        in_specs=[pl.BlockSpec((None, NSA_REP, TQ, HEAD_DIM), lambda bb, g, i: (bb, g, i, 0)),
                  pl.BlockSpec((None, HEAD_DIM, t), lambda bb, g, i: (bb, g, 0)),
                  pl.BlockSpec((None, HEAD_DIM, t), lambda bb, g, i: (bb, NSA_KV + g, 0)),
                  pl.BlockSpec((2, NSA_REP, TQ, TQ), lambda bb, g, i: (0, g, 0, 0))],
        out_specs=pl.BlockSpec((None, TQ, NSA_REP * HEAD_DIM), lambda bb, g, i: (bb, i, g)),
        out_shape=jax.ShapeDtypeStruct((b, t, NSA_W), F32),
        scratch_shapes=_flash_scratch(NSA_REP),
        compiler_params=_cparams(3, 40),
        name="nsa_win_prompt",
    )(q, kv_t, kv_t, btiles)


def _kmean_kernel(k_ref, o_ref, *, nblk):
    lane = lax.broadcasted_iota(jnp.int32, o_ref.shape, 1)
    acc = jnp.zeros(o_ref.shape, F32)
    for blk in range(nblk):
        mean = jnp.sum(k_ref[:, blk * MOBA_BLOCK:(blk + 1) * MOBA_BLOCK], axis=1, keepdims=True) / MOBA_BLOCK
        acc = jnp.where(lane == blk, mean, acc)
    o_ref[...] = acc


def _kmean_prompt(moba_t):
    bx, _, t = moba_t.shape
    return pl.pallas_call(
        functools.partial(_kmean_kernel, nblk=t // MOBA_BLOCK),
        grid=(bx,),
        in_specs=[pl.BlockSpec((None, MOBA_W, t), lambda b: (b, 0, 0))],
        out_specs=pl.BlockSpec((None, MOBA_W, LANES), lambda b: (b, 0, 0)),
        out_shape=jax.ShapeDtypeStruct((bx, MOBA_W, LANES), F32),
        compiler_params=_cparams(1, 40),
        name="moba_kmean_prompt",
    )(moba_t)


def _moba_kernel(q_ref, k_ref, v_ref, km_ref, e_ref, bt_ref, o_ref, *scr, nblk, k_m):
    i = pl.program_id(2)
    hb = q_ref.shape[0]
    states = _flash_states(scr)
    qs, q_augs = [], []
    for hh in range(hb):
        q = q_ref[hh]
        gs = _dot(q, km_ref[hh * HEAD_DIM:(hh + 1) * HEAD_DIM, :].astype(BF16))
        blk = lax.broadcasted_iota(jnp.int32, gs.shape, 1)
        gs = jnp.where(blk < i, gs, NEG_INF)
        qs.append(q)
        q_augs.append(_augment_query(q, _topk_drop_mask(gs, nblk, k_m)))
        _softmax_init(*states[hh])

    def tile(kt, bias_kind, masked):
        start = pl.multiple_of(kt * TQ, TQ)
        rows = [slice(hh * HEAD_DIM, (hh + 1) * HEAD_DIM) for hh in range(hb)]
        k_ts = [k_ref[r, pl.ds(start, TQ)].astype(BF16) for r in rows]
        v_ts = [v_ref[r, pl.ds(start, TQ)].astype(BF16) for r in rows]
        adds = None if bias_kind is None else [bt_ref[bias_kind, hh] for hh in range(hb)]
        if masked:
            e = e_ref[kt]
            k_ts = [jnp.concatenate([k_t, e], axis=0) for k_t in k_ts]
        _flash_tile(q_augs if masked else qs, k_ts, v_ts, states, adds=adds)

    def far_body(kt, c):
        tile(kt, None, True)
        return c

    lax.fori_loop(0, jnp.maximum(i - 1, 0), far_body, 0)

    @pl.when(i >= 1)
    def _():
        tile(i - 1, 1, True)

    tile(i, 0, False)
    o_ref[...] = jnp.concatenate([_softmax_finish(st[1], st[2]) for st in states], axis=1)


def _moba_prompt(q, moba_t, kmean_t, emat, btiles, k_m):
    b, _, t, _ = q.shape
    hb = MOBA_HB
    nq = t // TQ
    q0 = NSA_HEADS // hb
    kb = MOBA_W // (hb * HEAD_DIM)
    return pl.pallas_call(
        functools.partial(_moba_kernel, nblk=t // MOBA_BLOCK, k_m=k_m),
        grid=(b, MOBA_HEADS // hb, nq),
        in_specs=[pl.BlockSpec((None, hb, TQ, HEAD_DIM), lambda bb, hp, i: (bb, q0 + hp, i, 0)),
                  pl.BlockSpec((None, hb * HEAD_DIM, t), lambda bb, hp, i: (bb, hp, 0)),
                  pl.BlockSpec((None, hb * HEAD_DIM, t), lambda bb, hp, i: (bb, kb + hp, 0)),
                  pl.BlockSpec((None, hb * HEAD_DIM, LANES), lambda bb, hp, i: (bb, hp, 0)),
                  _full_spec(emat, 3),
                  pl.BlockSpec((2, hb, TQ, TQ), lambda bb, hp, i: (0, q0 + hp, 0, 0))],
        out_specs=pl.BlockSpec((None, TQ, hb * HEAD_DIM), lambda bb, hp, i: (bb, i, hp)),
        out_shape=jax.ShapeDtypeStruct((b, t, MOBA_W), F32),
        scratch_shapes=_flash_scratch(hb),
        compiler_params=_cparams(3, 48),
        name="moba_prompt",
    )(q, moba_t, moba_t, kmean_t, emat, btiles)


def _outproj_kernel(x_ref, gate_ref, az_ref, bz_ref, ma_ref, mb_ref, ag_ref, oc_ref, os_ref, ow_ref, ob_ref,
                    eg_ref, wua_ref, wub_ref, wo_ref, lng_ref, lnb_ref, y_ref, *, alpha):
    w = NSA_W
    g = _sigmoid(ag_ref[...])
    g_hi = g.astype(BF16)
    g_lo = (g - g_hi.astype(F32)).astype(BF16)
    ge = _dot(g_hi, eg_ref[...]) + _dot(g_lo, eg_ref[...])
    o_a = ge[:, 0:w] * oc_ref[...] + ge[:, w:2 * w] * os_ref[...] + ge[:, 2 * w:3 * w] * ow_ref[...]
    y_a = _dot((o_a * _silu(az_ref[...])).astype(BF16), wua_ref[...])
    y_b = _dot((ob_ref[...] * _silu(bz_ref[...])).astype(BF16), wub_ref[...])
    mixed = _dot((_sigmoid(ma_ref[...]) * y_a + _sigmoid(mb_ref[...]) * y_b).astype(BF16), wo_ref[...])
    z = alpha * x_ref[...] + gate_ref[...] * mixed
    mu = jnp.mean(z, axis=-1, keepdims=True)
    var = jnp.mean(jnp.square(z - mu), axis=-1, keepdims=True)
    y_ref[...] = (z - mu) * lax.rsqrt(var + LN_EPS) * lng_ref[...] + lnb_ref[...]


def _outproj(x, gate, z, o_c, o_s, o_w, o_b, eg, wua, wub, wo, ln_g, ln_b, alpha):
    bx, t, d = x.shape
    tm = min(t, 256)
    nt = t // tm

    def tok(width, col):
        return pl.BlockSpec((None, tm, width), lambda m: (m // nt, m % nt, col))

    return pl.pallas_call(
        functools.partial(_outproj_kernel, alpha=alpha),
        grid=(bx * nt,),
        in_specs=[tok(d, 0), _mod_spec(gate, tm, nt),
                  tok(NSA_W, Z_AZ // NSA_W), tok(MOBA_W, Z_BZ // MOBA_W), tok(d, Z_MA // d), tok(d, Z_MB // d),
                  tok(AG_PAD, Z_AG // AG_PAD), tok(NSA_W, 0), tok(NSA_W, 0), tok(NSA_W, 0), tok(MOBA_W, 0),
                  _full_spec(eg, 1), _full_spec(wua, 1), _full_spec(wub, 1), _full_spec(wo, 1),
                  _full_spec(ln_g, 1), _full_spec(ln_b, 1)],
        out_specs=tok(d, 0),
        out_shape=jax.ShapeDtypeStruct((bx, t, d), F32),
        compiler_params=_cparams(1, 48),
        name="out_proj",
    )(x, gate, z, z, z, z, z, o_c, o_s, o_w, o_b, eg, wua, wub, wo, ln_g, ln_b)


def _dec_attend(s, v_t, s_self, v_self):
    m = jnp.maximum(jnp.max(s, axis=1, keepdims=True), s_self)
    e = jnp.exp(s - m)
    e_self = jnp.exp(s_self - m)
    den = jnp.maximum(jnp.sum(e, axis=1, keepdims=True) + e_self, 1e-30)
    return (_dot_nt(e.astype(BF16), v_t) + e_self * v_self) / den


def _dec_cmp_kernel(tab_ref, q_ref, ck_ref, cv_ref, ov_ref, oc_ref, imp_ref, *, pos):
    ncp = ck_ref.shape[2]
    jl = ov_ref.shape[1]
    cend = lax.broadcasted_iota(jnp.int32, (SUBLANES, ncp), 1) * CMP_STRIDE + (CMP_LEN - 1)
    dist = pos - cend
    valid = dist >= 0
    bkt = _bucket(dist)
    j = lax.broadcasted_iota(jnp.int32, (SUBLANES, jl), 1)
    cur = pos // SLC_BLOCK
    for g in range(NSA_KV):
        s = jnp.where(valid, _dot(q_ref[g], ck_ref[g]) + _bias_rows(bkt, tab_ref[g]), NEG_INF)
        m = jnp.max(s, axis=1, keepdims=True)
        m = jnp.where(m == NEG_INF, 0.0, m)
        e = jnp.exp(s - m)
        p = (e / jnp.maximum(jnp.sum(e, axis=1, keepdims=True), 1e-30)).astype(BF16)
        oc_ref[g] = _dot_nt(p, cv_ref[g])
        imp4 = _dot(p, ov_ref[...])
        imp = imp4[0:1]
        for r in range(1, NSA_REP):
            imp = imp + imp4[r:r + 1]
        imp = jnp.broadcast_to(imp, (SUBLANES, jl))
        imp = jnp.where((j == 0) | (j == cur) | (j == cur - 1), jnp.inf, imp)
        imp_ref[g] = jnp.where(j <= cur, imp, NEG_INF)


def _dec_cmp(tab_g, q8, ck_t, cv_t, overlap, pos):
    db = q8.shape[0]
    ncp = ck_t.shape[3]
    jl = overlap.shape[1]
    cspec = pl.BlockSpec((None, NSA_KV, HEAD_DIM, ncp), lambda b: (b, 0, 0, 0))
    return pl.pallas_call(
        functools.partial(_dec_cmp_kernel, pos=pos),
        grid=(db,),
        in_specs=[_full_spec(tab_g, 1),
                  pl.BlockSpec((None, NSA_KV, SUBLANES, HEAD_DIM), lambda b: (b, 0, 0, 0)),
                  cspec, cspec, _full_spec(overlap, 1)],
        out_specs=[pl.BlockSpec((None, NSA_KV, SUBLANES, HEAD_DIM), lambda b: (b, 0, 0, 0)),
                   pl.BlockSpec((None, NSA_KV, SUBLANES, jl), lambda b: (b, 0, 0, 0))],
        out_shape=[jax.ShapeDtypeStruct((db, NSA_KV, SUBLANES, HEAD_DIM), F32),
                   jax.ShapeDtypeStruct((db, NSA_KV, SUBLANES, jl), F32)],
        compiler_params=_cparams(1),
        name="dec_nsa_cmp",
    )(tab_g, q8, ck_t, cv_t, overlap)


def _topk_idx_kernel(s_ref, idx_ref, *, n, k):
    st = s_ref[...].T
    nb = -(-n // SUBLANES)
    blocks = [st[rb * SUBLANES:(rb + 1) * SUBLANES, :] for rb in range(nb)]
    cnt = _rank_rows(blocks, n)
    sub = lax.broadcasted_iota(jnp.int32, blocks[0].shape, 0)
    rows = []
    for r in range(idx_ref.shape[0]):
        if r >= k:
            rows.append(jnp.full((1, st.shape[1]), -1, jnp.int32))
            continue
        acc = jnp.zeros(blocks[0].shape, jnp.int32)
        for rb in range(nb):
            hit = jnp.where(cnt[rb] == float(r), jnp.where(blocks[rb] > NEG_INF, 1, 0), 0)
            acc = acc + hit * (sub + (rb * SUBLANES + 1))
        rows.append(jnp.sum(acc, axis=0, keepdims=True) - 1)
    idx_ref[...] = jnp.concatenate(rows, axis=0)


def _topk_idx(scores, n, k):
    nrow, jl = scores.shape
    kp = -(-k // SUBLANES) * SUBLANES
    return pl.pallas_call(
        functools.partial(_topk_idx_kernel, n=n, k=k),
        grid=(1,),
        in_specs=[pl.BlockSpec((nrow, jl), lambda i: (0, 0))],
        out_specs=pl.BlockSpec((kp, nrow), lambda i: (0, 0)),
        out_shape=jax.ShapeDtypeStruct((kp, nrow), jnp.int32),
        compiler_params=_cparams(1),
        name="topk_idx",
    )(scores)


def _dec_slc_kernel(sel_ref, pt_ref, *refs, pos, n_slc, k_sel):
    k_refs, v_refs = refs[:k_sel], refs[k_sel:2 * k_sel]
    tab_ref, q_ref, kvn_ref, o_ref = refs[2 * k_sel:]
    b, g = pl.program_id(0), pl.program_id(1)
    n_keys = k_sel * PAGE_SIZE
    lane = lax.broadcasted_iota(jnp.int32, (SUBLANES, n_keys), 1)
    half = PAGE_SIZE // SLC_BLOCK
    kpos = jnp.zeros((SUBLANES, n_keys), jnp.int32)
    for k in range(k_sel):
        jk = sel_ref[(b * NSA_KV + g) * k_sel + k]
        ok = (jk >= 0) & (jk < n_slc - 1)
        in_blk = (lane % PAGE_SIZE) // SLC_BLOCK == jk % half
        here = jnp.where(in_blk, (jk // half) * PAGE_SIZE + lane % PAGE_SIZE, pos + 1)
        kpos = jnp.where(lane // PAGE_SIZE == k, jnp.where(ok, here, pos + 1), kpos)
    dist = pos - kpos
    tab = tab_ref[g]
    q = q_ref[...]
    k_t = jnp.concatenate([r[...] for r in k_refs], axis=1).astype(BF16)
    v_t = jnp.concatenate([r[...] for r in v_refs], axis=1).astype(BF16)
    s = jnp.where(dist >= 0, _dot(q, k_t) + _bias_rows(_bucket(dist), tab), NEG_INF)
    s_self = jnp.sum(q.astype(F32) * kvn_ref[pl.ds(g, 1), :], axis=1, keepdims=True) + tab[:, 0:1]
    o_ref[...] = _dec_attend(s, v_t, s_self, kvn_ref[pl.ds(NSA_KV + g, 1), :])


def _dec_slc(sel_flat, pt_flat, cache_t, layer, tab_g, q8, kv_new, pos, n_slc, k_sel, n_pages):
    db = q8.shape[0]
    half = PAGE_SIZE // SLC_BLOCK

    def blk_spec(k, kv):
        def imap(b, g, sel, pt):
            j = jnp.clip(sel[(b * NSA_KV + g) * k_sel + k], 0, n_slc - 2)
            return (pt[b * n_pages + j // half], layer, kv, g, 0, 0)
        return pl.BlockSpec((None, None, None, None, HEAD_DIM, PAGE_SIZE), imap)

    return pl.pallas_call(
        functools.partial(_dec_slc_kernel, pos=pos, n_slc=n_slc, k_sel=k_sel),
        grid_spec=pltpu.PrefetchScalarGridSpec(
            num_scalar_prefetch=2,
            grid=(db, NSA_KV),
            in_specs=[blk_spec(k, kv) for kv in range(2) for k in range(k_sel)] + [
                _full_spec(tab_g, 4),
                pl.BlockSpec((None, None, SUBLANES, HEAD_DIM), lambda b, g, sel, pt: (b, g, 0, 0)),
                pl.BlockSpec((None, 2 * NSA_KV, HEAD_DIM), lambda b, g, sel, pt: (b, 0, 0))],
            out_specs=pl.BlockSpec((None, None, SUBLANES, HEAD_DIM), lambda b, g, sel, pt: (b, g, 0, 0))),
        out_shape=jax.ShapeDtypeStruct((db, NSA_KV, SUBLANES, HEAD_DIM), F32),
        compiler_params=_cparams(2),
        name="dec_nsa_slc",
    )(sel_flat, pt_flat, *([cache_t] * (2 * k_sel)), tab_g, q8, kv_new)


def _dec_win_kernel(k0_ref, k1_ref, v0_ref, v1_ref, tab_ref, q_ref, kvn_ref, o_ref):
    n = k0_ref.shape[1]
    dist = n - lax.broadcasted_iota(jnp.int32, (SUBLANES, n), 1)
    bkt = _bucket(dist)
    for g, (k_ref, v_ref) in enumerate(((k0_ref, v0_ref), (k1_ref, v1_ref))):
        tab = tab_ref[g]
        q = q_ref[g]
        s = jnp.where(dist < WINDOW, _dot(q, k_ref[...].astype(BF16)) + _bias_rows(bkt, tab), NEG_INF)
        s_self = jnp.sum(q.astype(F32) * kvn_ref[g:g + 1, :], axis=1, keepdims=True) + tab[:, 0:1]
        o_ref[g] = _dec_attend(s, v_ref[...].astype(BF16), s_self, kvn_ref[NSA_KV + g:NSA_KV + g + 1, :])


def _dec_win(state_t, layer, tab_g, q8, kv_new):
    db = q8.shape[0]
    n = state_t.shape[5]

    def st_spec(kv, g):
        return pl.BlockSpec((None, None, None, None, HEAD_DIM, n), lambda b: (layer, b, kv, g, 0, 0))

    return pl.pallas_call(
        _dec_win_kernel,
        grid=(db,),
        in_specs=[st_spec(0, 0), st_spec(0, 1), st_spec(1, 0), st_spec(1, 1),
                  _full_spec(tab_g, 1),
                  pl.BlockSpec((None, NSA_KV, SUBLANES, HEAD_DIM), lambda b: (b, 0, 0, 0)),
                  pl.BlockSpec((None, 2 * NSA_KV, HEAD_DIM), lambda b: (b, 0, 0))],
        out_specs=pl.BlockSpec((None, NSA_KV, SUBLANES, HEAD_DIM), lambda b: (b, 0, 0, 0)),
        out_shape=jax.ShapeDtypeStruct((db, NSA_KV, SUBLANES, HEAD_DIM), F32),
        compiler_params=_cparams(1),
        name="dec_nsa_win",
    )(state_t, state_t, state_t, state_t, tab_g, q8, kv_new)


def _dec_moba_sweep_kernel(pt_ref, *refs, n_x):
    k_refs = refs[:n_x]
    qb_ref, s_ref = refs[n_x:]
    qb = qb_ref[...]
    for k, k_ref in enumerate(k_refs):
        prod = k_ref[...] * qb
        s_ref[:, k * PAGE_SIZE:(k + 1) * PAGE_SIZE] = jnp.concatenate(
            [jnp.sum(prod[h * HEAD_DIM:(h + 1) * HEAD_DIM], axis=0, keepdims=True) for h in range(MOBA_HEADS)],
            axis=0)


def _dec_moba_sweep(cache_t, page_table, layer, q_lanes):
    db, n_pages = page_table.shape
    pps = PAGES_PER_STEP

    def page_spec(k):
        return pl.BlockSpec((None, None, None, MOBA_W, PAGE_SIZE),
                            lambda b, i, pt: (pt[b * n_pages + i * pps + k], layer, 0, 0, 0))

    return pl.pallas_call(
        functools.partial(_dec_moba_sweep_kernel, n_x=pps),
        grid_spec=pltpu.PrefetchScalarGridSpec(
            num_scalar_prefetch=1,
            grid=(db, n_pages // pps),
            in_specs=[page_spec(k) for k in range(pps)] + [
                pl.BlockSpec((None, MOBA_W, PAGE_SIZE), lambda b, i, pt: (b, 0, 0))],
            out_specs=pl.BlockSpec((None, MOBA_HEADS, pps * PAGE_SIZE), lambda b, i, pt: (b, 0, i))),
        out_shape=jax.ShapeDtypeStruct((db, MOBA_HEADS, n_pages * PAGE_SIZE), F32),
        compiler_params=_cparams(2, 40),
        name="dec_moba_sweep",
    )(page_table.reshape(-1), *([cache_t] * pps), q_lanes)


def _dec_moba_gate_kernel(s_ref, gs_ref, *, nblk):
    lane = lax.broadcasted_iota(jnp.int32, gs_ref.shape, 1)
    gs = jnp.full(gs_ref.shape, NEG_INF, F32)
    for blk in range(nblk):
        mean = jnp.sum(s_ref[:, blk * MOBA_BLOCK:(blk + 1) * MOBA_BLOCK], axis=1, keepdims=True) / MOBA_BLOCK
        gs = jnp.where(lane == blk, mean, gs)
    gs_ref[...] = gs


def _dec_moba_gate(s_all, nblk):
    db, _, p = s_all.shape
    return pl.pallas_call(
        functools.partial(_dec_moba_gate_kernel, nblk=nblk),
        grid=(db,),
        in_specs=[pl.BlockSpec((None, MOBA_HEADS, p), lambda b: (b, 0, 0))],
        out_specs=pl.BlockSpec((None, MOBA_HEADS, LANES), lambda b: (b, 0, 0)),
        out_shape=jax.ShapeDtypeStruct((db, MOBA_HEADS, LANES), F32),
        compiler_params=_cparams(1),
        name="dec_moba_gate",
    )(s_all)


def _dec_moba_attend_kernel(sel_ref, pt_ref, *refs, pos, k_m):
    ppb = MOBA_BLOCK // PAGE_SIZE
    s_refs, v_refs = refs[:k_m], refs[k_m:k_m + k_m * ppb]
    tab_ref, q_ref, kn_ref, vn_ref, o_ref = refs[k_m + k_m * ppb:]
    b, h = pl.program_id(0), pl.program_id(1)
    n_keys = k_m * MOBA_BLOCK
    lane = lax.broadcasted_iota(jnp.int32, (SUBLANES, n_keys), 1)
    kpos = jnp.zeros((SUBLANES, n_keys), jnp.int32)
    for k in range(k_m):
        jk = sel_ref[(b * MOBA_HEADS + h) * k_m + k]
        kpos = jnp.where(lane // MOBA_BLOCK == k, jnp.where(jk >= 0, jk * MOBA_BLOCK + lane % MOBA_BLOCK, pos + 1),
                         kpos)
    dist = pos - kpos
    tab = tab_ref[...]
    s = jnp.concatenate([r[...] for r in s_refs], axis=1)
    s = jnp.where(dist >= 0, s + _bias_rows(_bucket(dist), tab), NEG_INF)
    v_t = jnp.concatenate([r[...] for r in v_refs], axis=1).astype(BF16)
    s_self = jnp.sum(q_ref[...] * kn_ref[...], axis=1, keepdims=True) + tab[:, 0:1]
    o_ref[...] = _dec_attend(s, v_t, s_self, vn_ref[...])


def _dec_moba_attend(sel_flat, pt_flat, s_all, cache_t, layer, tab_h, q, k_new, v_new, pos, k_m, n_pages, nblk):
    db = q.shape[0]
    ppb = MOBA_BLOCK // PAGE_SIZE

    def sel_of(b, h, sel, k):
        return jnp.clip(sel[(b * MOBA_HEADS + h) * k_m + k], 0, nblk - 1)

    def s_spec(k):
        return pl.BlockSpec((None, MOBA_HEADS, MOBA_BLOCK), lambda b, h, sel, pt: (b, 0, sel_of(b, h, sel, k)))

    def v_spec(k, pg):
        return pl.BlockSpec(
            (None, None, None, None, HEAD_DIM, PAGE_SIZE),
            lambda b, h, sel, pt: (pt[b * n_pages + sel_of(b, h, sel, k) * ppb + pg], layer, 1, h, 0, 0))

    row_spec = pl.BlockSpec((None, MOBA_HEADS, HEAD_DIM), lambda b, h, sel, pt: (b, 0, 0))
    return pl.pallas_call(
        functools.partial(_dec_moba_attend_kernel, pos=pos, k_m=k_m),
        grid_spec=pltpu.PrefetchScalarGridSpec(
            num_scalar_prefetch=2,
            grid=(db, MOBA_HEADS),
            in_specs=[s_spec(k) for k in range(k_m)] + [v_spec(k, pg) for k in range(k_m) for pg in range(ppb)] + [
                _full_spec(tab_h, 4), row_spec, row_spec, row_spec],
            out_specs=pl.BlockSpec((None, None, MOBA_HEADS, HEAD_DIM), lambda b, h, sel, pt: (b, h, 0, 0))),
        out_shape=jax.ShapeDtypeStruct((db, MOBA_HEADS, MOBA_HEADS, HEAD_DIM), F32),
        compiler_params=_cparams(2),
        name="dec_moba_attend",
    )(sel_flat, pt_flat, *([s_all] * k_m), *([cache_t] * (k_m * ppb)), tab_h, q, k_new, v_new)


def _overlap_matrix(n_cmp, n_slc, rows, cols):
    i = np.arange(n_cmp)[:, None]
    j = np.arange(n_slc)[None, :]
    units = SLC_BLOCK // CMP_STRIDE
    m = sum(((i + u) // units == j).astype(np.float32) for u in range(CMP_LEN // CMP_STRIDE))
    out = np.zeros((rows, cols), np.float32)
    out[:n_cmp, :n_slc] = m
    return jnp.asarray(out, dtype=BF16)


def _block_membership(t, block):
    rows = AUG_W - HEAD_DIM
    key = np.arange(t).reshape(t // TQ, 1, TQ)
    j = np.arange(rows).reshape(1, rows, 1)
    return jnp.asarray((key // block == j).astype(np.float32) * -MASK_BIG, dtype=BF16)


def _token_group_permutation():
    p = np.zeros((LANES, LANES), np.float32)
    for l in range(CMP_STRIDE):
        for m in range(LANES // CMP_STRIDE):
            p[l * (LANES // CMP_STRIDE) + m, CMP_STRIDE * m + l] = 1.0
    return jnp.asarray(p, dtype=BF16)


def _gate_expand_matrix():
    e = np.zeros((AG_PAD, 3 * NSA_W), np.float32)
    for h in range(NSA_HEADS):
        for br in range(3):
            e[h * 3 + br, br * NSA_W + h * HEAD_DIM: br * NSA_W + (h + 1) * HEAD_DIM] = 1.0
    return jnp.asarray(e, dtype=BF16)


def _split_w_in(w_in):
    scale = HEAD_DIM ** -0.5
    o = np.cumsum([0, 512, 128, 128, 128, 128, 128, 128, 24, 512, 512, 512, 512, 512, 1024, 1024])
    a_q, kv3, a_g, a_z, b_q, b_kv, b_z, m_ab = (
        w_in[..., o[0]:o[1]], w_in[..., o[1]:o[7]], w_in[..., o[7]:o[8]], w_in[..., o[8]:o[9]],
        w_in[..., o[9]:o[10]], w_in[..., o[10]:o[12]], w_in[..., o[12]:o[13]], w_in[..., o[13]:o[15]])
    pad = jnp.zeros(w_in.shape[:-1] + (AG_PAD - a_g.shape[-1],), w_in.dtype)
    wq = (jnp.concatenate([a_q, b_q], axis=-1) * scale).astype(BF16)
    wz = jnp.concatenate([a_z, b_z, m_ab, a_g, pad], axis=-1).astype(BF16)
    wkv = jnp.swapaxes(jnp.concatenate([kv3, b_kv], axis=-1), 1, 2).astype(BF16)
    return wq, wz, wkv


def _cmp_stage1_weights(w1):
    w1r = w1.reshape(2, CMP_STRIDE, HEAD_DIM, HEAD_DIM)
    eye = jnp.eye(NSA_KV, dtype=w1.dtype)
    w = jnp.einsum("hlde,gf->lgdhfe", w1r, eye)
    return w.reshape(CMP_STRIDE * NSA_KV * HEAD_DIM, 2 * NSA_KV * HEAD_DIM).astype(BF16)


def _cmp_stage2_weights(w2):
    w2t = jnp.swapaxes(w2, 1, 2)
    z = jnp.zeros_like(w2t)
    return jnp.stack([jnp.concatenate([w2t, z], axis=2), jnp.concatenate([z, w2t], axis=2)], axis=1).astype(BF16)


def _cache_view(c):
    return jnp.transpose(c, (0, 1, 3, 4, 5, 2))


def _kv_output(kv_t, heads):
    st = jnp.stack(kv_t, axis=1)
    b, depth, _, t = st.shape
    return jnp.transpose(st.reshape(b, depth, 2, heads, HEAD_DIM, t), (0, 1, 5, 2, 3, 4))


def _layer_prompt(x, mod, lw, consts):
    b, t, d = x.shape
    shift, scale, gate = mod
    q, z, cmp_t, slc_t, win_t, moba_t = _inproj(x, scale, shift, lw["wq"], lw["wz"], lw["wkv"])
    n_slc = t // SLC_BLOCK
    k_sel = min(SLC_TOPK, n_slc)
    abk, abv = _cmp_proj_prompt(cmp_t, consts["perm"], lw["cmp_wk"], lw["cmp_wv"])
    ck_t, cv_t = _cmp_mlp(abk, abv, lw["pos_flat"], lw["phi_w1"], lw["phi_b1"], lw["cmp_w2t"], lw["phi_b2"])
    o_c, sel = _nsa_cmp_prompt(consts["rel_bias"], q, ck_t, cv_t, consts["overlap_p"], n_slc, k_sel)
    o_s = _nsa_slc_prompt(q, slc_t, sel, consts["e_slc"], consts["btiles"])
    o_w = _nsa_win_prompt(q, win_t, consts["btiles"])
    nblk = t // MOBA_BLOCK
    kmean_t = _kmean_prompt(moba_t)
    o_b = _moba_prompt(q, moba_t, kmean_t, consts["e_moba"], consts["btiles"], min(MOBA_TOPK, nblk - 1))
    y = _outproj(x, gate, z, o_c, o_s, o_w, o_b, consts["eg"], lw["w_up_a"], lw["w_up_b"], lw["w_out"],
                 lw["ln_g"], lw["ln_b"], consts["alpha"])
    return y, dict(cmp=cmp_t, slc=slc_t, win=win_t[:, :, t - min(WINDOW, t):], moba=moba_t)


def _layer_sample(x, mod, lw, consts, layer, caches, page_table):
    _, db, d = x.shape
    shift, scale, gate = mod
    cache_cmp, cache_slc, cache_moba, state_win = caches
    n_phys, depth = cache_cmp.shape[:2]
    n_pages = page_table.shape[1]
    pos = n_pages * PAGE_SIZE
    pt_flat = page_table.reshape(-1)
    q, z, cmp_t, slc_t, win_t, moba_t = _inproj(x, scale, shift, lw["wq"], lw["wz"], lw["wkv"])
    cmp_n, slc_n, win_n = (a[0].T.reshape(db, 2 * NSA_KV, HEAD_DIM) for a in (cmp_t, slc_t, win_t))
    moba_n = moba_t[0].T.reshape(db, 2, MOBA_HEADS, HEAD_DIM)
    qa = jnp.transpose(q[0, :NSA_HEADS, :, :], (1, 0, 2)).reshape(db, NSA_KV, NSA_REP, HEAD_DIM)
    q8 = jnp.pad(qa, ((0, 0), (0, 0), (0, SUBLANES - NSA_REP), (0, 0)))
    qb = jnp.transpose(q[0, NSA_HEADS:, :, :], (1, 0, 2)).astype(F32)
    abk, abv = _cmp_proj_paged(cache_cmp.reshape(n_phys, depth, 2, KV_W // 2, PAGE_SIZE), page_table, layer,
                               consts["perm"], lw["cmp_wk"], lw["cmp_wv"])
    ck_t, cv_t = _cmp_mlp(abk, abv, lw["pos_flat"], lw["phi_w1"], lw["phi_b1"], lw["cmp_w2t"], lw["phi_b2"])
    n_slc = pos // SLC_BLOCK + 1
    k_sel = min(SLC_TOPK, n_slc)
    o_c8, imp = _dec_cmp(consts["tab_g"], q8, ck_t, cv_t, consts["overlap_s"], pos)
    imp2 = imp[:, :, 0].reshape(db * NSA_KV, -1)
    imp2 = jnp.pad(imp2, ((0, LANES - db * NSA_KV), (0, 0)), constant_values=NEG_INF)
    sel = _topk_idx(imp2, n_slc, k_sel)[:k_sel, :db * NSA_KV].T.reshape(-1)
    o_s8 = _dec_slc(sel, pt_flat, cache_slc, layer, consts["tab_g"], q8, slc_n, pos, n_slc, k_sel, n_pages)
    o_w8 = _dec_win(state_win, layer, consts["tab_g"], q8, win_n)
    o_c, o_s, o_w = (a[:, :, :NSA_REP].reshape(1, db, NSA_W) for a in (o_c8, o_s8, o_w8))
    nblk = pos // MOBA_BLOCK
    k_m = min(MOBA_TOPK, nblk)
    q_lanes = jnp.broadcast_to(qb.reshape(db, MOBA_W, 1), (db, MOBA_W, PAGE_SIZE))
    s_all = _dec_moba_sweep(cache_moba.reshape(n_phys, depth, 2, MOBA_W, PAGE_SIZE), page_table, layer, q_lanes)
    gs = _dec_moba_gate(s_all, nblk).reshape(db * MOBA_HEADS, LANES)
    sel_m = _topk_idx(gs, nblk, k_m)[:k_m].T.reshape(-1)
    o_b8 = _dec_moba_attend(sel_m, pt_flat, s_all, cache_moba, layer, consts["tab_h"], qb, moba_n[:, 0],
                            moba_n[:, 1], pos, k_m, n_pages, nblk)
    hd = jnp.arange(MOBA_HEADS)
    o_b = o_b8[:, hd, hd].reshape(1, db, MOBA_W)
    y = _outproj(x, gate, z, o_c, o_s, o_w, o_b, consts["eg"], lw["w_up_a"], lw["w_up_b"], lw["w_out"],
                 lw["ln_g"], lw["ln_b"], consts["alpha"])
    new = dict(cmp=cmp_n.reshape(db, 1, 2, NSA_KV, HEAD_DIM), slc=slc_n.reshape(db, 1, 2, NSA_KV, HEAD_DIM),
               win=win_n.reshape(db, 1, 2, NSA_KV, HEAD_DIM), moba=moba_n.reshape(db, 1, 2, MOBA_HEADS, HEAD_DIM))
    return y, new


def kernel(x_prompt, x_sample, cache_nsa_cmp, cache_nsa_slc, cache_moba, state_nsa_win, page_table, c_prompt, c_sample, rel_bias, w_ada, b_ada, w_in, phi_pos, phi_w1, phi_b1, phi_w2, phi_b2, w_up_a, w_up_b, w_out, ln_g, ln_b):
    b, t, d = x_prompt.shape
    db = x_sample.shape[0]
    depth = w_ada.shape[0]
    n_pages = page_table.shape[1]
    pos = n_pages * PAGE_SIZE
    assert x_sample.shape[1] == 1 and t % TQ == 0 and t >= WINDOW and n_pages % PAGES_PER_STEP == 0
    assert db * NSA_KV <= LANES and state_nsa_win.shape[2] == WINDOW and pos // MOBA_BLOCK >= 1
    assert t // MOBA_BLOCK <= LANES and t // SLC_BLOCK <= LANES

    mc = -(-(b + db) // SUBLANES) * SUBLANES
    c_all = jnp.pad(jnp.concatenate([c_prompt, c_sample], axis=0), ((0, mc - b - db), (0, 0)))
    mod = _ada(c_all, w_ada, b_ada)

    wq, wz, wkv = _split_w_in(w_in)
    tab_rel = (rel_bias - rel_bias[N_BUCKETS - 1][None, :]).T
    tab_g = jnp.pad(tab_rel[:NSA_HEADS].reshape(NSA_KV, NSA_REP, N_BUCKETS),
                    ((0, 0), (0, SUBLANES - NSA_REP), (0, 0)))
    n_slc_s = pos // SLC_BLOCK + 1
    consts = dict(
        rel_bias=rel_bias,
        alpha=float((2 * depth) ** 0.25),
        btiles=_bias_tiles(rel_bias),
        perm=_token_group_permutation(),
        overlap_p=_overlap_matrix(t // CMP_STRIDE - 1, t // SLC_BLOCK, t // CMP_STRIDE, LANES),
        overlap_s=_overlap_matrix(pos // CMP_STRIDE - 1, n_slc_s, pos // CMP_STRIDE, -(-n_slc_s // LANES) * LANES),
        e_slc=_block_membership(t, SLC_BLOCK),
        e_moba=_block_membership(t, MOBA_BLOCK),
        eg=_gate_expand_matrix(),
        tab_g=tab_g,
        tab_h=tab_rel[NSA_HEADS:],
    )
    caches = (_cache_view(cache_nsa_cmp), _cache_view(cache_nsa_slc), _cache_view(cache_moba),
              jnp.transpose(state_nsa_win, (0, 1, 3, 4, 5, 2)))

    yp, ys = x_prompt, x_sample.reshape(1, db, d)
    new_p, new_s = [], []
    for l in range(depth):
        lw = dict(
            wq=wq[l], wz=wz[l], wkv=wkv[l],
            cmp_wk=_cmp_stage1_weights(phi_w1[l, 0]), cmp_wv=_cmp_stage1_weights(phi_w1[l, 1]),
            pos_flat=phi_pos[l].reshape(2, 1, CMP_LEN * HEAD_DIM),
            phi_w1=phi_w1[l], phi_b1=phi_b1[l].reshape(2, 1, HEAD_DIM),
            cmp_w2t=_cmp_stage2_weights(phi_w2[l]), phi_b2=phi_b2[l].reshape(2, HEAD_DIM, 1),
            w_up_a=w_up_a[l].astype(BF16), w_up_b=w_up_b[l].astype(BF16), w_out=w_out[l].astype(BF16),
            ln_g=ln_g[l].reshape(1, d), ln_b=ln_b[l].reshape(1, d))
        shift, scale, gate = jnp.split(mod[l], 3, axis=-1)
        mod_p = tuple(a[:b, None, :] for a in (shift, scale, gate))
        mod_s = tuple(a[None, b:b + db, :] for a in (shift, scale, gate))
        yp, np_ = _layer_prompt(yp, mod_p, lw, consts)
        ys, ns_ = _layer_sample(ys, mod_s, lw, consts, l, caches, page_table)
        new_p.append(np_)
        new_s.append(ns_)

    def stack_s(key):
        return jnp.stack([it[key] for it in new_s], axis=1)

    win_p = jnp.moveaxis(_kv_output([it["win"] for it in new_p], NSA_KV), 1, 0)
    return (yp, ys.reshape(db, 1, d),
            _kv_output([it["cmp"] for it in new_p], NSA_KV), stack_s("cmp"),
            _kv_output([it["slc"] for it in new_p], NSA_KV), stack_s("slc"),
            _kv_output([it["moba"] for it in new_p], MOBA_HEADS), stack_s("moba"),
            win_p, jnp.moveaxis(stack_s("win"), 1, 0))
```

```python
import functools
import math

import numpy as np
import jax
import jax.numpy as jnp
from jax import lax
from jax.experimental import pallas as pl
from jax.experimental.pallas import tpu as pltpu

F32 = jnp.float32
BF16 = jnp.bfloat16
NEG_INF = float("-inf")
MASK_BIG = 2.0 ** 127

HEAD_DIM = 64
NSA_HEADS = 8
NSA_KV = 2
NSA_REP = NSA_HEADS // NSA_KV
CMP_LEN = 32
CMP_STRIDE = 16
SLC_BLOCK = 64
SLC_TOPK = 16
WINDOW = 512
MOBA_HEADS = 8
MOBA_BLOCK = 256
MOBA_TOPK = 3
N_HEADS = NSA_HEADS + MOBA_HEADS
N_BUCKETS = 32
MAX_EXACT = N_BUCKETS // 2
MAX_DISTANCE = 128
LN_EPS = 1e-5
PAGE_SIZE = 128

LANES = 128
SUBLANES = 8
TQ = 256
MOBA_HB = 8
CMP_NEAR = 40
H_QA, H_QB, H_KB, H_KS, H_KW = 0, 8, 16, 24, 26
N_QK = 28
PAGES_PER_STEP = 16

NSA_W = NSA_HEADS * HEAD_DIM
MOBA_W = MOBA_HEADS * HEAD_DIM
KV_W = 2 * NSA_KV * HEAD_DIM
MOBA_KV_W = 2 * MOBA_W
Z_AZ, Z_BZ, Z_MA, Z_MB, Z_AG = 0, 512, 1024, 2048, 3072
AG_PAD = 128
Z_W = Z_AG + AG_PAD


def _cparams(n_axes, vmem_mb=None):
    kw = dict(dimension_semantics=("arbitrary",) * n_axes)
    if vmem_mb is not None:
        kw["vmem_limit_bytes"] = vmem_mb * 1024 * 1024
    return pltpu.CompilerParams(**kw)


def _dot(a, b):
    return jnp.dot(a, b, preferred_element_type=F32)


def _dot_nt(a, b):
    return lax.dot_general(a, b, (((1,), (1,)), ((), ())), preferred_element_type=F32)


def _sigmoid(x):
    return 1.0 / (1.0 + jnp.exp(-x))


def _silu(x):
    return x * _sigmoid(x)


def _bucket(dist):
    n = jnp.maximum(dist, 0)
    nf = jnp.maximum(n, 1).astype(F32)
    large = MAX_EXACT + (jnp.log(nf / MAX_EXACT) / math.log(MAX_DISTANCE / MAX_EXACT)
                         * (N_BUCKETS - MAX_EXACT)).astype(jnp.int32)
    return jnp.where(n < MAX_EXACT, n, jnp.minimum(large, N_BUCKETS - 1))


def _bias_rows(bkt, tab):
    bias = jnp.zeros(bkt.shape, F32)
    for b in range(N_BUCKETS):
        bias = jnp.where(bkt == b, tab[:, b:b + 1], bias)
    return bias


def _rank_rows(blocks, n):
    nb = len(blocks)
    cnt = [jnp.zeros(blocks[0].shape, F32) for _ in range(nb)]
    sub = lax.broadcasted_iota(jnp.int32, blocks[0].shape, 0)
    for jp in range(n):
        rb0, r0 = divmod(jp, SUBLANES)
        row = blocks[rb0][r0:r0 + 1, :]
        for rb in range(nb):
            a = blocks[rb]
            if rb < rb0:
                ahead = jnp.where(row > a, 1.0, 0.0)
            elif rb > rb0:
                ahead = jnp.where(row >= a, 1.0, 0.0)
            else:
                ahead = jnp.where(sub > r0, jnp.where(row >= a, 1.0, 0.0), jnp.where(row > a, 1.0, 0.0))
            cnt[rb] = cnt[rb] + ahead
    return cnt


def _topk_drop_rows(score_t, n, k):
    jl, q = score_t.shape
    nb = -(-n // SUBLANES)
    blocks = [score_t[rb * SUBLANES:(rb + 1) * SUBLANES, :] for rb in range(nb)]
    cnt = _rank_rows(blocks, n)
    drop = [jnp.where(c < k, jnp.where(a > NEG_INF, 0.0, -MASK_BIG), -MASK_BIG) for c, a in zip(cnt, blocks)]
    if nb * SUBLANES < jl:
        drop.append(jnp.full((jl - nb * SUBLANES, q), -MASK_BIG, F32))
    return jnp.concatenate(drop, axis=0)


def _softmax_init(m_scr, l_scr, acc_scr):
    m_scr[...] = jnp.full(m_scr.shape, -MASK_BIG, F32)
    l_scr[...] = jnp.zeros(l_scr.shape, F32)
    acc_scr[...] = jnp.zeros(acc_scr.shape, F32)


def _flash_tile(qs, ks, v_ts, states, adds):
    n_s = len(qs)
    ss = [_dot_nt(ks[c], qs[c]) for c in range(n_s)]
    ps, alphas = [], []
    for c in range(n_s):
        s = ss[c]
        for a in adds[c]:
            s = s + a
        m_scr, l_scr, _ = states[c]
        m_prev = m_scr[...]
        m_next = jnp.maximum(m_prev, jnp.max(s, axis=0, keepdims=True))
        alpha = jnp.exp(m_prev - m_next)
        p = jnp.exp(s - m_next)
        l_scr[...] = alpha * l_scr[...] + jnp.sum(p, axis=0, keepdims=True)
        m_scr[...] = m_next
        ps.append(p.astype(BF16))
        alphas.append(alpha)
    for c in range(n_s):
        acc_scr = states[c][2]
        acc_scr[...] = acc_scr[...] * alphas[c] + _dot(v_ts[c], ps[c])


def _flash_scratch(n_streams):
    per = [pltpu.VMEM((1, TQ), F32), pltpu.VMEM((1, TQ), F32), pltpu.VMEM((HEAD_DIM, TQ), F32)]
    return per * n_streams


def _flash_states(scr):
    return [tuple(scr[3 * c:3 * c + 3]) for c in range(len(scr) // 3)]


def _softmax_finish(l_scr, acc_scr):
    return (acc_scr[...] / jnp.maximum(l_scr[...], 1e-30)).T


def _full_spec(a, n_grid, single=True):
    kw = dict(pipeline_mode=pl.Buffered(1)) if single else {}
    return pl.BlockSpec(a.shape, lambda *_: (0,) * a.ndim, **kw)


def _ada_kernel(c_ref, w_ref, b_ref, o_ref):
    a = _silu(c_ref[...]).astype(BF16)
    o_ref[...] = _dot(a, w_ref[...].astype(BF16)) + b_ref[...]


def _ada(c_all, w_ada, b_ada):
    depth, d, n3 = w_ada.shape
    mc = c_all.shape[0]
    tn = 1024
    return pl.pallas_call(
        _ada_kernel,
        grid=(depth, n3 // tn),
        in_specs=[pl.BlockSpec((mc, d), lambda l, j: (0, 0)),
                  pl.BlockSpec((None, d, tn), lambda l, j: (l, 0, j)),
                  pl.BlockSpec((None, 1, tn), lambda l, j: (l, 0, j))],
        out_specs=pl.BlockSpec((None, mc, tn), lambda l, j: (l, 0, j)),
        out_shape=jax.ShapeDtypeStruct((depth, mc, n3), F32),
        compiler_params=_cparams(2, 40),
        name="ada_mod",
    )(c_all, w_ada, b_ada.reshape(depth, 1, n3))


def _inproj_kernel(x_ref, sc_ref, sh_ref, wq_ref, wz_ref, wkv_ref, q_ref, z_ref, cmp_ref, slc_ref, win_ref,
                   moba_ref):
    h = (x_ref[...] * (1.0 + sc_ref[...]) + sh_ref[...]).astype(BF16)
    q = _dot(h, wq_ref[...])
    for hd in range(N_QK):
        q_ref[hd] = q[:, hd * HEAD_DIM:(hd + 1) * HEAD_DIM].astype(BF16)
    z_ref[...] = _dot(h, wz_ref[...])
    r = 0
    for o_ref in (cmp_ref, slc_ref, win_ref, moba_ref):
        n = o_ref.shape[0]
        o_ref[...] = _dot_nt(wkv_ref[r:r + n, :], h)
        r += n


def _mod_spec(mod, tm, nt):
    if mod.shape[1] == 1:
        return pl.BlockSpec((None, 1, mod.shape[2]), lambda m: (m // nt, 0, 0))
    return pl.BlockSpec((None, tm, mod.shape[2]), lambda m: (m // nt, m % nt, 0))


def _inproj(x, scale, shift, wq, wz, wkv):
    bx, t, d = x.shape
    tm = min(t, 512)
    nt = t // tm

    def kv_spec(rows):
        return pl.BlockSpec((None, rows, tm), lambda m: (m // nt, 0, m % nt))

    def kv_shape(rows):
        return jax.ShapeDtypeStruct((bx, rows, t), F32)

    return pl.pallas_call(
        _inproj_kernel,
        grid=(bx * nt,),
        in_specs=[pl.BlockSpec((None, tm, d), lambda m: (m // nt, m % nt, 0)),
                  _mod_spec(scale, tm, nt), _mod_spec(shift, tm, nt),
                  _full_spec(wq, 1), _full_spec(wz, 1), _full_spec(wkv, 1)],
        out_specs=[pl.BlockSpec((None, N_QK, tm, HEAD_DIM), lambda m: (m // nt, 0, m % nt, 0)),
                   pl.BlockSpec((None, tm, Z_W), lambda m: (m // nt, m % nt, 0)),
                   kv_spec(KV_W), kv_spec(KV_W), kv_spec(KV_W), kv_spec(MOBA_KV_W)],
        out_shape=[jax.ShapeDtypeStruct((bx, N_QK, t, HEAD_DIM), BF16),
                   jax.ShapeDtypeStruct((bx, t, Z_W), F32),
                   kv_shape(KV_W), kv_shape(KV_W), kv_shape(KV_W), kv_shape(MOBA_KV_W)],
        compiler_params=_cparams(1, 56),
        name="in_proj",
    )(x, scale, shift, wq, wz, wkv)


def _cmp_proj_kernel(*refs, n_x, n_prefetch=0):
    refs = refs[n_prefetch:]
    perm_ref, wk_ref, wv_ref, abk_ref, abv_ref = refs[2 * n_x:]
    perm = perm_ref[...]
    for x_refs, w_ref, ab_ref in ((refs[:n_x], wk_ref, abk_ref), (refs[n_x:2 * n_x], wv_ref, abv_ref)):
        rows = []
        for x_ref in x_refs:
            for c in range(x_ref.shape[1] // LANES):
                x_t = x_ref[:, c * LANES:(c + 1) * LANES].astype(BF16)
                xp = _dot_nt(perm, x_t)
                rows.append(jnp.concatenate(
                    [xp[l * SUBLANES:(l + 1) * SUBLANES] for l in range(CMP_STRIDE)], axis=1))
        xr = jnp.concatenate(rows, axis=0).astype(BF16)
        ab_ref[...] = _dot(xr, w_ref[...])


def _cmp_proj_prompt(cmp_t, perm, wk, wv):
    bx, _, t = cmp_t.shape
    tc = min(t, PAGES_PER_STEP * LANES)
    half = KV_W // 2
    m = tc // CMP_STRIDE
    ospec = pl.BlockSpec((None, m, 256), lambda b, i: (b, i, 0))
    oshape = jax.ShapeDtypeStruct((bx, t // CMP_STRIDE, 256), F32)
    return pl.pallas_call(
        functools.partial(_cmp_proj_kernel, n_x=1),
        grid=(bx, t // tc),
        in_specs=[pl.BlockSpec((None, half, tc), lambda b, i: (b, 0, i)),
                  pl.BlockSpec((None, half, tc), lambda b, i: (b, 1, i)),
                  _full_spec(perm, 2), _full_spec(wk, 2), _full_spec(wv, 2)],
        out_specs=[ospec, ospec],
        out_shape=[oshape, oshape],
        compiler_params=_cparams(2),
        name="cmp_proj_prompt",
    )(cmp_t, cmp_t, perm, wk, wv)


def _cmp_proj_paged(cache_t, page_table, layer, perm, wk, wv):
    db, n_pages = page_table.shape
    pps = PAGES_PER_STEP
    m = PAGE_SIZE // CMP_STRIDE
    half = KV_W // 2

    def page_spec(k, kv):
        return pl.BlockSpec((None, None, None, half, PAGE_SIZE),
                            lambda b, i, pt: (pt[b * n_pages + i * pps + k], layer, kv, 0, 0))

    ospec = pl.BlockSpec((None, pps * m, 256), lambda b, i, pt: (b, i, 0))
    oshape = jax.ShapeDtypeStruct((db, n_pages * m, 256), F32)
    return pl.pallas_call(
        functools.partial(_cmp_proj_kernel, n_x=pps, n_prefetch=1),
        grid_spec=pltpu.PrefetchScalarGridSpec(
            num_scalar_prefetch=1,
            grid=(db, n_pages // pps),
            in_specs=[page_spec(k, kv) for kv in range(2) for k in range(pps)] + [
                _full_spec(perm, 3), _full_spec(wk, 3), _full_spec(wv, 3)],
            out_specs=[ospec, ospec]),
        out_shape=[oshape, oshape],
        compiler_params=_cparams(2),
        name="cmp_proj_paged",
    )(page_table.reshape(-1), *([cache_t] * (2 * pps)), perm, wk, wv)


def _cmp_mlp_kernel(abk_ref, abv_ref, pos_ref, w1_ref, b1_ref, w2_ref, b2_ref, ck_ref, cv_ref, ckr_ref):
    m = abk_ref.shape[0]
    col = lax.broadcasted_iota(jnp.int32, (HEAD_DIM, m), 1)
    for kv, (ab_ref, o_ref) in enumerate(((abk_ref, ck_ref), (abv_ref, cv_ref))):
        pos = jnp.broadcast_to(pos_ref[kv], (SUBLANES, CMP_LEN * HEAD_DIM)).astype(BF16)
        c0 = _dot(pos, w1_ref[kv].astype(BF16))[0:1, :] + b1_ref[kv]
        c0 = jnp.concatenate([c0] * NSA_KV, axis=1)
        ab = ab_ref[...]
        nxt = pltpu.roll(ab[:, LANES:], m - 1, 0)
        h = jax.nn.gelu(ab[:, :LANES] + nxt + c0).astype(BF16)
        for g in range(NSA_KV):
            y_t = jnp.where(col < m - 1, _dot_nt(w2_ref[kv, g], h) + b2_ref[kv], 0.0)
            o_ref[g] = y_t.astype(BF16)
            if kv == 0:
                ckr_ref[g] = y_t.T.astype(BF16)


def _cmp_mlp(abk, abv, pos_flat, w1, b1, w2t, b2col):
    bx, m, _ = abk.shape
    abspec = pl.BlockSpec((None, m, 256), lambda b: (b, 0, 0))
    ospec = pl.BlockSpec((None, NSA_KV, HEAD_DIM, m), lambda b: (b, 0, 0, 0))
    oshape = jax.ShapeDtypeStruct((bx, NSA_KV, HEAD_DIM, m), BF16)
    return pl.pallas_call(
        _cmp_mlp_kernel,
        grid=(bx,),
        in_specs=[abspec, abspec, _full_spec(pos_flat, 1), _full_spec(w1, 1), _full_spec(b1, 1),
                  _full_spec(w2t, 1), _full_spec(b2col, 1)],
        out_specs=[ospec, ospec, pl.BlockSpec((None, NSA_KV, m, HEAD_DIM), lambda b: (b, 0, 0, 0))],
        out_shape=[oshape, oshape, jax.ShapeDtypeStruct((bx, NSA_KV, m, HEAD_DIM), BF16)],
        compiler_params=_cparams(1),
        name="cmp_mlp",
    )(abk, abv, pos_flat, w1, b1, w2t, b2col)


def _bias_tiles_kernel(tab_ref, o_ref):
    h = pl.program_id(0)
    key = lax.broadcasted_iota(jnp.int32, (TQ, TQ), 0)
    qry = lax.broadcasted_iota(jnp.int32, (TQ, TQ), 1)
    far = tab_ref[N_BUCKETS - 1, h]
    for kind in range(2):
        dist = qry - key + kind * TQ
        bkt = _bucket(dist)
        bias = jnp.zeros((TQ, TQ), F32)
        for b in range(N_BUCKETS - 1):
            bias = jnp.where(bkt == b, tab_ref[b, h] - far, bias)
        o_ref[kind] = jnp.where(dist >= 0, bias, NEG_INF)


def _bias_tiles(rel_bias):
    nh = rel_bias.shape[1]
    return pl.pallas_call(
        _bias_tiles_kernel,
        grid=(nh,),
        in_specs=[pl.BlockSpec(memory_space=pltpu.SMEM)],
        out_specs=pl.BlockSpec((2, None, TQ, TQ), lambda h: (0, h, 0, 0)),
        out_shape=jax.ShapeDtypeStruct((2, nh, TQ, TQ), F32),
        compiler_params=_cparams(1),
        name="bias_tiles",
    )(rel_bias)


def _nsa_cmp_kernel(tab_ref, q_ref, ck_ref, cv_ref, ov_ref, oc_ref, dn_ref, *s_scr, n_slc, k_sel):
    i = pl.program_id(1)
    ncp = ck_ref.shape[1]
    jl = ov_ref.shape[0]
    q0 = i * TQ
    qpos = q0 + lax.broadcasted_iota(jnp.int32, (ncp, TQ), 1)
    cend = lax.broadcasted_iota(jnp.int32, (ncp, TQ), 0) * CMP_STRIDE + (CMP_LEN - 1)
    valid = qpos >= cend
    n0 = jnp.maximum(q0 - (MAX_DISTANCE + CMP_LEN - 1), 0) // CMP_STRIDE
    n0 = pl.multiple_of(jnp.minimum(n0 // SUBLANES * SUBLANES, ncp - CMP_NEAR), SUBLANES)
    near_end = (n0 + lax.broadcasted_iota(jnp.int32, (CMP_NEAR, TQ), 0)) * CMP_STRIDE + (CMP_LEN - 1)
    bkt = _bucket(q0 + lax.broadcasted_iota(jnp.int32, (CMP_NEAR, TQ), 1) - near_end)
    j = lax.broadcasted_iota(jnp.int32, (jl, TQ), 0)
    cur = (q0 + lax.broadcasted_iota(jnp.int32, (jl, TQ), 1)) // SLC_BLOCK
    forced = (j == 0) | (j == cur) | (j == cur - 1)
    outs = []
    for g in range(NSA_KV):
        heads = range(g * NSA_REP, (g + 1) * NSA_REP)
        bias = [jnp.zeros((CMP_NEAR, TQ), F32) for _ in heads]
        for b in range(N_BUCKETS - 1):
            hit = bkt == b
            bias = [jnp.where(hit, tab_ref[b, h] - tab_ref[N_BUCKETS - 1, h], bb) for h, bb in zip(heads, bias)]
        ck = ck_ref[g]
        cv = cv_ref[g]
        imp = jnp.zeros((jl, TQ), F32)
        for r, h in enumerate(heads):
            scr = s_scr[r]
            scr[...] = _dot_nt(ck, q_ref[h])
            scr[pl.ds(n0, CMP_NEAR), :] = scr[pl.ds(n0, CMP_NEAR), :] + bias[r]
            s = jnp.where(valid, scr[...], NEG_INF)
            m = jnp.max(s, axis=0, keepdims=True)
            m = jnp.where(m == NEG_INF, 0.0, m)
            e = jnp.exp(s - m)
            p = (e / jnp.maximum(jnp.sum(e, axis=0, keepdims=True), 1e-30)).astype(BF16)
            outs.append(_dot(cv, p).T)
            imp = imp + _dot(ov_ref[...], p)
        imp = jnp.where(forced, jnp.inf, imp)
        imp = jnp.where(j <= cur, imp, NEG_INF)
        dn_ref[g] = _topk_drop_rows(imp, n_slc, k_sel)
    oc_ref[...] = jnp.concatenate(outs, axis=1)


def _nsa_cmp_prompt(rel_bias, qk, ck, cv_t, overlap_t, n_slc, k_sel):
    b, _, t, _ = qk.shape
    ncp = ck.shape[2]
    jl = overlap_t.shape[0]
    return pl.pallas_call(
        functools.partial(_nsa_cmp_kernel, n_slc=n_slc, k_sel=k_sel),
        grid=(b, t // TQ),
        in_specs=[pl.BlockSpec(memory_space=pltpu.SMEM),
                  pl.BlockSpec((None, NSA_HEADS, TQ, HEAD_DIM), lambda bb, i: (bb, H_QA // NSA_HEADS, i, 0)),
                  pl.BlockSpec((None, NSA_KV, ncp, HEAD_DIM), lambda bb, i: (bb, 0, 0, 0)),
                  pl.BlockSpec((None, NSA_KV, HEAD_DIM, ncp), lambda bb, i: (bb, 0, 0, 0)),
                  _full_spec(overlap_t, 2)],
        out_specs=[pl.BlockSpec((None, TQ, NSA_W), lambda bb, i: (bb, i, 0)),
                   pl.BlockSpec((None, NSA_KV, jl, TQ), lambda bb, i: (bb, 0, 0, i))],
        out_shape=[jax.ShapeDtypeStruct((b, t, NSA_W), F32),
                   jax.ShapeDtypeStruct((b, NSA_KV, jl, t), F32)],
        scratch_shapes=[pltpu.VMEM((ncp, TQ), F32)] * NSA_REP,
        compiler_params=_cparams(2, 40),
        name="nsa_cmp_prompt",
    )(rel_bias, qk, ck, cv_t, overlap_t)


def _k_tile(ref, kt):
    return ref[pl.ds(pl.multiple_of(kt * TQ, TQ), TQ), :]


def _v_tile(ref, kt):
    return ref[:, pl.ds(pl.multiple_of(kt * TQ, TQ), TQ)].astype(BF16)


def _nsa_slc_kernel(q_ref, k0_ref, k1_ref, v0_ref, v1_ref, dn_ref, bt_ref, o_ref, *scr):
    i = pl.program_id(1)
    states = _flash_states(scr)
    qs = [q_ref[h] for h in range(NSA_HEADS)]
    per_tile = TQ // SLC_BLOCK
    for st in states:
        _softmax_init(*st)

    def tile(kt, kind):
        ks, v_ts, adds = [], [], []
        for g, (k_ref, v_ref) in enumerate(((k0_ref, v0_ref), (k1_ref, v1_ref))):
            mask = jnp.concatenate(
                [jnp.broadcast_to(dn_ref[g, pl.ds(kt * per_tile + jb, 1), :], (SLC_BLOCK, TQ))
                 for jb in range(per_tile)], axis=0)
            ks += [_k_tile(k_ref, kt)] * NSA_REP
            v_ts += [_v_tile(v_ref, kt)] * NSA_REP
            adds += [[mask] if kind is None else [mask, bt_ref[kind, g * NSA_REP + r]] for r in range(NSA_REP)]
        _flash_tile(qs, ks, v_ts, states, adds)

    def far_body(kt, c):
        tile(kt, None)
        return c

    lax.fori_loop(0, jnp.maximum(i - 1, 0), far_body, 0)

    @pl.when(i >= 1)
    def _():
        tile(i - 1, 1)

    tile(i, 0)
    o_ref[...] = jnp.concatenate([_softmax_finish(st[1], st[2]) for st in states], axis=1)


def _nsa_slc_prompt(qk, kv_t, dn, btiles):
    b, _, t, _ = qk.shape
    jl = dn.shape[2]
    return pl.pallas_call(
        _nsa_slc_kernel,
        grid=(b, t // TQ),
        in_specs=_nsa_dense_specs(t, H_KS) + [
            pl.BlockSpec((None, NSA_KV, jl, TQ), lambda bb, i: (bb, 0, 0, i)),
            pl.BlockSpec((2, NSA_HEADS, TQ, TQ), lambda bb, i: (0, 0, 0, 0))],
        out_specs=pl.BlockSpec((None, TQ, NSA_W), lambda bb, i: (bb, i, 0)),
        out_shape=jax.ShapeDtypeStruct((b, t, NSA_W), F32),
        scratch_shapes=_flash_scratch(NSA_HEADS),
        compiler_params=_cparams(2, 40),
        name="nsa_slc_prompt",
    )(qk, qk, qk, kv_t, kv_t, dn, btiles)


def _nsa_dense_specs(t, k_head):
    return [pl.BlockSpec((None, NSA_HEADS, TQ, HEAD_DIM), lambda bb, i: (bb, H_QA // NSA_HEADS, i, 0)),
            pl.BlockSpec((None, None, t, HEAD_DIM), lambda bb, i: (bb, k_head, 0, 0)),
            pl.BlockSpec((None, None, t, HEAD_DIM), lambda bb, i: (bb, k_head + 1, 0, 0)),
            pl.BlockSpec((None, HEAD_DIM, t), lambda bb, i: (bb, NSA_KV, 0)),
            pl.BlockSpec((None, HEAD_DIM, t), lambda bb, i: (bb, NSA_KV + 1, 0))]


def _nsa_win_kernel(q_ref, k0_ref, k1_ref, v0_ref, v1_ref, bt_ref, o_ref, *scr):
    i = pl.program_id(1)
    states = _flash_states(scr)
    qs = [q_ref[h] for h in range(NSA_HEADS)]
    for st in states:
        _softmax_init(*st)

    def tile(kt, adds):
        ks = [_k_tile(k0_ref, kt)] * NSA_REP + [_k_tile(k1_ref, kt)] * NSA_REP
        v_ts = [_v_tile(v0_ref, kt)] * NSA_REP + [_v_tile(v1_ref, kt)] * NSA_REP
        _flash_tile(qs, ks, v_ts, states, adds)

    @pl.when(i >= WINDOW // TQ)
    def _():
        key = lax.broadcasted_iota(jnp.int32, (TQ, TQ), 0)
        qry = lax.broadcasted_iota(jnp.int32, (TQ, TQ), 1)
        tile(i - WINDOW // TQ, [[jnp.where(key > qry, 0.0, NEG_INF)]] * NSA_HEADS)

    @pl.when(i >= 1)
    def _():
        tile(i - 1, [[bt_ref[1, h]] for h in range(NSA_HEADS)])

    tile(i, [[bt_ref[0, h]] for h in range(NSA_HEADS)])
    o_ref[...] = jnp.concatenate([_softmax_finish(st[1], st[2]) for st in states], axis=1)


def _nsa_win_prompt(qk, kv_t, btiles):
    b, _, t, _ = qk.shape
    return pl.pallas_call(
        _nsa_win_kernel,
        grid=(b, t // TQ),
        in_specs=_nsa_dense_specs(t, H_KW) + [
            pl.BlockSpec((2, NSA_HEADS, TQ, TQ), lambda bb, i: (0, 0, 0, 0))],
        out_specs=pl.BlockSpec((None, TQ, NSA_W), lambda bb, i: (bb, i, 0)),
        out_shape=jax.ShapeDtypeStruct((b, t, NSA_W), F32),
        scratch_shapes=_flash_scratch(NSA_HEADS),
        compiler_params=_cparams(2, 40),
        name="nsa_win_prompt",
    )(qk, qk, qk, kv_t, kv_t, btiles)


def _kmean_kernel(k_ref, o_ref, *, nblk):
    lane = lax.broadcasted_iota(jnp.int32, (MOBA_W, LANES), 1)
    acc = jnp.zeros((MOBA_W, LANES), F32)
    for blk in range(nblk):
        mean = jnp.sum(k_ref[:, blk * MOBA_BLOCK:(blk + 1) * MOBA_BLOCK], axis=1, keepdims=True) / MOBA_BLOCK
        acc = jnp.where(lane == blk, mean, acc)
    for h in range(MOBA_HEADS):
        o_ref[h] = acc[h * HEAD_DIM:(h + 1) * HEAD_DIM, :].T.astype(BF16)


def _kmean_prompt(moba_t):
    bx, _, t = moba_t.shape
    return pl.pallas_call(
        functools.partial(_kmean_kernel, nblk=t // MOBA_BLOCK),
        grid=(bx,),
        in_specs=[pl.BlockSpec((None, MOBA_W, t), lambda b: (b, 0, 0))],
        out_specs=pl.BlockSpec((None, MOBA_HEADS, LANES, HEAD_DIM), lambda b: (b, 0, 0, 0)),
        out_shape=jax.ShapeDtypeStruct((bx, MOBA_HEADS, LANES, HEAD_DIM), BF16),
        compiler_params=_cparams(1, 40),
        name="moba_kmean_prompt",
    )(moba_t)


def _moba_kernel(q_ref, k_ref, v_ref, km_ref, bt_ref, o_ref, *scr, nblk, k_m):
    i = pl.program_id(2)
    hb = q_ref.shape[0]
    states = _flash_states(scr[:3 * hb])
    dn_scr = scr[3 * hb:]
    qs = [q_ref[hh] for hh in range(hb)]
    for hh in range(hb):
        gs = _dot_nt(km_ref[hh], qs[hh])
        blk = lax.broadcasted_iota(jnp.int32, gs.shape, 0)
        dn_scr[hh][...] = _topk_drop_rows(jnp.where(blk < i, gs, NEG_INF), nblk, k_m)
        _softmax_init(*states[hh])

    def tile(kt, bias_kind, masked):
        start = pl.multiple_of(kt * TQ, TQ)
        ks = [k_ref[hh, pl.ds(start, TQ), :] for hh in range(hb)]
        v_ts = [v_ref[hh * HEAD_DIM:(hh + 1) * HEAD_DIM, pl.ds(start, TQ)].astype(BF16) for hh in range(hb)]
        adds = [[] for _ in range(hb)]
        if masked:
            adds = [[dn_scr[hh][pl.ds(kt, 1), :]] for hh in range(hb)]
        if bias_kind is not None:
            adds = [a + [bt_ref[bias_kind, hh]] for hh, a in enumerate(adds)]
        _flash_tile(qs, ks, v_ts, states, adds)

    def far_body(kt, c):
        tile(kt, None, True)
        return c

    lax.fori_loop(0, jnp.maximum(i - 1, 0), far_body, 0)

    @pl.when(i >= 1)
    def _():
        tile(i - 1, 1, True)

    tile(i, 0, False)
    o_ref[...] = jnp.concatenate([_softmax_finish(st[1], st[2]) for st in states], axis=1)


def _moba_prompt(qk, moba_t, kmean, btiles, k_m):
    b, _, t, _ = qk.shape
    hb = MOBA_HB
    nq = t // TQ
    vb = MOBA_W // (hb * HEAD_DIM)
    once = dict(pipeline_mode=pl.Buffered(1))
    return pl.pallas_call(
        functools.partial(_moba_kernel, nblk=t // MOBA_BLOCK, k_m=k_m),
        grid=(b, MOBA_HEADS // hb, nq),
        in_specs=[pl.BlockSpec((None, hb, TQ, HEAD_DIM), lambda bb, hp, i: (bb, H_QB // hb + hp, i, 0)),
                  pl.BlockSpec((None, hb, t, HEAD_DIM), lambda bb, hp, i: (bb, H_KB // hb + hp, 0, 0), **once),
                  pl.BlockSpec((None, hb * HEAD_DIM, t), lambda bb, hp, i: (bb, vb + hp, 0), **once),
                  pl.BlockSpec((None, hb, LANES, HEAD_DIM), lambda bb, hp, i: (bb, hp, 0, 0)),
                  pl.BlockSpec((2, hb, TQ, TQ), lambda bb, hp, i: (0, NSA_HEADS // hb + hp, 0, 0), **once)],
        out_specs=pl.BlockSpec((None, TQ, hb * HEAD_DIM), lambda bb, hp, i: (bb, i, hp)),
        out_shape=jax.ShapeDtypeStruct((b, t, MOBA_W), F32),
        scratch_shapes=_flash_scratch(hb) + [pltpu.VMEM((LANES, TQ), F32)] * hb,
        compiler_params=_cparams(3, 48),
        name="moba_prompt",
    )(qk, qk, moba_t, kmean, btiles)


def _outproj_kernel(x_ref, gate_ref, az_ref, bz_ref, ma_ref, mb_ref, ag_ref, oc_ref, os_ref, ow_ref, ob_ref,
                    eg_ref, wua_ref, wub_ref, wo_ref, lng_ref, lnb_ref, y_ref, *, alpha):
    w = NSA_W
    g = _sigmoid(ag_ref[...])
    g_hi = g.astype(BF16)
    g_lo = (g - g_hi.astype(F32)).astype(BF16)
    ge = _dot(g_hi, eg_ref[...]) + _dot(g_lo, eg_ref[...])
    o_a = ge[:, 0:w] * oc_ref[...] + ge[:, w:2 * w] * os_ref[...] + ge[:, 2 * w:3 * w] * ow_ref[...]
    y_a = _dot((o_a * _silu(az_ref[...])).astype(BF16), wua_ref[...])
    y_b = _dot((ob_ref[...] * _silu(bz_ref[...])).astype(BF16), wub_ref[...])
    mixed = _dot((_sigmoid(ma_ref[...]) * y_a + _sigmoid(mb_ref[...]) * y_b).astype(BF16), wo_ref[...])
    z = alpha * x_ref[...] + gate_ref[...] * mixed
    mu = jnp.mean(z, axis=-1, keepdims=True)
    var = jnp.mean(jnp.square(z - mu), axis=-1, keepdims=True)
    y_ref[...] = (z - mu) * lax.rsqrt(var + LN_EPS) * lng_ref[...] + lnb_ref[...]


def _outproj(x, gate, z, o_c, o_s, o_w, o_b, eg, wua, wub, wo, ln_g, ln_b, alpha):
    bx, t, d = x.shape
    tm = min(t, 256)
    nt = t // tm

    def tok(width, col):
        return pl.BlockSpec((None, tm, width), lambda m: (m // nt, m % nt, col))

    return pl.pallas_call(
        functools.partial(_outproj_kernel, alpha=alpha),
        grid=(bx * nt,),
        in_specs=[tok(d, 0), _mod_spec(gate, tm, nt),
                  tok(NSA_W, Z_AZ // NSA_W), tok(MOBA_W, Z_BZ // MOBA_W), tok(d, Z_MA // d), tok(d, Z_MB // d),
                  tok(AG_PAD, Z_AG // AG_PAD), tok(NSA_W, 0), tok(NSA_W, 0), tok(NSA_W, 0), tok(MOBA_W, 0),
                  _full_spec(eg, 1), _full_spec(wua, 1), _full_spec(wub, 1), _full_spec(wo, 1),
                  _full_spec(ln_g, 1), _full_spec(ln_b, 1)],
        out_specs=tok(d, 0),
        out_shape=jax.ShapeDtypeStruct((bx, t, d), F32),
        compiler_params=_cparams(1, 48),
        name="out_proj",
    )(x, gate, z, z, z, z, z, o_c, o_s, o_w, o_b, eg, wua, wub, wo, ln_g, ln_b)


def _dec_attend(s, v_t, s_self, v_self):
    m = jnp.maximum(jnp.max(s, axis=1, keepdims=True), s_self)
    e = jnp.exp(s - m)
    e_self = jnp.exp(s_self - m)
    den = jnp.maximum(jnp.sum(e, axis=1, keepdims=True) + e_self, 1e-30)
    return (_dot_nt(e.astype(BF16), v_t) + e_self * v_self) / den


def _dec_cmp_kernel(tab_ref, q_ref, ck_ref, cv_ref, ov_ref, oc_ref, imp_ref, *, pos):
    ncp = ck_ref.shape[2]
    jl = ov_ref.shape[1]
    cend = lax.broadcasted_iota(jnp.int32, (SUBLANES, ncp), 1) * CMP_STRIDE + (CMP_LEN - 1)
    dist = pos - cend
    valid = dist >= 0
    bkt = _bucket(dist)
    j = lax.broadcasted_iota(jnp.int32, (SUBLANES, jl), 1)
    cur = pos // SLC_BLOCK
    for g in range(NSA_KV):
        s = jnp.where(valid, _dot(q_ref[g], ck_ref[g]) + _bias_rows(bkt, tab_ref[g]), NEG_INF)
        m = jnp.max(s, axis=1, keepdims=True)
        m = jnp.where(m == NEG_INF, 0.0, m)
        e = jnp.exp(s - m)
        p = (e / jnp.maximum(jnp.sum(e, axis=1, keepdims=True), 1e-30)).astype(BF16)
        oc_ref[g] = _dot_nt(p, cv_ref[g])
        imp4 = _dot(p, ov_ref[...])
        imp = imp4[0:1]
        for r in range(1, NSA_REP):
            imp = imp + imp4[r:r + 1]
        imp = jnp.broadcast_to(imp, (SUBLANES, jl))
        imp = jnp.where((j == 0) | (j == cur) | (j == cur - 1), jnp.inf, imp)
        imp_ref[g] = jnp.where(j <= cur, imp, NEG_INF)


def _dec_cmp(tab_g, q8, ck_t, cv_t, overlap, pos):
    db = q8.shape[0]
    ncp = ck_t.shape[3]
    jl = overlap.shape[1]
    cspec = pl.BlockSpec((None, NSA_KV, HEAD_DIM, ncp), lambda b: (b, 0, 0, 0))
    return pl.pallas_call(
        functools.partial(_dec_cmp_kernel, pos=pos),
        grid=(db,),
        in_specs=[_full_spec(tab_g, 1),
                  pl.BlockSpec((None, NSA_KV, SUBLANES, HEAD_DIM), lambda b: (b, 0, 0, 0)),
                  cspec, cspec, _full_spec(overlap, 1)],
        out_specs=[pl.BlockSpec((None, NSA_KV, SUBLANES, HEAD_DIM), lambda b: (b, 0, 0, 0)),
                   pl.BlockSpec((None, NSA_KV, SUBLANES, jl), lambda b: (b, 0, 0, 0))],
        out_shape=[jax.ShapeDtypeStruct((db, NSA_KV, SUBLANES, HEAD_DIM), F32),
                   jax.ShapeDtypeStruct((db, NSA_KV, SUBLANES, jl), F32)],
        compiler_params=_cparams(1),
        name="dec_nsa_cmp",
    )(tab_g, q8, ck_t, cv_t, overlap)


def _topk_idx_kernel(s_ref, idx_ref, *, n, k):
    st = s_ref[...].T
    nb = -(-n // SUBLANES)
    blocks = [st[rb * SUBLANES:(rb + 1) * SUBLANES, :] for rb in range(nb)]
    cnt = _rank_rows(blocks, n)
    sub = lax.broadcasted_iota(jnp.int32, blocks[0].shape, 0)
    rows = []
    for r in range(idx_ref.shape[0]):
        if r >= k:
            rows.append(jnp.full((1, st.shape[1]), -1, jnp.int32))
            continue
        acc = jnp.zeros(blocks[0].shape, jnp.int32)
        for rb in range(nb):
            hit = jnp.where(cnt[rb] == float(r), jnp.where(blocks[rb] > NEG_INF, 1, 0), 0)
            acc = acc + hit * (sub + (rb * SUBLANES + 1))
        rows.append(jnp.sum(acc, axis=0, keepdims=True) - 1)
    idx_ref[...] = jnp.concatenate(rows, axis=0)


def _topk_idx(scores, n, k):
    nrow, jl = scores.shape
    kp = -(-k // SUBLANES) * SUBLANES
    return pl.pallas_call(
        functools.partial(_topk_idx_kernel, n=n, k=k),
        grid=(1,),
        in_specs=[pl.BlockSpec((nrow, jl), lambda i: (0, 0))],
        out_specs=pl.BlockSpec((kp, nrow), lambda i: (0, 0)),
        out_shape=jax.ShapeDtypeStruct((kp, nrow), jnp.int32),
        compiler_params=_cparams(1),
        name="topk_idx",
    )(scores)


def _dec_slc_kernel(sel_ref, pt_ref, *refs, pos, n_slc, k_sel):
    k_refs, v_refs = refs[:k_sel], refs[k_sel:2 * k_sel]
    tab_ref, q_ref, kvn_ref, o_ref = refs[2 * k_sel:]
    b, g = pl.program_id(0), pl.program_id(1)
    n_keys = k_sel * PAGE_SIZE
    lane = lax.broadcasted_iota(jnp.int32, (SUBLANES, n_keys), 1)
    half = PAGE_SIZE // SLC_BLOCK
    kpos = jnp.zeros((SUBLANES, n_keys), jnp.int32)
    for k in range(k_sel):
        jk = sel_ref[(b * NSA_KV + g) * k_sel + k]
        ok = (jk >= 0) & (jk < n_slc - 1)
        in_blk = (lane % PAGE_SIZE) // SLC_BLOCK == jk % half
        here = jnp.where(in_blk, (jk // half) * PAGE_SIZE + lane % PAGE_SIZE, pos + 1)
        kpos = jnp.where(lane // PAGE_SIZE == k, jnp.where(ok, here, pos + 1), kpos)
    dist = pos - kpos
    tab = tab_ref[g]
    q = q_ref[...]
    k_t = jnp.concatenate([r[...] for r in k_refs], axis=1).astype(BF16)
    v_t = jnp.concatenate([r[...] for r in v_refs], axis=1).astype(BF16)
    s = jnp.where(dist >= 0, _dot(q, k_t) + _bias_rows(_bucket(dist), tab), NEG_INF)
    s_self = jnp.sum(q.astype(F32) * kvn_ref[pl.ds(g, 1), :], axis=1, keepdims=True) + tab[:, 0:1]
    o_ref[...] = _dec_attend(s, v_t, s_self, kvn_ref[pl.ds(NSA_KV + g, 1), :])


def _dec_slc(sel_flat, pt_flat, cache_t, layer, tab_g, q8, kv_new, pos, n_slc, k_sel, n_pages):
    db = q8.shape[0]
    half = PAGE_SIZE // SLC_BLOCK

    def blk_spec(k, kv):
        def imap(b, g, sel, pt):
            j = jnp.clip(sel[(b * NSA_KV + g) * k_sel + k], 0, n_slc - 2)
            return (pt[b * n_pages + j // half], layer, kv, g, 0, 0)
        return pl.BlockSpec((None, None, None, None, HEAD_DIM, PAGE_SIZE), imap)

    return pl.pallas_call(
        functools.partial(_dec_slc_kernel, pos=pos, n_slc=n_slc, k_sel=k_sel),
        grid_spec=pltpu.PrefetchScalarGridSpec(
            num_scalar_prefetch=2,
            grid=(db, NSA_KV),
            in_specs=[blk_spec(k, kv) for kv in range(2) for k in range(k_sel)] + [
                _full_spec(tab_g, 4),
                pl.BlockSpec((None, None, SUBLANES, HEAD_DIM), lambda b, g, sel, pt: (b, g, 0, 0)),
                pl.BlockSpec((None, 2 * NSA_KV, HEAD_DIM), lambda b, g, sel, pt: (b, 0, 0))],
            out_specs=pl.BlockSpec((None, None, SUBLANES, HEAD_DIM), lambda b, g, sel, pt: (b, g, 0, 0))),
        out_shape=jax.ShapeDtypeStruct((db, NSA_KV, SUBLANES, HEAD_DIM), F32),
        compiler_params=_cparams(2),
        name="dec_nsa_slc",
    )(sel_flat, pt_flat, *([cache_t] * (2 * k_sel)), tab_g, q8, kv_new)


def _dec_win_kernel(k0_ref, k1_ref, v0_ref, v1_ref, tab_ref, q_ref, kvn_ref, o_ref):
    n = k0_ref.shape[1]
    dist = n - lax.broadcasted_iota(jnp.int32, (SUBLANES, n), 1)
    bkt = _bucket(dist)
    for g, (k_ref, v_ref) in enumerate(((k0_ref, v0_ref), (k1_ref, v1_ref))):
        tab = tab_ref[g]
        q = q_ref[g]
        s = jnp.where(dist < WINDOW, _dot(q, k_ref[...].astype(BF16)) + _bias_rows(bkt, tab), NEG_INF)
        s_self = jnp.sum(q.astype(F32) * kvn_ref[g:g + 1, :], axis=1, keepdims=True) + tab[:, 0:1]
        o_ref[g] = _dec_attend(s, v_ref[...].astype(BF16), s_self, kvn_ref[NSA_KV + g:NSA_KV + g + 1, :])


def _dec_win(state_t, layer, tab_g, q8, kv_new):
    db = q8.shape[0]
    n = state_t.shape[5]

    def st_spec(kv, g):
        return pl.BlockSpec((None, None, None, None, HEAD_DIM, n), lambda b: (layer, b, kv, g, 0, 0))

    return pl.pallas_call(
        _dec_win_kernel,
        grid=(db,),
        in_specs=[st_spec(0, 0), st_spec(0, 1), st_spec(1, 0), st_spec(1, 1),
                  _full_spec(tab_g, 1),
                  pl.BlockSpec((None, NSA_KV, SUBLANES, HEAD_DIM), lambda b: (b, 0, 0, 0)),
                  pl.BlockSpec((None, 2 * NSA_KV, HEAD_DIM), lambda b: (b, 0, 0))],
        out_specs=pl.BlockSpec((None, NSA_KV, SUBLANES, HEAD_DIM), lambda b: (b, 0, 0, 0)),
        out_shape=jax.ShapeDtypeStruct((db, NSA_KV, SUBLANES, HEAD_DIM), F32),
        compiler_params=_cparams(1),
        name="dec_nsa_win",
    )(state_t, state_t, state_t, state_t, tab_g, q8, kv_new)


def _dec_moba_sweep_kernel(pt_ref, *refs, n_x):
    k_refs = refs[:n_x]
    qb_ref, s_ref = refs[n_x:]
    qb = qb_ref[...]
    for k, k_ref in enumerate(k_refs):
        prod = k_ref[...] * qb
        s_ref[:, k * PAGE_SIZE:(k + 1) * PAGE_SIZE] = jnp.concatenate(
            [jnp.sum(prod[h * HEAD_DIM:(h + 1) * HEAD_DIM], axis=0, keepdims=True) for h in range(MOBA_HEADS)],
            axis=0)


def _dec_moba_sweep(cache_t, page_table, layer, q_lanes):
    db, n_pages = page_table.shape
    pps = PAGES_PER_STEP

    def page_spec(k):
        return pl.BlockSpec((None, None, None, MOBA_W, PAGE_SIZE),
                            lambda b, i, pt: (pt[b * n_pages + i * pps + k], layer, 0, 0, 0))

    return pl.pallas_call(
        functools.partial(_dec_moba_sweep_kernel, n_x=pps),
        grid_spec=pltpu.PrefetchScalarGridSpec(
            num_scalar_prefetch=1,
            grid=(db, n_pages // pps),
            in_specs=[page_spec(k) for k in range(pps)] + [
                pl.BlockSpec((None, MOBA_W, PAGE_SIZE), lambda b, i, pt: (b, 0, 0))],
            out_specs=pl.BlockSpec((None, MOBA_HEADS, pps * PAGE_SIZE), lambda b, i, pt: (b, 0, i))),
        out_shape=jax.ShapeDtypeStruct((db, MOBA_HEADS, n_pages * PAGE_SIZE), F32),
        compiler_params=_cparams(2, 40),
        name="dec_moba_sweep",
    )(page_table.reshape(-1), *([cache_t] * pps), q_lanes)


def _dec_moba_gate_kernel(s_ref, gs_ref, *, nblk):
    lane = lax.broadcasted_iota(jnp.int32, gs_ref.shape, 1)
    gs = jnp.full(gs_ref.shape, NEG_INF, F32)
    for blk in range(nblk):
        mean = jnp.sum(s_ref[:, blk * MOBA_BLOCK:(blk + 1) * MOBA_BLOCK], axis=1, keepdims=True) / MOBA_BLOCK
        gs = jnp.where(lane == blk, mean, gs)
    gs_ref[...] = gs


def _dec_moba_gate(s_all, nblk):
    db, _, p = s_all.shape
    return pl.pallas_call(
        functools.partial(_dec_moba_gate_kernel, nblk=nblk),
        grid=(db,),
        in_specs=[pl.BlockSpec((None, MOBA_HEADS, p), lambda b: (b, 0, 0))],
        out_specs=pl.BlockSpec((None, MOBA_HEADS, LANES), lambda b: (b, 0, 0)),
        out_shape=jax.ShapeDtypeStruct((db, MOBA_HEADS, LANES), F32),
        compiler_params=_cparams(1),
        name="dec_moba_gate",
    )(s_all)


def _dec_moba_attend_kernel(sel_ref, pt_ref, *refs, pos, k_m):
    ppb = MOBA_BLOCK // PAGE_SIZE
    s_refs, v_refs = refs[:k_m], refs[k_m:k_m + k_m * ppb]
    tab_ref, q_ref, kn_ref, vn_ref, o_ref = refs[k_m + k_m * ppb:]
    b, h = pl.program_id(0), pl.program_id(1)
    n_keys = k_m * MOBA_BLOCK
    lane = lax.broadcasted_iota(jnp.int32, (SUBLANES, n_keys), 1)
    kpos = jnp.zeros((SUBLANES, n_keys), jnp.int32)
    for k in range(k_m):
        jk = sel_ref[(b * MOBA_HEADS + h) * k_m + k]
        kpos = jnp.where(lane // MOBA_BLOCK == k, jnp.where(jk >= 0, jk * MOBA_BLOCK + lane % MOBA_BLOCK, pos + 1),
                         kpos)
    dist = pos - kpos
    tab = tab_ref[...]
    s = jnp.concatenate([r[...] for r in s_refs], axis=1)
    s = jnp.where(dist >= 0, s + _bias_rows(_bucket(dist), tab), NEG_INF)
    v_t = jnp.concatenate([r[...] for r in v_refs], axis=1).astype(BF16)
    s_self = jnp.sum(q_ref[...] * kn_ref[...], axis=1, keepdims=True) + tab[:, 0:1]
    o_ref[...] = _dec_attend(s, v_t, s_self, vn_ref[...])


def _dec_moba_attend(sel_flat, pt_flat, s_all, cache_t, layer, tab_h, q, k_new, v_new, pos, k_m, n_pages, nblk):
    db = q.shape[0]
    ppb = MOBA_BLOCK // PAGE_SIZE

    def sel_of(b, h, sel, k):
        return jnp.clip(sel[(b * MOBA_HEADS + h) * k_m + k], 0, nblk - 1)

    def s_spec(k):
        return pl.BlockSpec((None, MOBA_HEADS, MOBA_BLOCK), lambda b, h, sel, pt: (b, 0, sel_of(b, h, sel, k)))

    def v_spec(k, pg):
        return pl.BlockSpec(
            (None, None, None, None, HEAD_DIM, PAGE_SIZE),
            lambda b, h, sel, pt: (pt[b * n_pages + sel_of(b, h, sel, k) * ppb + pg], layer, 1, h, 0, 0))

    row_spec = pl.BlockSpec((None, MOBA_HEADS, HEAD_DIM), lambda b, h, sel, pt: (b, 0, 0))
    return pl.pallas_call(
        functools.partial(_dec_moba_attend_kernel, pos=pos, k_m=k_m),
        grid_spec=pltpu.PrefetchScalarGridSpec(
            num_scalar_prefetch=2,
            grid=(db, MOBA_HEADS),
            in_specs=[s_spec(k) for k in range(k_m)] + [v_spec(k, pg) for k in range(k_m) for pg in range(ppb)] + [
                _full_spec(tab_h, 4), row_spec, row_spec, row_spec],
            out_specs=pl.BlockSpec((None, None, MOBA_HEADS, HEAD_DIM), lambda b, h, sel, pt: (b, h, 0, 0))),
        out_shape=jax.ShapeDtypeStruct((db, MOBA_HEADS, MOBA_HEADS, HEAD_DIM), F32),
        compiler_params=_cparams(2),
        name="dec_moba_attend",
    )(sel_flat, pt_flat, *([s_all] * k_m), *([cache_t] * (k_m * ppb)), tab_h, q, k_new, v_new)


def _overlap_matrix(n_cmp, n_slc, rows, cols):
    i = np.arange(n_cmp)[:, None]
    j = np.arange(n_slc)[None, :]
    units = SLC_BLOCK // CMP_STRIDE
    m = sum(((i + u) // units == j).astype(np.float32) for u in range(CMP_LEN // CMP_STRIDE))
    out = np.zeros((rows, cols), np.float32)
    out[:n_cmp, :n_slc] = m
    return jnp.asarray(out, dtype=BF16)


def _token_group_permutation():
    p = np.zeros((LANES, LANES), np.float32)
    for l in range(CMP_STRIDE):
        for m in range(LANES // CMP_STRIDE):
            p[l * (LANES // CMP_STRIDE) + m, CMP_STRIDE * m + l] = 1.0
    return jnp.asarray(p, dtype=BF16)


def _gate_expand_matrix():
    e = np.zeros((AG_PAD, 3 * NSA_W), np.float32)
    for h in range(NSA_HEADS):
        for br in range(3):
            e[h * 3 + br, br * NSA_W + h * HEAD_DIM: br * NSA_W + (h + 1) * HEAD_DIM] = 1.0
    return jnp.asarray(e, dtype=BF16)


def _split_w_in(w_in):
    scale = HEAD_DIM ** -0.5
    o = np.cumsum([0, 512, 128, 128, 128, 128, 128, 128, 24, 512, 512, 512, 512, 512, 1024, 1024])
    a_q, kv3, a_g, a_z, b_q, b_kv, b_z, m_ab = (
        w_in[..., o[0]:o[1]], w_in[..., o[1]:o[7]], w_in[..., o[7]:o[8]], w_in[..., o[8]:o[9]],
        w_in[..., o[9]:o[10]], w_in[..., o[10]:o[12]], w_in[..., o[12]:o[13]], w_in[..., o[13]:o[15]])
    pad = jnp.zeros(w_in.shape[:-1] + (AG_PAD - a_g.shape[-1],), w_in.dtype)
    k_rm = jnp.concatenate([w_in[..., o[10]:o[11]], w_in[..., o[3]:o[4]], w_in[..., o[5]:o[6]]], axis=-1)
    wq = jnp.concatenate([a_q * scale, b_q * scale, k_rm], axis=-1).astype(BF16)
    wz = jnp.concatenate([a_z, b_z, m_ab, a_g, pad], axis=-1).astype(BF16)
    wkv = jnp.swapaxes(jnp.concatenate([kv3, b_kv], axis=-1), 1, 2).astype(BF16)
    return wq, wz, wkv


def _cmp_stage1_weights(w1):
    w1r = w1.reshape(2, CMP_STRIDE, HEAD_DIM, HEAD_DIM)
    eye = jnp.eye(NSA_KV, dtype=w1.dtype)
    w = jnp.einsum("hlde,gf->lgdhfe", w1r, eye)
    return w.reshape(CMP_STRIDE * NSA_KV * HEAD_DIM, 2 * NSA_KV * HEAD_DIM).astype(BF16)


def _cmp_stage2_weights(w2):
    w2t = jnp.swapaxes(w2, 1, 2)
    z = jnp.zeros_like(w2t)
    return jnp.stack([jnp.concatenate([w2t, z], axis=2), jnp.concatenate([z, w2t], axis=2)], axis=1).astype(BF16)


def _cache_view(c):
    return jnp.transpose(c, (0, 1, 3, 4, 5, 2))


def _kv_output(kv_t, heads):
    st = jnp.stack(kv_t, axis=1)
    b, depth, _, t = st.shape
    return jnp.transpose(st.reshape(b, depth, 2, heads, HEAD_DIM, t), (0, 1, 5, 2, 3, 4))


def _layer_prompt(x, mod, lw, consts):
    b, t, d = x.shape
    shift, scale, gate = mod
    q, z, cmp_t, slc_t, win_t, moba_t = _inproj(x, scale, shift, lw["wq"], lw["wz"], lw["wkv"])
    n_slc = t // SLC_BLOCK
    k_sel = min(SLC_TOPK, n_slc)
    abk, abv = _cmp_proj_prompt(cmp_t, consts["perm"], lw["cmp_wk"], lw["cmp_wv"])
    _, cv_t, ck = _cmp_mlp(abk, abv, lw["pos_flat"], lw["phi_w1"], lw["phi_b1"], lw["cmp_w2t"], lw["phi_b2"])
    o_c, dn = _nsa_cmp_prompt(consts["rel_bias"], q, ck, cv_t, consts["overlap_p"], n_slc, k_sel)
    o_s = _nsa_slc_prompt(q, slc_t, dn, consts["btiles"])
    o_w = _nsa_win_prompt(q, win_t, consts["btiles"])
    nblk = t // MOBA_BLOCK
    o_b = _moba_prompt(q, moba_t, _kmean_prompt(moba_t), consts["btiles"], min(MOBA_TOPK, nblk - 1))
    y = _outproj(x, gate, z, o_c, o_s, o_w, o_b, consts["eg"], lw["w_up_a"], lw["w_up_b"], lw["w_out"],
                 lw["ln_g"], lw["ln_b"], consts["alpha"])
    return y, dict(cmp=cmp_t, slc=slc_t, win=win_t[:, :, t - min(WINDOW, t):], moba=moba_t)


def _layer_sample(x, mod, lw, consts, layer, caches, page_table):
    _, db, d = x.shape
    shift, scale, gate = mod
    cache_cmp, cache_slc, cache_moba, state_win = caches
    n_phys, depth = cache_cmp.shape[:2]
    n_pages = page_table.shape[1]
    pos = n_pages * PAGE_SIZE
    pt_flat = page_table.reshape(-1)
    q, z, cmp_t, slc_t, win_t, moba_t = _inproj(x, scale, shift, lw["wq"], lw["wz"], lw["wkv"])
    cmp_n, slc_n, win_n = (a[0].T.reshape(db, 2 * NSA_KV, HEAD_DIM) for a in (cmp_t, slc_t, win_t))
    moba_n = moba_t[0].T.reshape(db, 2, MOBA_HEADS, HEAD_DIM)
    qa = jnp.transpose(q[0, H_QA:H_QA + NSA_HEADS], (1, 0, 2)).reshape(db, NSA_KV, NSA_REP, HEAD_DIM)
    q8 = jnp.pad(qa, ((0, 0), (0, 0), (0, SUBLANES - NSA_REP), (0, 0)))
    qb = jnp.transpose(q[0, H_QB:H_QB + MOBA_HEADS], (1, 0, 2)).astype(F32)
    abk, abv = _cmp_proj_paged(cache_cmp.reshape(n_phys, depth, 2, KV_W // 2, PAGE_SIZE), page_table, layer,
                               consts["perm"], lw["cmp_wk"], lw["cmp_wv"])
    ck_t, cv_t, _ = _cmp_mlp(abk, abv, lw["pos_flat"], lw["phi_w1"], lw["phi_b1"], lw["cmp_w2t"], lw["phi_b2"])
    n_slc = pos // SLC_BLOCK + 1
    k_sel = min(SLC_TOPK, n_slc)
    o_c8, imp = _dec_cmp(consts["tab_g"], q8, ck_t, cv_t, consts["overlap_s"], pos)
    imp2 = imp[:, :, 0].reshape(db * NSA_KV, -1)
    imp2 = jnp.pad(imp2, ((0, LANES - db * NSA_KV), (0, 0)), constant_values=NEG_INF)
    sel = _topk_idx(imp2, n_slc, k_sel)[:k_sel, :db * NSA_KV].T.reshape(-1)
    o_s8 = _dec_slc(sel, pt_flat, cache_slc, layer, consts["tab_g"], q8, slc_n, pos, n_slc, k_sel, n_pages)
    o_w8 = _dec_win(state_win, layer, consts["tab_g"], q8, win_n)
    o_c, o_s, o_w = (a[:, :, :NSA_REP].reshape(1, db, NSA_W) for a in (o_c8, o_s8, o_w8))
    nblk = pos // MOBA_BLOCK
    k_m = min(MOBA_TOPK, nblk)
    q_lanes = jnp.broadcast_to(qb.reshape(db, MOBA_W, 1), (db, MOBA_W, PAGE_SIZE))
    s_all = _dec_moba_sweep(cache_moba.reshape(n_phys, depth, 2, MOBA_W, PAGE_SIZE), page_table, layer, q_lanes)
    gs = _dec_moba_gate(s_all, nblk).reshape(db * MOBA_HEADS, LANES)
    sel_m = _topk_idx(gs, nblk, k_m)[:k_m].T.reshape(-1)
    o_b8 = _dec_moba_attend(sel_m, pt_flat, s_all, cache_moba, layer, consts["tab_h"], qb, moba_n[:, 0],
                            moba_n[:, 1], pos, k_m, n_pages, nblk)
    hd = jnp.arange(MOBA_HEADS)
    o_b = o_b8[:, hd, hd].reshape(1, db, MOBA_W)
    y = _outproj(x, gate, z, o_c, o_s, o_w, o_b, consts["eg"], lw["w_up_a"], lw["w_up_b"], lw["w_out"],
                 lw["ln_g"], lw["ln_b"], consts["alpha"])
    new = dict(cmp=cmp_n.reshape(db, 1, 2, NSA_KV, HEAD_DIM), slc=slc_n.reshape(db, 1, 2, NSA_KV, HEAD_DIM),
               win=win_n.reshape(db, 1, 2, NSA_KV, HEAD_DIM), moba=moba_n.reshape(db, 1, 2, MOBA_HEADS, HEAD_DIM))
    return y, new


def kernel(x_prompt, x_sample, cache_nsa_cmp, cache_nsa_slc, cache_moba, state_nsa_win, page_table, c_prompt, c_sample, rel_bias, w_ada, b_ada, w_in, phi_pos, phi_w1, phi_b1, phi_w2, phi_b2, w_up_a, w_up_b, w_out, ln_g, ln_b):
    b, t, d = x_prompt.shape
    db = x_sample.shape[0]
    depth = w_ada.shape[0]
    n_pages = page_table.shape[1]
    pos = n_pages * PAGE_SIZE
    assert x_sample.shape[1] == 1 and t % TQ == 0 and t >= WINDOW and n_pages % PAGES_PER_STEP == 0
    assert db * NSA_KV <= LANES and state_nsa_win.shape[2] == WINDOW and pos // MOBA_BLOCK >= 1
    assert t // MOBA_BLOCK <= LANES and t // SLC_BLOCK <= LANES

    mc = -(-(b + db) // SUBLANES) * SUBLANES
    c_all = jnp.pad(jnp.concatenate([c_prompt, c_sample], axis=0), ((0, mc - b - db), (0, 0)))
    mod = _ada(c_all, w_ada, b_ada)

    wq, wz, wkv = _split_w_in(w_in)
    tab_rel = (rel_bias - rel_bias[N_BUCKETS - 1][None, :]).T
    tab_g = jnp.pad(tab_rel[:NSA_HEADS].reshape(NSA_KV, NSA_REP, N_BUCKETS),
                    ((0, 0), (0, SUBLANES - NSA_REP), (0, 0)))
    n_slc_s = pos // SLC_BLOCK + 1
    consts = dict(
        rel_bias=rel_bias,
        alpha=float((2 * depth) ** 0.25),
        btiles=_bias_tiles(rel_bias),
        perm=_token_group_permutation(),
        overlap_p=_overlap_matrix(t // CMP_STRIDE - 1, t // SLC_BLOCK, t // CMP_STRIDE, LANES).T,
        overlap_s=_overlap_matrix(pos // CMP_STRIDE - 1, n_slc_s, pos // CMP_STRIDE, -(-n_slc_s // LANES) * LANES),
        eg=_gate_expand_matrix(),
        tab_g=tab_g,
        tab_h=tab_rel[NSA_HEADS:],
    )
    caches = (_cache_view(cache_nsa_cmp), _cache_view(cache_nsa_slc), _cache_view(cache_moba),
              jnp.transpose(state_nsa_win, (0, 1, 3, 4, 5, 2)))

    yp, ys = x_prompt, x_sample.reshape(1, db, d)
    new_p, new_s = [], []
    for l in range(depth):
        lw = dict(
            wq=wq[l], wz=wz[l], wkv=wkv[l],
            cmp_wk=_cmp_stage1_weights(phi_w1[l, 0]), cmp_wv=_cmp_stage1_weights(phi_w1[l, 1]),
            pos_flat=phi_pos[l].reshape(2, 1, CMP_LEN * HEAD_DIM),
            phi_w1=phi_w1[l], phi_b1=phi_b1[l].reshape(2, 1, HEAD_DIM),
            cmp_w2t=_cmp_stage2_weights(phi_w2[l]), phi_b2=phi_b2[l].reshape(2, HEAD_DIM, 1),
            w_up_a=w_up_a[l].astype(BF16), w_up_b=w_up_b[l].astype(BF16), w_out=w_out[l].astype(BF16),
            ln_g=ln_g[l].reshape(1, d), ln_b=ln_b[l].reshape(1, d))
        shift, scale, gate = jnp.split(mod[l], 3, axis=-1)
        mod_p = tuple(a[:b, None, :] for a in (shift, scale, gate))
        mod_s = tuple(a[None, b:b + db, :] for a in (shift, scale, gate))
        yp, np_ = _layer_prompt(yp, mod_p, lw, consts)
        ys, ns_ = _layer_sample(ys, mod_s, lw, consts, l, caches, page_table)
        new_p.append(np_)
        new_s.append(ns_)

    def stack_s(key):
        return jnp.stack([it[key] for it in new_s], axis=1)

    win_p = jnp.moveaxis(_kv_output([it["win"] for it in new_p], NSA_KV), 1, 0)
    return (yp, ys.reshape(db, 1, d),
            _kv_output([it["cmp"] for it in new_p], NSA_KV), stack_s("cmp"),
            _kv_output([it["slc"] for it in new_p], NSA_KV), stack_s("slc"),
            _kv_output([it["moba"] for it in new_p], MOBA_HEADS), stack_s("moba"),
            win_p, jnp.moveaxis(stack_s("win"), 1, 0))
```

```python
import functools
import math

import numpy as np
import jax
import jax.numpy as jnp
from jax import lax
from jax.experimental import pallas as pl
from jax.experimental.pallas import tpu as pltpu

F32 = jnp.float32
BF16 = jnp.bfloat16
NEG_INF = float("-inf")
MASK_BIG = 2.0 ** 127

HEAD_DIM = 64
NSA_HEADS = 8
NSA_KV = 2
NSA_REP = NSA_HEADS // NSA_KV
CMP_LEN = 32
CMP_STRIDE = 16
SLC_BLOCK = 64
SLC_TOPK = 16
WINDOW = 512
MOBA_HEADS = 8
MOBA_BLOCK = 256
MOBA_TOPK = 3
N_HEADS = NSA_HEADS + MOBA_HEADS
N_BUCKETS = 32
MAX_EXACT = N_BUCKETS // 2
MAX_DISTANCE = 128
LN_EPS = 1e-5
PAGE_SIZE = 128

LANES = 128
SUBLANES = 8
TQ = 256
MOBA_HB = 8
CMP_NEAR = 40
H_QA, H_QB, H_KB, H_KS, H_KW = 0, 8, 16, 24, 26
N_QK = 28
PAGES_PER_STEP = 16

NSA_W = NSA_HEADS * HEAD_DIM
MOBA_W = MOBA_HEADS * HEAD_DIM
KV_W = 2 * NSA_KV * HEAD_DIM
MOBA_KV_W = 2 * MOBA_W
Z_AZ, Z_BZ, Z_MA, Z_MB, Z_AG = 0, 512, 1024, 2048, 3072
AG_PAD = 128
Z_W = Z_AG + AG_PAD


def _cparams(n_axes, vmem_mb=None):
    kw = dict(dimension_semantics=("arbitrary",) * n_axes)
    if vmem_mb is not None:
        kw["vmem_limit_bytes"] = vmem_mb * 1024 * 1024
    return pltpu.CompilerParams(**kw)


def _dot(a, b):
    return jnp.dot(a, b, preferred_element_type=F32)


def _dot_nt(a, b):
    return lax.dot_general(a, b, (((1,), (1,)), ((), ())), preferred_element_type=F32)


def _sigmoid(x):
    return 1.0 / (1.0 + jnp.exp(-x))


def _silu(x):
    return x * _sigmoid(x)


def _bucket(dist):
    n = jnp.maximum(dist, 0)
    nf = jnp.maximum(n, 1).astype(F32)
    large = MAX_EXACT + (jnp.log(nf / MAX_EXACT) / math.log(MAX_DISTANCE / MAX_EXACT)
                         * (N_BUCKETS - MAX_EXACT)).astype(jnp.int32)
    return jnp.where(n < MAX_EXACT, n, jnp.minimum(large, N_BUCKETS - 1))


def _bias_rows(bkt, tab):
    bias = jnp.zeros(bkt.shape, F32)
    for b in range(N_BUCKETS):
        bias = jnp.where(bkt == b, tab[:, b:b + 1], bias)
    return bias


def _rank_rows(blocks, n):
    nb = len(blocks)
    cnt = [jnp.zeros(blocks[0].shape, F32) for _ in range(nb)]
    sub = lax.broadcasted_iota(jnp.int32, blocks[0].shape, 0)
    for jp in range(n):
        rb0, r0 = divmod(jp, SUBLANES)
        row = blocks[rb0][r0:r0 + 1, :]
        for rb in range(nb):
            a = blocks[rb]
            if rb < rb0:
                ahead = jnp.where(row > a, 1.0, 0.0)
            elif rb > rb0:
                ahead = jnp.where(row >= a, 1.0, 0.0)
            else:
                ahead = jnp.where(sub > r0, jnp.where(row >= a, 1.0, 0.0), jnp.where(row > a, 1.0, 0.0))
            cnt[rb] = cnt[rb] + ahead
    return cnt


def _topk_drop_rows(score_t, n, k):
    jl, q = score_t.shape
    nb = -(-n // SUBLANES)
    blocks = [score_t[rb * SUBLANES:(rb + 1) * SUBLANES, :] for rb in range(nb)]
    cnt = _rank_rows(blocks, n)
    drop = [jnp.where(c < k, jnp.where(a > NEG_INF, 0.0, -MASK_BIG), -MASK_BIG) for c, a in zip(cnt, blocks)]
    if nb * SUBLANES < jl:
        drop.append(jnp.full((jl - nb * SUBLANES, q), -MASK_BIG, F32))
    return jnp.concatenate(drop, axis=0)


def _softmax_init(m_scr, l_scr, acc_scr):
    m_scr[...] = jnp.full(m_scr.shape, -MASK_BIG, F32)
    l_scr[...] = jnp.zeros(l_scr.shape, F32)
    acc_scr[...] = jnp.zeros(acc_scr.shape, F32)


def _flash_tile(qs, ks, v_ts, states, adds):
    n_s = len(qs)
    ss = [_dot_nt(ks[c], qs[c]) for c in range(n_s)]
    ps, alphas = [], []
    for c in range(n_s):
        s = ss[c]
        for a in adds[c]:
            s = s + a
        m_scr, l_scr, _ = states[c]
        m_prev = m_scr[...]
        m_next = jnp.maximum(m_prev, jnp.max(s, axis=0, keepdims=True))
        alpha = jnp.exp(m_prev - m_next)
        p = jnp.exp(s - m_next)
        l_scr[...] = alpha * l_scr[...] + jnp.sum(p, axis=0, keepdims=True)
        m_scr[...] = m_next
        ps.append(p.astype(BF16))
        alphas.append(alpha)
    for c in range(n_s):
        acc_scr = states[c][2]
        acc_scr[...] = acc_scr[...] * alphas[c] + _dot(v_ts[c], ps[c])


def _far_tiles(n, tile):
    def pair(j, c):
        tile(2 * j)
        tile(2 * j + 1)
        return c

    lax.fori_loop(0, n // 2, pair, 0)

    @pl.when(n % 2 == 1)
    def _():
        tile(n - 1)


def _flash_scratch(n_streams):
    per = [pltpu.VMEM((1, TQ), F32), pltpu.VMEM((1, TQ), F32), pltpu.VMEM((HEAD_DIM, TQ), F32)]
    return per * n_streams


def _flash_states(scr):
    return [tuple(scr[3 * c:3 * c + 3]) for c in range(len(scr) // 3)]


def _softmax_finish(l_scr, acc_scr):
    return (acc_scr[...] / jnp.maximum(l_scr[...], 1e-30)).T


def _full_spec(a, n_grid, single=True):
    kw = dict(pipeline_mode=pl.Buffered(1)) if single else {}
    return pl.BlockSpec(a.shape, lambda *_: (0,) * a.ndim, **kw)


def _ada_kernel(c_ref, w_ref, b_ref, o_ref):
    a = _silu(c_ref[...]).astype(BF16)
    o_ref[...] = _dot(a, w_ref[...].astype(BF16)) + b_ref[...]


def _ada(c_all, w_ada, b_ada):
    depth, d, n3 = w_ada.shape
    mc = c_all.shape[0]
    tn = 1024
    return pl.pallas_call(
        _ada_kernel,
        grid=(depth, n3 // tn),
        in_specs=[pl.BlockSpec((mc, d), lambda l, j: (0, 0)),
                  pl.BlockSpec((None, d, tn), lambda l, j: (l, 0, j)),
                  pl.BlockSpec((None, 1, tn), lambda l, j: (l, 0, j))],
        out_specs=pl.BlockSpec((None, mc, tn), lambda l, j: (l, 0, j)),
        out_shape=jax.ShapeDtypeStruct((depth, mc, n3), F32),
        compiler_params=_cparams(2, 40),
        name="ada_mod",
    )(c_all, w_ada, b_ada.reshape(depth, 1, n3))


def _inproj_kernel(x_ref, sc_ref, sh_ref, wq_ref, wz_ref, wkv_ref, q_ref, z_ref, cmp_ref, slc_ref, win_ref,
                   moba_ref):
    h = (x_ref[...] * (1.0 + sc_ref[...]) + sh_ref[...]).astype(BF16)
    q = _dot(h, wq_ref[...])
    for hd in range(N_QK):
        q_ref[hd] = q[:, hd * HEAD_DIM:(hd + 1) * HEAD_DIM].astype(BF16)
    z_ref[...] = _dot(h, wz_ref[...])
    r = 0
    for o_ref in (cmp_ref, slc_ref, win_ref, moba_ref):
        n = o_ref.shape[0]
        o_ref[...] = _dot_nt(wkv_ref[r:r + n, :], h)
        r += n


def _mod_spec(mod, tm, nt):
    if mod.shape[1] == 1:
        return pl.BlockSpec((None, 1, mod.shape[2]), lambda m: (m // nt, 0, 0))
    return pl.BlockSpec((None, tm, mod.shape[2]), lambda m: (m // nt, m % nt, 0))


def _inproj(x, scale, shift, wq, wz, wkv):
    bx, t, d = x.shape
    tm = min(t, 512)
    nt = t // tm

    def kv_spec(rows):
        return pl.BlockSpec((None, rows, tm), lambda m: (m // nt, 0, m % nt))

    def kv_shape(rows):
        return jax.ShapeDtypeStruct((bx, rows, t), F32)

    return pl.pallas_call(
        _inproj_kernel,
        grid=(bx * nt,),
        in_specs=[pl.BlockSpec((None, tm, d), lambda m: (m // nt, m % nt, 0)),
                  _mod_spec(scale, tm, nt), _mod_spec(shift, tm, nt),
                  _full_spec(wq, 1), _full_spec(wz, 1), _full_spec(wkv, 1)],
        out_specs=[pl.BlockSpec((None, N_QK, tm, HEAD_DIM), lambda m: (m // nt, 0, m % nt, 0)),
                   pl.BlockSpec((None, tm, Z_W), lambda m: (m // nt, m % nt, 0)),
                   kv_spec(KV_W), kv_spec(KV_W), kv_spec(KV_W), kv_spec(MOBA_KV_W)],
        out_shape=[jax.ShapeDtypeStruct((bx, N_QK, t, HEAD_DIM), BF16),
                   jax.ShapeDtypeStruct((bx, t, Z_W), F32),
                   kv_shape(KV_W), kv_shape(KV_W), kv_shape(KV_W), kv_shape(MOBA_KV_W)],
        compiler_params=_cparams(1, 56),
        name="in_proj",
    )(x, scale, shift, wq, wz, wkv)


def _cmp_proj_kernel(*refs, n_x, n_prefetch=0):
    refs = refs[n_prefetch:]
    perm_ref, wk_ref, wv_ref, abk_ref, abv_ref = refs[n_x:]
    perm = perm_ref[...]
    chunks = [x_ref[:, c * LANES:(c + 1) * LANES].astype(BF16)
              for x_ref in refs[:n_x] for c in range(x_ref.shape[1] // LANES)]
    groups = LANES // CMP_STRIDE
    rows_k, rows_v = [], []
    for c0 in range(0, len(chunks), 2):
        pair = chunks[c0:c0 + 2]
        xp = _dot_nt(perm[:len(pair) * LANES, :len(pair) * LANES], jnp.concatenate(pair, axis=1))
        for ci in range(len(pair)):
            parts = [xp[ci * LANES + l * groups:ci * LANES + (l + 1) * groups] for l in range(CMP_STRIDE)]
            rows_k.append(jnp.concatenate([p[:, :KV_W // 2] for p in parts], axis=1))
            rows_v.append(jnp.concatenate([p[:, KV_W // 2:] for p in parts], axis=1))
    for rows, w_ref, ab_ref in ((rows_k, wk_ref, abk_ref), (rows_v, wv_ref, abv_ref)):
        xr = jnp.concatenate(rows, axis=0).astype(BF16)
        ab_ref[...] = _dot(xr, w_ref[...])


def _cmp_proj_prompt(cmp_t, perm, wk, wv):
    bx, _, t = cmp_t.shape
    tc = min(t, PAGES_PER_STEP * LANES)
    m = tc // CMP_STRIDE
    ospec = pl.BlockSpec((None, m, 256), lambda b, i: (b, i, 0))
    oshape = jax.ShapeDtypeStruct((bx, t // CMP_STRIDE, 256), F32)
    return pl.pallas_call(
        functools.partial(_cmp_proj_kernel, n_x=1),
        grid=(bx, t // tc),
        in_specs=[pl.BlockSpec((None, KV_W, tc), lambda b, i: (b, 0, i)),
                  _full_spec(perm, 2), _full_spec(wk, 2), _full_spec(wv, 2)],
        out_specs=[ospec, ospec],
        out_shape=[oshape, oshape],
        compiler_params=_cparams(2),
        name="cmp_proj_prompt",
    )(cmp_t, perm, wk, wv)


def _cmp_proj_paged(cache_t, page_table, layer, perm, wk, wv):
    db, n_pages = page_table.shape
    pps = PAGES_PER_STEP
    m = PAGE_SIZE // CMP_STRIDE

    def page_spec(k):
        return pl.BlockSpec((None, None, KV_W, PAGE_SIZE),
                            lambda b, i, pt: (pt[b * n_pages + i * pps + k], layer, 0, 0))

    ospec = pl.BlockSpec((None, pps * m, 256), lambda b, i, pt: (b, i, 0))
    oshape = jax.ShapeDtypeStruct((db, n_pages * m, 256), F32)
    return pl.pallas_call(
        functools.partial(_cmp_proj_kernel, n_x=pps, n_prefetch=1),
        grid_spec=pltpu.PrefetchScalarGridSpec(
            num_scalar_prefetch=1,
            grid=(db, n_pages // pps),
            in_specs=[page_spec(k) for k in range(pps)] + [
                _full_spec(perm, 3), _full_spec(wk, 3), _full_spec(wv, 3)],
            out_specs=[ospec, ospec]),
        out_shape=[oshape, oshape],
        compiler_params=_cparams(2),
        name="cmp_proj_paged",
    )(page_table.reshape(-1), *([cache_t] * pps), perm, wk, wv)


def _cmp_mlp_kernel(abk_ref, abv_ref, pos_ref, w1_ref, b1_ref, w2_ref, b2_ref, ck_ref, cv_ref, ckr_ref):
    m = abk_ref.shape[0]
    col = lax.broadcasted_iota(jnp.int32, (HEAD_DIM, m), 1)
    for kv, (ab_ref, o_ref) in enumerate(((abk_ref, ck_ref), (abv_ref, cv_ref))):
        pos = jnp.broadcast_to(pos_ref[kv], (SUBLANES, CMP_LEN * HEAD_DIM)).astype(BF16)
        c0 = _dot(pos, w1_ref[kv].astype(BF16))[0:1, :] + b1_ref[kv]
        c0 = jnp.concatenate([c0] * NSA_KV, axis=1)
        ab = ab_ref[...]
        nxt = pltpu.roll(ab[:, LANES:], m - 1, 0)
        h = jax.nn.gelu(ab[:, :LANES] + nxt + c0).astype(BF16)
        for g in range(NSA_KV):
            y_t = jnp.where(col < m - 1, _dot_nt(w2_ref[kv, g], h) + b2_ref[kv], 0.0)
            o_ref[g] = y_t.astype(BF16)
            if kv == 0:
                ckr_ref[g] = y_t.T.astype(BF16)


def _cmp_mlp(abk, abv, pos_flat, w1, b1, w2t, b2col):
    bx, m, _ = abk.shape
    abspec = pl.BlockSpec((None, m, 256), lambda b: (b, 0, 0))
    ospec = pl.BlockSpec((None, NSA_KV, HEAD_DIM, m), lambda b: (b, 0, 0, 0))
    oshape = jax.ShapeDtypeStruct((bx, NSA_KV, HEAD_DIM, m), BF16)
    return pl.pallas_call(
        _cmp_mlp_kernel,
        grid=(bx,),
        in_specs=[abspec, abspec, _full_spec(pos_flat, 1), _full_spec(w1, 1), _full_spec(b1, 1),
                  _full_spec(w2t, 1), _full_spec(b2col, 1)],
        out_specs=[ospec, ospec, pl.BlockSpec((None, NSA_KV, m, HEAD_DIM), lambda b: (b, 0, 0, 0))],
        out_shape=[oshape, oshape, jax.ShapeDtypeStruct((bx, NSA_KV, m, HEAD_DIM), BF16)],
        compiler_params=_cparams(1),
        name="cmp_mlp",
    )(abk, abv, pos_flat, w1, b1, w2t, b2col)


def _bias_tiles_kernel(tab_ref, o_ref):
    h = pl.program_id(0)
    key = lax.broadcasted_iota(jnp.int32, (TQ, TQ), 0)
    qry = lax.broadcasted_iota(jnp.int32, (TQ, TQ), 1)
    far = tab_ref[N_BUCKETS - 1, h]
    for kind in range(2):
        dist = qry - key + kind * TQ
        bkt = _bucket(dist)
        bias = jnp.zeros((TQ, TQ), F32)
        for b in range(N_BUCKETS - 1):
            bias = jnp.where(bkt == b, tab_ref[b, h] - far, bias)
        o_ref[kind] = jnp.where(dist >= 0, bias, NEG_INF)


def _bias_tiles(rel_bias):
    nh = rel_bias.shape[1]
    return pl.pallas_call(
        _bias_tiles_kernel,
        grid=(nh,),
        in_specs=[pl.BlockSpec(memory_space=pltpu.SMEM)],
        out_specs=pl.BlockSpec((2, None, TQ, TQ), lambda h: (0, h, 0, 0)),
        out_shape=jax.ShapeDtypeStruct((2, nh, TQ, TQ), F32),
        compiler_params=_cparams(1),
        name="bias_tiles",
    )(rel_bias)


def _nsa_cmp_kernel(tab_ref, q_ref, ck_ref, cv_ref, ov_ref, oc_ref, dn_ref, *s_scr, n_slc, k_sel):
    i = pl.program_id(1)
    ncp = ck_ref.shape[1]
    jl = ov_ref.shape[0]
    q0 = i * TQ
    qpos = q0 + lax.broadcasted_iota(jnp.int32, (ncp, TQ), 1)
    cend = lax.broadcasted_iota(jnp.int32, (ncp, TQ), 0) * CMP_STRIDE + (CMP_LEN - 1)
    valid = qpos >= cend
    n0 = jnp.maximum(q0 - (MAX_DISTANCE + CMP_LEN - 1), 0) // CMP_STRIDE
    n0 = pl.multiple_of(jnp.minimum(n0 // SUBLANES * SUBLANES, ncp - CMP_NEAR), SUBLANES)
    near_end = (n0 + lax.broadcasted_iota(jnp.int32, (CMP_NEAR, TQ), 0)) * CMP_STRIDE + (CMP_LEN - 1)
    bkt = _bucket(q0 + lax.broadcasted_iota(jnp.int32, (CMP_NEAR, TQ), 1) - near_end)
    j = lax.broadcasted_iota(jnp.int32, (jl, TQ), 0)
    cur = (q0 + lax.broadcasted_iota(jnp.int32, (jl, TQ), 1)) // SLC_BLOCK
    forced = (j == 0) | (j == cur) | (j == cur - 1)
    outs = []
    for g in range(NSA_KV):
        heads = range(g * NSA_REP, (g + 1) * NSA_REP)
        bias = [jnp.zeros((CMP_NEAR, TQ), F32) for _ in heads]
        for b in range(N_BUCKETS - 1):
            hit = bkt == b
            bias = [jnp.where(hit, tab_ref[b, h] - tab_ref[N_BUCKETS - 1, h], bb) for h, bb in zip(heads, bias)]
        ck = ck_ref[g]
        cv = cv_ref[g]
        imp = jnp.zeros((jl, TQ), F32)
        for r, h in enumerate(heads):
            scr = s_scr[r]
            scr[...] = _dot_nt(ck, q_ref[h])
            scr[pl.ds(n0, CMP_NEAR), :] = scr[pl.ds(n0, CMP_NEAR), :] + bias[r]
            s = jnp.where(valid, scr[...], NEG_INF)
            m = jnp.max(s, axis=0, keepdims=True)
            m = jnp.where(m == NEG_INF, 0.0, m)
            e = jnp.exp(s - m)
            p = (e / jnp.maximum(jnp.sum(e, axis=0, keepdims=True), 1e-30)).astype(BF16)
            outs.append(_dot(cv, p).T)
            imp = imp + _dot(ov_ref[...], p)
        imp = jnp.where(forced, jnp.inf, imp)
        imp = jnp.where(j <= cur, imp, NEG_INF)
        dn_ref[g] = _topk_drop_rows(imp, n_slc, k_sel)
    oc_ref[...] = jnp.concatenate(outs, axis=1)


def _nsa_cmp_prompt(rel_bias, qk, ck, cv_t, overlap_t, n_slc, k_sel):
    b, _, t, _ = qk.shape
    ncp = ck.shape[2]
    jl = overlap_t.shape[0]
    return pl.pallas_call(
        functools.partial(_nsa_cmp_kernel, n_slc=n_slc, k_sel=k_sel),
        grid=(b, t // TQ),
        in_specs=[pl.BlockSpec(memory_space=pltpu.SMEM),
                  pl.BlockSpec((None, NSA_HEADS, TQ, HEAD_DIM), lambda bb, i: (bb, H_QA // NSA_HEADS, i, 0)),
                  pl.BlockSpec((None, NSA_KV, ncp, HEAD_DIM), lambda bb, i: (bb, 0, 0, 0)),
                  pl.BlockSpec((None, NSA_KV, HEAD_DIM, ncp), lambda bb, i: (bb, 0, 0, 0)),
                  _full_spec(overlap_t, 2)],
        out_specs=[pl.BlockSpec((None, TQ, NSA_W), lambda bb, i: (bb, i, 0)),
                   pl.BlockSpec((None, NSA_KV, jl, TQ), lambda bb, i: (bb, 0, 0, i))],
        out_shape=[jax.ShapeDtypeStruct((b, t, NSA_W), F32),
                   jax.ShapeDtypeStruct((b, NSA_KV, jl, t), F32)],
        scratch_shapes=[pltpu.VMEM((ncp, TQ), F32)] * NSA_REP,
        compiler_params=_cparams(2, 40),
        name="nsa_cmp_prompt",
    )(rel_bias, qk, ck, cv_t, overlap_t)


def _k_tile(ref, kt):
    return ref[pl.ds(pl.multiple_of(kt * TQ, TQ), TQ), :]


def _v_tile(ref, kt):
    return ref[:, pl.ds(pl.multiple_of(kt * TQ, TQ), TQ)].astype(BF16)


def _nsa_slc_kernel(q_ref, k0_ref, k1_ref, v0_ref, v1_ref, dn_ref, bt_ref, o_ref, *scr):
    i = pl.program_id(1)
    states = _flash_states(scr)
    qs = [q_ref[h] for h in range(NSA_HEADS)]
    per_tile = TQ // SLC_BLOCK
    for st in states:
        _softmax_init(*st)

    def tile(kt, kind):
        ks, v_ts, adds = [], [], []
        for g, (k_ref, v_ref) in enumerate(((k0_ref, v0_ref), (k1_ref, v1_ref))):
            mask = jnp.concatenate(
                [jnp.broadcast_to(dn_ref[g, pl.ds(kt * per_tile + jb, 1), :], (SLC_BLOCK, TQ))
                 for jb in range(per_tile)], axis=0)
            ks += [_k_tile(k_ref, kt)] * NSA_REP
            v_ts += [_v_tile(v_ref, kt)] * NSA_REP
            adds += [[mask] if kind is None else [mask, bt_ref[kind, g * NSA_REP + r]] for r in range(NSA_REP)]
        _flash_tile(qs, ks, v_ts, states, adds)

    _far_tiles(jnp.maximum(i - 1, 0), lambda kt: tile(kt, None))

    @pl.when(i >= 1)
    def _():
        tile(i - 1, 1)

    tile(i, 0)
    o_ref[...] = jnp.concatenate([_softmax_finish(st[1], st[2]) for st in states], axis=1)


def _nsa_slc_prompt(qk, kv_t, dn, btiles):
    b, _, t, _ = qk.shape
    jl = dn.shape[2]
    return pl.pallas_call(
        _nsa_slc_kernel,
        grid=(b, t // TQ),
        in_specs=_nsa_dense_specs(t, H_KS) + [
            pl.BlockSpec((None, NSA_KV, jl, TQ), lambda bb, i: (bb, 0, 0, i)),
            pl.BlockSpec((2, NSA_HEADS, TQ, TQ), lambda bb, i: (0, 0, 0, 0))],
        out_specs=pl.BlockSpec((None, TQ, NSA_W), lambda bb, i: (bb, i, 0)),
        out_shape=jax.ShapeDtypeStruct((b, t, NSA_W), F32),
        scratch_shapes=_flash_scratch(NSA_HEADS),
        compiler_params=_cparams(2, 40),
        name="nsa_slc_prompt",
    )(qk, qk, qk, kv_t, kv_t, dn, btiles)


def _nsa_dense_specs(t, k_head):
    return [pl.BlockSpec((None, NSA_HEADS, TQ, HEAD_DIM), lambda bb, i: (bb, H_QA // NSA_HEADS, i, 0)),
            pl.BlockSpec((None, None, t, HEAD_DIM), lambda bb, i: (bb, k_head, 0, 0)),
            pl.BlockSpec((None, None, t, HEAD_DIM), lambda bb, i: (bb, k_head + 1, 0, 0)),
            pl.BlockSpec((None, HEAD_DIM, t), lambda bb, i: (bb, NSA_KV, 0)),
            pl.BlockSpec((None, HEAD_DIM, t), lambda bb, i: (bb, NSA_KV + 1, 0))]


def _nsa_win_kernel(q_ref, k0_ref, k1_ref, v0_ref, v1_ref, bt_ref, o_ref, *scr):
    i = pl.program_id(1)
    states = _flash_states(scr)
    qs = [q_ref[h] for h in range(NSA_HEADS)]
    for st in states:
        _softmax_init(*st)

    def tile(kt, adds):
        ks = [_k_tile(k0_ref, kt)] * NSA_REP + [_k_tile(k1_ref, kt)] * NSA_REP
        v_ts = [_v_tile(v0_ref, kt)] * NSA_REP + [_v_tile(v1_ref, kt)] * NSA_REP
        _flash_tile(qs, ks, v_ts, states, adds)

    @pl.when(i >= WINDOW // TQ)
    def _():
        key = lax.broadcasted_iota(jnp.int32, (TQ, TQ), 0)
        qry = lax.broadcasted_iota(jnp.int32, (TQ, TQ), 1)
        tile(i - WINDOW // TQ, [[jnp.where(key > qry, 0.0, NEG_INF)]] * NSA_HEADS)

    @pl.when(i >= 1)
    def _():
        tile(i - 1, [[bt_ref[1, h]] for h in range(NSA_HEADS)])

    tile(i, [[bt_ref[0, h]] for h in range(NSA_HEADS)])
    o_ref[...] = jnp.concatenate([_softmax_finish(st[1], st[2]) for st in states], axis=1)


def _nsa_win_prompt(qk, kv_t, btiles):
    b, _, t, _ = qk.shape
    return pl.pallas_call(
        _nsa_win_kernel,
        grid=(b, t // TQ),
        in_specs=_nsa_dense_specs(t, H_KW) + [
            pl.BlockSpec((2, NSA_HEADS, TQ, TQ), lambda bb, i: (0, 0, 0, 0))],
        out_specs=pl.BlockSpec((None, TQ, NSA_W), lambda bb, i: (bb, i, 0)),
        out_shape=jax.ShapeDtypeStruct((b, t, NSA_W), F32),
        scratch_shapes=_flash_scratch(NSA_HEADS),
        compiler_params=_cparams(2, 40),
        name="nsa_win_prompt",
    )(qk, qk, qk, kv_t, kv_t, btiles)


def _kmean_kernel(k_ref, o_ref, *, nblk):
    lane = lax.broadcasted_iota(jnp.int32, (MOBA_W, LANES), 1)
    acc = jnp.zeros((MOBA_W, LANES), F32)
    for blk in range(nblk):
        mean = jnp.sum(k_ref[:, blk * MOBA_BLOCK:(blk + 1) * MOBA_BLOCK], axis=1, keepdims=True) / MOBA_BLOCK
        acc = jnp.where(lane == blk, mean, acc)
    for h in range(MOBA_HEADS):
        o_ref[h] = acc[h * HEAD_DIM:(h + 1) * HEAD_DIM, :].T.astype(BF16)


def _kmean_prompt(moba_t):
    bx, _, t = moba_t.shape
    return pl.pallas_call(
        functools.partial(_kmean_kernel, nblk=t // MOBA_BLOCK),
        grid=(bx,),
        in_specs=[pl.BlockSpec((None, MOBA_W, t), lambda b: (b, 0, 0))],
        out_specs=pl.BlockSpec((None, MOBA_HEADS, LANES, HEAD_DIM), lambda b: (b, 0, 0, 0)),
        out_shape=jax.ShapeDtypeStruct((bx, MOBA_HEADS, LANES, HEAD_DIM), BF16),
        compiler_params=_cparams(1, 40),
        name="moba_kmean_prompt",
    )(moba_t)


def _moba_kernel(q_ref, k_ref, v_ref, km_ref, bt_ref, o_ref, *scr, nblk, k_m):
    i = pl.program_id(2)
    hb = q_ref.shape[0]
    states = _flash_states(scr[:3 * hb])
    dn_scr = scr[3 * hb:]
    qs = [q_ref[hh] for hh in range(hb)]
    for hh in range(hb):
        gs = _dot_nt(km_ref[hh], qs[hh])
        blk = lax.broadcasted_iota(jnp.int32, gs.shape, 0)
        dn_scr[hh][...] = _topk_drop_rows(jnp.where(blk < i, gs, NEG_INF), nblk, k_m)
        _softmax_init(*states[hh])

    def tile(kt, bias_kind, masked):
        start = pl.multiple_of(kt * TQ, TQ)
        ks = [k_ref[hh, pl.ds(start, TQ), :] for hh in range(hb)]
        v_ts = [v_ref[hh * HEAD_DIM:(hh + 1) * HEAD_DIM, pl.ds(start, TQ)].astype(BF16) for hh in range(hb)]
        adds = [[] for _ in range(hb)]
        if masked:
            adds = [[dn_scr[hh][pl.ds(kt, 1), :]] for hh in range(hb)]
        if bias_kind is not None:
            adds = [a + [bt_ref[bias_kind, hh]] for hh, a in enumerate(adds)]
        _flash_tile(qs, ks, v_ts, states, adds)

    _far_tiles(jnp.maximum(i - 1, 0), lambda kt: tile(kt, None, True))

    @pl.when(i >= 1)
    def _():
        tile(i - 1, 1, True)

    tile(i, 0, False)
    o_ref[...] = jnp.concatenate([_softmax_finish(st[1], st[2]) for st in states], axis=1)


def _moba_prompt(qk, moba_t, kmean, btiles, k_m):
    b, _, t, _ = qk.shape
    hb = MOBA_HB
    nq = t // TQ
    vb = MOBA_W // (hb * HEAD_DIM)
    once = dict(pipeline_mode=pl.Buffered(1))
    return pl.pallas_call(
        functools.partial(_moba_kernel, nblk=t // MOBA_BLOCK, k_m=k_m),
        grid=(b, MOBA_HEADS // hb, nq),
        in_specs=[pl.BlockSpec((None, hb, TQ, HEAD_DIM), lambda bb, hp, i: (bb, H_QB // hb + hp, i, 0)),
                  pl.BlockSpec((None, hb, t, HEAD_DIM), lambda bb, hp, i: (bb, H_KB // hb + hp, 0, 0), **once),
                  pl.BlockSpec((None, hb * HEAD_DIM, t), lambda bb, hp, i: (bb, vb + hp, 0), **once),
                  pl.BlockSpec((None, hb, LANES, HEAD_DIM), lambda bb, hp, i: (bb, hp, 0, 0)),
                  pl.BlockSpec((2, hb, TQ, TQ), lambda bb, hp, i: (0, NSA_HEADS // hb + hp, 0, 0), **once)],
        out_specs=pl.BlockSpec((None, TQ, hb * HEAD_DIM), lambda bb, hp, i: (bb, i, hp)),
        out_shape=jax.ShapeDtypeStruct((b, t, MOBA_W), F32),
        scratch_shapes=_flash_scratch(hb) + [pltpu.VMEM((LANES, TQ), F32)] * hb,
        compiler_params=_cparams(3, 48),
        name="moba_prompt",
    )(qk, qk, moba_t, kmean, btiles)


def _outproj_kernel(x_ref, gate_ref, az_ref, bz_ref, ma_ref, mb_ref, ag_ref, oc_ref, os_ref, ow_ref, ob_ref,
                    eg_ref, wua_ref, wub_ref, wo_ref, lng_ref, lnb_ref, y_ref, *, alpha):
    w = NSA_W
    g = _sigmoid(ag_ref[...])
    g_hi = g.astype(BF16)
    g_lo = (g - g_hi.astype(F32)).astype(BF16)
    ge = _dot(g_hi, eg_ref[...]) + _dot(g_lo, eg_ref[...])
    o_a = ge[:, 0:w] * oc_ref[...] + ge[:, w:2 * w] * os_ref[...] + ge[:, 2 * w:3 * w] * ow_ref[...]
    y_a = _dot((o_a * _silu(az_ref[...])).astype(BF16), wua_ref[...])
    y_b = _dot((ob_ref[...] * _silu(bz_ref[...])).astype(BF16), wub_ref[...])
    mixed = _dot((_sigmoid(ma_ref[...]) * y_a + _sigmoid(mb_ref[...]) * y_b).astype(BF16), wo_ref[...])
    z = alpha * x_ref[...] + gate_ref[...] * mixed
    mu = jnp.mean(z, axis=-1, keepdims=True)
    var = jnp.mean(jnp.square(z - mu), axis=-1, keepdims=True)
    y_ref[...] = (z - mu) * lax.rsqrt(var + LN_EPS) * lng_ref[...] + lnb_ref[...]


def _outproj(x, gate, z, o_c, o_s, o_w, o_b, eg, wua, wub, wo, ln_g, ln_b, alpha):
    bx, t, d = x.shape
    tm = min(t, 256)
    nt = t // tm

    def tok(width, col):
        return pl.BlockSpec((None, tm, width), lambda m: (m // nt, m % nt, col))

    return pl.pallas_call(
        functools.partial(_outproj_kernel, alpha=alpha),
        grid=(bx * nt,),
        in_specs=[tok(d, 0), _mod_spec(gate, tm, nt),
                  tok(NSA_W, Z_AZ // NSA_W), tok(MOBA_W, Z_BZ // MOBA_W), tok(d, Z_MA // d), tok(d, Z_MB // d),
                  tok(AG_PAD, Z_AG // AG_PAD), tok(NSA_W, 0), tok(NSA_W, 0), tok(NSA_W, 0), tok(MOBA_W, 0),
                  _full_spec(eg, 1), _full_spec(wua, 1), _full_spec(wub, 1), _full_spec(wo, 1),
                  _full_spec(ln_g, 1), _full_spec(ln_b, 1)],
        out_specs=tok(d, 0),
        out_shape=jax.ShapeDtypeStruct((bx, t, d), F32),
        compiler_params=_cparams(1, 48),
        name="out_proj",
    )(x, gate, z, z, z, z, z, o_c, o_s, o_w, o_b, eg, wua, wub, wo, ln_g, ln_b)


def _dec_attend(s, v_t, s_self, v_self):
    m = jnp.maximum(jnp.max(s, axis=1, keepdims=True), s_self)
    e = jnp.exp(s - m)
    e_self = jnp.exp(s_self - m)
    den = jnp.maximum(jnp.sum(e, axis=1, keepdims=True) + e_self, 1e-30)
    return (_dot_nt(e.astype(BF16), v_t) + e_self * v_self) / den


def _dec_cmp_kernel(tab_ref, q_ref, ck_ref, cv_ref, ov_ref, oc_ref, imp_ref, *, pos):
    ncp = ck_ref.shape[2]
    jl = ov_ref.shape[1]
    cend = lax.broadcasted_iota(jnp.int32, (SUBLANES, ncp), 1) * CMP_STRIDE + (CMP_LEN - 1)
    dist = pos - cend
    valid = dist >= 0
    bkt = _bucket(dist)
    j = lax.broadcasted_iota(jnp.int32, (SUBLANES, jl), 1)
    cur = pos // SLC_BLOCK
    for g in range(NSA_KV):
        s = jnp.where(valid, _dot(q_ref[g], ck_ref[g]) + _bias_rows(bkt, tab_ref[g]), NEG_INF)
        m = jnp.max(s, axis=1, keepdims=True)
        m = jnp.where(m == NEG_INF, 0.0, m)
        e = jnp.exp(s - m)
        p = (e / jnp.maximum(jnp.sum(e, axis=1, keepdims=True), 1e-30)).astype(BF16)
        oc_ref[g] = _dot_nt(p, cv_ref[g])
        imp4 = _dot(p, ov_ref[...])
        imp = imp4[0:1]
        for r in range(1, NSA_REP):
            imp = imp + imp4[r:r + 1]
        imp = jnp.broadcast_to(imp, (SUBLANES, jl))
        imp = jnp.where((j == 0) | (j == cur) | (j == cur - 1), jnp.inf, imp)
        imp_ref[g] = jnp.where(j <= cur, imp, NEG_INF)


def _dec_cmp(tab_g, q8, ck_t, cv_t, overlap, pos):
    db = q8.shape[0]
    ncp = ck_t.shape[3]
    jl = overlap.shape[1]
    cspec = pl.BlockSpec((None, NSA_KV, HEAD_DIM, ncp), lambda b: (b, 0, 0, 0))
    return pl.pallas_call(
        functools.partial(_dec_cmp_kernel, pos=pos),
        grid=(db,),
        in_specs=[_full_spec(tab_g, 1),
                  pl.BlockSpec((None, NSA_KV, SUBLANES, HEAD_DIM), lambda b: (b, 0, 0, 0)),
                  cspec, cspec, _full_spec(overlap, 1)],
        out_specs=[pl.BlockSpec((None, NSA_KV, SUBLANES, HEAD_DIM), lambda b: (b, 0, 0, 0)),
                   pl.BlockSpec((None, NSA_KV, SUBLANES, jl), lambda b: (b, 0, 0, 0))],
        out_shape=[jax.ShapeDtypeStruct((db, NSA_KV, SUBLANES, HEAD_DIM), F32),
                   jax.ShapeDtypeStruct((db, NSA_KV, SUBLANES, jl), F32)],
        compiler_params=_cparams(1),
        name="dec_nsa_cmp",
    )(tab_g, q8, ck_t, cv_t, overlap)


def _topk_idx_kernel(s_ref, idx_ref, *, n, k):
    st = s_ref[...].T
    nb = -(-n // SUBLANES)
    blocks = [st[rb * SUBLANES:(rb + 1) * SUBLANES, :] for rb in range(nb)]
    cnt = _rank_rows(blocks, n)
    sub = lax.broadcasted_iota(jnp.int32, blocks[0].shape, 0)
    rows = []
    for r in range(idx_ref.shape[0]):
        if r >= k:
            rows.append(jnp.full((1, st.shape[1]), -1, jnp.int32))
            continue
        acc = jnp.zeros(blocks[0].shape, jnp.int32)
        for rb in range(nb):
            hit = jnp.where(cnt[rb] == float(r), jnp.where(blocks[rb] > NEG_INF, 1, 0), 0)
            acc = acc + hit * (sub + (rb * SUBLANES + 1))
        rows.append(jnp.sum(acc, axis=0, keepdims=True) - 1)
    idx_ref[...] = jnp.concatenate(rows, axis=0)


def _topk_idx(scores, n, k):
    nrow, jl = scores.shape
    kp = -(-k // SUBLANES) * SUBLANES
    return pl.pallas_call(
        functools.partial(_topk_idx_kernel, n=n, k=k),
        grid=(1,),
        in_specs=[pl.BlockSpec((nrow, jl), lambda i: (0, 0))],
        out_specs=pl.BlockSpec((kp, nrow), lambda i: (0, 0)),
        out_shape=jax.ShapeDtypeStruct((kp, nrow), jnp.int32),
        compiler_params=_cparams(1),
        name="topk_idx",
    )(scores)


def _dec_slc_kernel(sel_ref, pt_ref, *refs, pos, n_slc, k_sel):
    tab_ref, q_ref, kvn_ref, o_ref = refs[NSA_KV * 2 * k_sel:]
    b = pl.program_id(0)
    n_keys = k_sel * PAGE_SIZE
    lane = lax.broadcasted_iota(jnp.int32, (SUBLANES, n_keys), 1)
    half = PAGE_SIZE // SLC_BLOCK
    for g in range(NSA_KV):
        k_refs = refs[g * 2 * k_sel:g * 2 * k_sel + k_sel]
        v_refs = refs[g * 2 * k_sel + k_sel:(g + 1) * 2 * k_sel]
        kpos = jnp.zeros((SUBLANES, n_keys), jnp.int32)
        for k in range(k_sel):
            jk = sel_ref[(b * NSA_KV + g) * k_sel + k]
            ok = (jk >= 0) & (jk < n_slc - 1)
            in_blk = (lane % PAGE_SIZE) // SLC_BLOCK == jk % half
            here = jnp.where(in_blk, (jk // half) * PAGE_SIZE + lane % PAGE_SIZE, pos + 1)
            kpos = jnp.where(lane // PAGE_SIZE == k, jnp.where(ok, here, pos + 1), kpos)
        dist = pos - kpos
        tab = tab_ref[g]
        q = q_ref[g]
        k_t = jnp.concatenate([r[...] for r in k_refs], axis=1).astype(BF16)
        v_t = jnp.concatenate([r[...] for r in v_refs], axis=1).astype(BF16)
        s = jnp.where(dist >= 0, _dot(q, k_t) + _bias_rows(_bucket(dist), tab), NEG_INF)
        s_self = jnp.sum(q.astype(F32) * kvn_ref[g:g + 1, :], axis=1, keepdims=True) + tab[:, 0:1]
        o_ref[g] = _dec_attend(s, v_t, s_self, kvn_ref[NSA_KV + g:NSA_KV + g + 1, :])


def _dec_slc(sel_flat, pt_flat, cache_t, layer, tab_g, q8, kv_new, pos, n_slc, k_sel, n_pages):
    db = q8.shape[0]
    half = PAGE_SIZE // SLC_BLOCK

    def blk_spec(g, k, kv):
        def imap(b, sel, pt):
            j = jnp.clip(sel[(b * NSA_KV + g) * k_sel + k], 0, n_slc - 2)
            return (pt[b * n_pages + j // half], layer, kv, g, 0, 0)
        return pl.BlockSpec((None, None, None, None, HEAD_DIM, PAGE_SIZE), imap)

    qspec = pl.BlockSpec((None, NSA_KV, SUBLANES, HEAD_DIM), lambda b, sel, pt: (b, 0, 0, 0))
    return pl.pallas_call(
        functools.partial(_dec_slc_kernel, pos=pos, n_slc=n_slc, k_sel=k_sel),
        grid_spec=pltpu.PrefetchScalarGridSpec(
            num_scalar_prefetch=2,
            grid=(db,),
            in_specs=[blk_spec(g, k, kv) for g in range(NSA_KV) for kv in range(2) for k in range(k_sel)] + [
                _full_spec(tab_g, 3), qspec,
                pl.BlockSpec((None, 2 * NSA_KV, HEAD_DIM), lambda b, sel, pt: (b, 0, 0))],
            out_specs=qspec),
        out_shape=jax.ShapeDtypeStruct((db, NSA_KV, SUBLANES, HEAD_DIM), F32),
        compiler_params=_cparams(1),
        name="dec_nsa_slc",
    )(sel_flat, pt_flat, *([cache_t] * (NSA_KV * 2 * k_sel)), tab_g, q8, kv_new)


def _dec_win_kernel(k0_ref, k1_ref, v0_ref, v1_ref, tab_ref, q_ref, kvn_ref, o_ref):
    n = k0_ref.shape[1]
    dist = n - lax.broadcasted_iota(jnp.int32, (SUBLANES, n), 1)
    bkt = _bucket(dist)
    for g, (k_ref, v_ref) in enumerate(((k0_ref, v0_ref), (k1_ref, v1_ref))):
        tab = tab_ref[g]
        q = q_ref[g]
        s = jnp.where(dist < WINDOW, _dot(q, k_ref[...].astype(BF16)) + _bias_rows(bkt, tab), NEG_INF)
        s_self = jnp.sum(q.astype(F32) * kvn_ref[g:g + 1, :], axis=1, keepdims=True) + tab[:, 0:1]
        o_ref[g] = _dec_attend(s, v_ref[...].astype(BF16), s_self, kvn_ref[NSA_KV + g:NSA_KV + g + 1, :])


def _dec_win(state_t, layer, tab_g, q8, kv_new):
    db = q8.shape[0]
    n = state_t.shape[5]

    def st_spec(kv, g):
        return pl.BlockSpec((None, None, None, None, HEAD_DIM, n), lambda b: (layer, b, kv, g, 0, 0))

    return pl.pallas_call(
        _dec_win_kernel,
        grid=(db,),
        in_specs=[st_spec(0, 0), st_spec(0, 1), st_spec(1, 0), st_spec(1, 1),
                  _full_spec(tab_g, 1),
                  pl.BlockSpec((None, NSA_KV, SUBLANES, HEAD_DIM), lambda b: (b, 0, 0, 0)),
                  pl.BlockSpec((None, 2 * NSA_KV, HEAD_DIM), lambda b: (b, 0, 0))],
        out_specs=pl.BlockSpec((None, NSA_KV, SUBLANES, HEAD_DIM), lambda b: (b, 0, 0, 0)),
        out_shape=jax.ShapeDtypeStruct((db, NSA_KV, SUBLANES, HEAD_DIM), F32),
        compiler_params=_cparams(1),
        name="dec_nsa_win",
    )(state_t, state_t, state_t, state_t, tab_g, q8, kv_new)


def _dec_moba_sweep_kernel(pt_ref, *refs, n_x):
    k_refs = refs[:n_x]
    qb_ref, s_ref = refs[n_x:]
    qb = qb_ref[...]
    for k, k_ref in enumerate(k_refs):
        prod = k_ref[...] * qb
        s_ref[:, k * PAGE_SIZE:(k + 1) * PAGE_SIZE] = jnp.concatenate(
            [jnp.sum(prod[h * HEAD_DIM:(h + 1) * HEAD_DIM], axis=0, keepdims=True) for h in range(MOBA_HEADS)],
            axis=0)


def _dec_moba_sweep(cache_t, page_table, layer, q_lanes):
    db, n_pages = page_table.shape
    pps = PAGES_PER_STEP

    def page_spec(k):
        return pl.BlockSpec((None, None, None, MOBA_W, PAGE_SIZE),
                            lambda b, i, pt: (pt[b * n_pages + i * pps + k], layer, 0, 0, 0))

    return pl.pallas_call(
        functools.partial(_dec_moba_sweep_kernel, n_x=pps),
        grid_spec=pltpu.PrefetchScalarGridSpec(
            num_scalar_prefetch=1,
            grid=(db, n_pages // pps),
            in_specs=[page_spec(k) for k in range(pps)] + [
                pl.BlockSpec((None, MOBA_W, PAGE_SIZE), lambda b, i, pt: (b, 0, 0))],
            out_specs=pl.BlockSpec((None, MOBA_HEADS, pps * PAGE_SIZE), lambda b, i, pt: (b, 0, i))),
        out_shape=jax.ShapeDtypeStruct((db, MOBA_HEADS, n_pages * PAGE_SIZE), F32),
        compiler_params=_cparams(2, 40),
        name="dec_moba_sweep",
    )(page_table.reshape(-1), *([cache_t] * pps), q_lanes)


def _dec_moba_gate_kernel(s_ref, gs_ref, *, nblk):
    lane = lax.broadcasted_iota(jnp.int32, gs_ref.shape, 1)
    gs = jnp.full(gs_ref.shape, NEG_INF, F32)
    for blk in range(nblk):
        mean = jnp.sum(s_ref[:, blk * MOBA_BLOCK:(blk + 1) * MOBA_BLOCK], axis=1, keepdims=True) / MOBA_BLOCK
        gs = jnp.where(lane == blk, mean, gs)
    gs_ref[...] = gs


def _dec_moba_gate(s_all, nblk):
    db, _, p = s_all.shape
    return pl.pallas_call(
        functools.partial(_dec_moba_gate_kernel, nblk=nblk),
        grid=(db,),
        in_specs=[pl.BlockSpec((None, MOBA_HEADS, p), lambda b: (b, 0, 0))],
        out_specs=pl.BlockSpec((None, MOBA_HEADS, LANES), lambda b: (b, 0, 0)),
        out_shape=jax.ShapeDtypeStruct((db, MOBA_HEADS, LANES), F32),
        compiler_params=_cparams(1),
        name="dec_moba_gate",
    )(s_all)


def _dec_moba_attend_kernel(sel_ref, pt_ref, *refs, pos, k_m):
    ppb = MOBA_BLOCK // PAGE_SIZE
    per_h = k_m + k_m * ppb
    tab_ref, q_ref, kn_ref, vn_ref, o_ref = refs[MOBA_HEADS * per_h:]
    b = pl.program_id(0)
    n_keys = k_m * MOBA_BLOCK
    lane = lax.broadcasted_iota(jnp.int32, (SUBLANES, n_keys), 1)
    tab = tab_ref[...]
    s_self = jnp.sum(q_ref[...] * kn_ref[...], axis=1, keepdims=True) + tab[:, 0:1]
    outs = []
    for h in range(MOBA_HEADS):
        s_refs = refs[h * per_h:h * per_h + k_m]
        v_refs = refs[h * per_h + k_m:(h + 1) * per_h]
        kpos = jnp.zeros((SUBLANES, n_keys), jnp.int32)
        for k in range(k_m):
            jk = sel_ref[(b * MOBA_HEADS + h) * k_m + k]
            kpos = jnp.where(lane // MOBA_BLOCK == k,
                             jnp.where(jk >= 0, jk * MOBA_BLOCK + lane % MOBA_BLOCK, pos + 1), kpos)
        dist = pos - kpos
        tab_h = jnp.broadcast_to(tab[h:h + 1], (SUBLANES, N_BUCKETS))
        s = jnp.broadcast_to(jnp.concatenate([r[h:h + 1, :] for r in s_refs], axis=1), (SUBLANES, n_keys))
        s = jnp.where(dist >= 0, s + _bias_rows(_bucket(dist), tab_h), NEG_INF)
        v_t = jnp.concatenate([r[...] for r in v_refs], axis=1).astype(BF16)
        o = _dec_attend(s, v_t, jnp.broadcast_to(s_self[h:h + 1], (SUBLANES, 1)), vn_ref[h:h + 1, :])
        outs.append(o[0:1])
    o_ref[...] = jnp.concatenate(outs, axis=0)


def _dec_moba_attend(sel_flat, pt_flat, s_all, cache_t, layer, tab_h, q, k_new, v_new, pos, k_m, n_pages, nblk):
    db = q.shape[0]
    ppb = MOBA_BLOCK // PAGE_SIZE

    def sel_of(b, h, sel, k):
        return jnp.clip(sel[(b * MOBA_HEADS + h) * k_m + k], 0, nblk - 1)

    def s_spec(h, k):
        return pl.BlockSpec((None, MOBA_HEADS, MOBA_BLOCK), lambda b, sel, pt: (b, 0, sel_of(b, h, sel, k)))

    def v_spec(h, k, pg):
        return pl.BlockSpec(
            (None, None, None, None, HEAD_DIM, PAGE_SIZE),
            lambda b, sel, pt: (pt[b * n_pages + sel_of(b, h, sel, k) * ppb + pg], layer, 1, h, 0, 0))

    head_specs, head_args = [], []
    for h in range(MOBA_HEADS):
        head_specs += [s_spec(h, k) for k in range(k_m)] + [v_spec(h, k, pg) for k in range(k_m) for pg in range(ppb)]
        head_args += [s_all] * k_m + [cache_t] * (k_m * ppb)
    row_spec = pl.BlockSpec((None, MOBA_HEADS, HEAD_DIM), lambda b, sel, pt: (b, 0, 0))
    return pl.pallas_call(
        functools.partial(_dec_moba_attend_kernel, pos=pos, k_m=k_m),
        grid_spec=pltpu.PrefetchScalarGridSpec(
            num_scalar_prefetch=2,
            grid=(db,),
            in_specs=head_specs + [_full_spec(tab_h, 3), row_spec, row_spec, row_spec],
            out_specs=row_spec),
        out_shape=jax.ShapeDtypeStruct((db, MOBA_HEADS, HEAD_DIM), F32),
        compiler_params=_cparams(1),
        name="dec_moba_attend",
    )(sel_flat, pt_flat, *head_args, tab_h, q, k_new, v_new)


def _overlap_matrix(n_cmp, n_slc, rows, cols):
    i = np.arange(n_cmp)[:, None]
    j = np.arange(n_slc)[None, :]
    units = SLC_BLOCK // CMP_STRIDE
    m = sum(((i + u) // units == j).astype(np.float32) for u in range(CMP_LEN // CMP_STRIDE))
    out = np.zeros((rows, cols), np.float32)
    out[:n_cmp, :n_slc] = m
    return jnp.asarray(out, dtype=BF16)


def _token_group_permutation():
    groups = LANES // CMP_STRIDE
    p = np.zeros((2 * LANES, 2 * LANES), np.float32)
    for c in range(2):
        for l in range(CMP_STRIDE):
            for m in range(groups):
                p[c * LANES + l * groups + m, c * LANES + CMP_STRIDE * m + l] = 1.0
    return jnp.asarray(p, dtype=BF16)


def _gate_expand_matrix():
    e = np.zeros((AG_PAD, 3 * NSA_W), np.float32)
    for h in range(NSA_HEADS):
        for br in range(3):
            e[h * 3 + br, br * NSA_W + h * HEAD_DIM: br * NSA_W + (h + 1) * HEAD_DIM] = 1.0
    return jnp.asarray(e, dtype=BF16)


def _split_w_in(w_in):
    scale = HEAD_DIM ** -0.5
    o = np.cumsum([0, 512, 128, 128, 128, 128, 128, 128, 24, 512, 512, 512, 512, 512, 1024, 1024])
    a_q, kv3, a_g, a_z, b_q, b_kv, b_z, m_ab = (
        w_in[..., o[0]:o[1]], w_in[..., o[1]:o[7]], w_in[..., o[7]:o[8]], w_in[..., o[8]:o[9]],
        w_in[..., o[9]:o[10]], w_in[..., o[10]:o[12]], w_in[..., o[12]:o[13]], w_in[..., o[13]:o[15]])
    pad = jnp.zeros(w_in.shape[:-1] + (AG_PAD - a_g.shape[-1],), w_in.dtype)
    k_rm = jnp.concatenate([w_in[..., o[10]:o[11]], w_in[..., o[3]:o[4]], w_in[..., o[5]:o[6]]], axis=-1)
    wq = jnp.concatenate([a_q * scale, b_q * scale, k_rm], axis=-1).astype(BF16)
    wz = jnp.concatenate([a_z, b_z, m_ab, a_g, pad], axis=-1).astype(BF16)
    wkv = jnp.swapaxes(jnp.concatenate([kv3, b_kv], axis=-1), 1, 2).astype(BF16)
    return wq, wz, wkv


def _cmp_stage1_weights(w1):
    w1r = w1.reshape(2, CMP_STRIDE, HEAD_DIM, HEAD_DIM)
    eye = jnp.eye(NSA_KV, dtype=w1.dtype)
    w = jnp.einsum("hlde,gf->lgdhfe", w1r, eye)
    return w.reshape(CMP_STRIDE * NSA_KV * HEAD_DIM, 2 * NSA_KV * HEAD_DIM).astype(BF16)


def _cmp_stage2_weights(w2):
    w2t = jnp.swapaxes(w2, 1, 2)
    z = jnp.zeros_like(w2t)
    return jnp.stack([jnp.concatenate([w2t, z], axis=2), jnp.concatenate([z, w2t], axis=2)], axis=1).astype(BF16)


def _cache_view(c):
    return jnp.transpose(c, (0, 1, 3, 4, 5, 2))


def _kv_output(kv_t, heads):
    st = jnp.stack(kv_t, axis=1)
    b, depth, _, t = st.shape
    return jnp.transpose(st.reshape(b, depth, 2, heads, HEAD_DIM, t), (0, 1, 5, 2, 3, 4))


def _layer_prompt(x, mod, lw, consts):
    b, t, d = x.shape
    shift, scale, gate = mod
    q, z, cmp_t, slc_t, win_t, moba_t = _inproj(x, scale, shift, lw["wq"], lw["wz"], lw["wkv"])
    n_slc = t // SLC_BLOCK
    k_sel = min(SLC_TOPK, n_slc)
    abk, abv = _cmp_proj_prompt(cmp_t, consts["perm"], lw["cmp_wk"], lw["cmp_wv"])
    _, cv_t, ck = _cmp_mlp(abk, abv, lw["pos_flat"], lw["phi_w1"], lw["phi_b1"], lw["cmp_w2t"], lw["phi_b2"])
    o_c, dn = _nsa_cmp_prompt(consts["rel_bias"], q, ck, cv_t, consts["overlap_p"], n_slc, k_sel)
    o_s = _nsa_slc_prompt(q, slc_t, dn, consts["btiles"])
    o_w = _nsa_win_prompt(q, win_t, consts["btiles"])
    nblk = t // MOBA_BLOCK
    o_b = _moba_prompt(q, moba_t, _kmean_prompt(moba_t), consts["btiles"], min(MOBA_TOPK, nblk - 1))
    y = _outproj(x, gate, z, o_c, o_s, o_w, o_b, consts["eg"], lw["w_up_a"], lw["w_up_b"], lw["w_out"],
                 lw["ln_g"], lw["ln_b"], consts["alpha"])
    return y, dict(cmp=cmp_t, slc=slc_t, win=win_t[:, :, t - min(WINDOW, t):], moba=moba_t)


def _layer_sample(x, mod, lw, consts, layer, caches, page_table):
    _, db, d = x.shape
    shift, scale, gate = mod
    cache_cmp, cache_slc, cache_moba, state_win = caches
    n_phys, depth = cache_cmp.shape[:2]
    n_pages = page_table.shape[1]
    pos = n_pages * PAGE_SIZE
    pt_flat = page_table.reshape(-1)
    q, z, cmp_t, slc_t, win_t, moba_t = _inproj(x, scale, shift, lw["wq"], lw["wz"], lw["wkv"])
    cmp_n, slc_n, win_n = (a[0].T.reshape(db, 2 * NSA_KV, HEAD_DIM) for a in (cmp_t, slc_t, win_t))
    moba_n = moba_t[0].T.reshape(db, 2, MOBA_HEADS, HEAD_DIM)
    qa = jnp.transpose(q[0, H_QA:H_QA + NSA_HEADS], (1, 0, 2)).reshape(db, NSA_KV, NSA_REP, HEAD_DIM)
    q8 = jnp.pad(qa, ((0, 0), (0, 0), (0, SUBLANES - NSA_REP), (0, 0)))
    qb = jnp.transpose(q[0, H_QB:H_QB + MOBA_HEADS], (1, 0, 2)).astype(F32)
    abk, abv = _cmp_proj_paged(cache_cmp.reshape(n_phys, depth, KV_W, PAGE_SIZE), page_table, layer,
                               consts["perm"], lw["cmp_wk"], lw["cmp_wv"])
    ck_t, cv_t, _ = _cmp_mlp(abk, abv, lw["pos_flat"], lw["phi_w1"], lw["phi_b1"], lw["cmp_w2t"], lw["phi_b2"])
    n_slc = pos // SLC_BLOCK + 1
    k_sel = min(SLC_TOPK, n_slc)
    o_c8, imp = _dec_cmp(consts["tab_g"], q8, ck_t, cv_t, consts["overlap_s"], pos)
    imp2 = imp[:, :, 0].reshape(db * NSA_KV, -1)
    imp2 = jnp.pad(imp2, ((0, LANES - db * NSA_KV), (0, 0)), constant_values=NEG_INF)
    sel = _topk_idx(imp2, n_slc, k_sel)[:k_sel, :db * NSA_KV].T.reshape(-1)
    o_s8 = _dec_slc(sel, pt_flat, cache_slc, layer, consts["tab_g"], q8, slc_n, pos, n_slc, k_sel, n_pages)
    o_w8 = _dec_win(state_win, layer, consts["tab_g"], q8, win_n)
    o_c, o_s, o_w = (a[:, :, :NSA_REP].reshape(1, db, NSA_W) for a in (o_c8, o_s8, o_w8))
    nblk = pos // MOBA_BLOCK
    k_m = min(MOBA_TOPK, nblk)
    q_lanes = jnp.broadcast_to(qb.reshape(db, MOBA_W, 1), (db, MOBA_W, PAGE_SIZE))
    s_all = _dec_moba_sweep(cache_moba.reshape(n_phys, depth, 2, MOBA_W, PAGE_SIZE), page_table, layer, q_lanes)
    gs = _dec_moba_gate(s_all, nblk).reshape(db * MOBA_HEADS, LANES)
    sel_m = _topk_idx(gs, nblk, k_m)[:k_m].T.reshape(-1)
    o_b8 = _dec_moba_attend(sel_m, pt_flat, s_all, cache_moba, layer, consts["tab_h"], qb, moba_n[:, 0],
                            moba_n[:, 1], pos, k_m, n_pages, nblk)
    o_b = o_b8.reshape(1, db, MOBA_W)
    y = _outproj(x, gate, z, o_c, o_s, o_w, o_b, consts["eg"], lw["w_up_a"], lw["w_up_b"], lw["w_out"],
                 lw["ln_g"], lw["ln_b"], consts["alpha"])
    new = dict(cmp=cmp_n.reshape(db, 1, 2, NSA_KV, HEAD_DIM), slc=slc_n.reshape(db, 1, 2, NSA_KV, HEAD_DIM),
               win=win_n.reshape(db, 1, 2, NSA_KV, HEAD_DIM), moba=moba_n.reshape(db, 1, 2, MOBA_HEADS, HEAD_DIM))
    return y, new


def kernel(x_prompt, x_sample, cache_nsa_cmp, cache_nsa_slc, cache_moba, state_nsa_win, page_table, c_prompt, c_sample, rel_bias, w_ada, b_ada, w_in, phi_pos, phi_w1, phi_b1, phi_w2, phi_b2, w_up_a, w_up_b, w_out, ln_g, ln_b):
    b, t, d = x_prompt.shape
    db = x_sample.shape[0]
    depth = w_ada.shape[0]
    n_pages = page_table.shape[1]
    pos = n_pages * PAGE_SIZE
    assert x_sample.shape[1] == 1 and t % TQ == 0 and t >= WINDOW and n_pages % PAGES_PER_STEP == 0
    assert db * NSA_KV <= LANES and state_nsa_win.shape[2] == WINDOW and pos // MOBA_BLOCK >= 1
    assert t // MOBA_BLOCK <= LANES and t // SLC_BLOCK <= LANES

    mc = -(-(b + db) // SUBLANES) * SUBLANES
    c_all = jnp.pad(jnp.concatenate([c_prompt, c_sample], axis=0), ((0, mc - b - db), (0, 0)))
    mod = _ada(c_all, w_ada, b_ada)

    wq, wz, wkv = _split_w_in(w_in)
    tab_rel = (rel_bias - rel_bias[N_BUCKETS - 1][None, :]).T
    tab_g = jnp.pad(tab_rel[:NSA_HEADS].reshape(NSA_KV, NSA_REP, N_BUCKETS),
                    ((0, 0), (0, SUBLANES - NSA_REP), (0, 0)))
    n_slc_s = pos // SLC_BLOCK + 1
    consts = dict(
        rel_bias=rel_bias,
        alpha=float((2 * depth) ** 0.25),
        btiles=_bias_tiles(rel_bias),
        perm=_token_group_permutation(),
        overlap_p=_overlap_matrix(t // CMP_STRIDE - 1, t // SLC_BLOCK, t // CMP_STRIDE, LANES).T,
        overlap_s=_overlap_matrix(pos // CMP_STRIDE - 1, n_slc_s, pos // CMP_STRIDE, -(-n_slc_s // LANES) * LANES),
        eg=_gate_expand_matrix(),
        tab_g=tab_g,
        tab_h=tab_rel[NSA_HEADS:],
    )
    caches = (_cache_view(cache_nsa_cmp), _cache_view(cache_nsa_slc), _cache_view(cache_moba),
              jnp.transpose(state_nsa_win, (0, 1, 3, 4, 5, 2)))

    yp, ys = x_prompt, x_sample.reshape(1, db, d)
    new_p, new_s = [], []
    for l in range(depth):
        lw = dict(
            wq=wq[l], wz=wz[l], wkv=wkv[l],
            cmp_wk=_cmp_stage1_weights(phi_w1[l, 0]), cmp_wv=_cmp_stage1_weights(phi_w1[l, 1]),
            pos_flat=phi_pos[l].reshape(2, 1, CMP_LEN * HEAD_DIM),
            phi_w1=phi_w1[l], phi_b1=phi_b1[l].reshape(2, 1, HEAD_DIM),
            cmp_w2t=_cmp_stage2_weights(phi_w2[l]), phi_b2=phi_b2[l].reshape(2, HEAD_DIM, 1),
            w_up_a=w_up_a[l].astype(BF16), w_up_b=w_up_b[l].astype(BF16), w_out=w_out[l].astype(BF16),
            ln_g=ln_g[l].reshape(1, d), ln_b=ln_b[l].reshape(1, d))
        shift, scale, gate = jnp.split(mod[l], 3, axis=-1)
        mod_p = tuple(a[:b, None, :] for a in (shift, scale, gate))
        mod_s = tuple(a[None, b:b + db, :] for a in (shift, scale, gate))
        yp, np_ = _layer_prompt(yp, mod_p, lw, consts)
        ys, ns_ = _layer_sample(ys, mod_s, lw, consts, l, caches, page_table)
        new_p.append(np_)
        new_s.append(ns_)

    def stack_s(key):
        return jnp.stack([it[key] for it in new_s], axis=1)

    win_p = jnp.moveaxis(_kv_output([it["win"] for it in new_p], NSA_KV), 1, 0)
    return (yp, ys.reshape(db, 1, d),
            _kv_output([it["cmp"] for it in new_p], NSA_KV), stack_s("cmp"),
            _kv_output([it["slc"] for it in new_p], NSA_KV), stack_s("slc"),
            _kv_output([it["moba"] for it in new_p], MOBA_HEADS), stack_s("moba"),
            win_p, jnp.moveaxis(stack_s("win"), 1, 0))
```

```python
import functools
import math

import numpy as np
import jax
import jax.numpy as jnp
from jax import lax
from jax.experimental import pallas as pl
from jax.experimental.pallas import tpu as pltpu

F32 = jnp.float32
BF16 = jnp.bfloat16
NEG_INF = float("-inf")
MASK_BIG = 2.0 ** 127
LOG2E = math.log2(math.e)

HEAD_DIM = 64
NSA_HEADS = 8
NSA_KV = 2
NSA_REP = NSA_HEADS // NSA_KV
CMP_LEN = 32
CMP_STRIDE = 16
SLC_BLOCK = 64
SLC_TOPK = 16
WINDOW = 512
MOBA_HEADS = 8
MOBA_BLOCK = 256
MOBA_TOPK = 3
N_HEADS = NSA_HEADS + MOBA_HEADS
N_BUCKETS = 32
MAX_EXACT = N_BUCKETS // 2
MAX_DISTANCE = 128
LN_EPS = 1e-5
PAGE_SIZE = 128

LANES = 128
SUBLANES = 8
TQ = 256
MOBA_HB = 8
CMP_NEAR = 40
H_QA, H_QB, H_KB, H_KS, H_KW = 0, 8, 16, 24, 26
N_QK = 28
PAGES_PER_STEP = 16

NSA_W = NSA_HEADS * HEAD_DIM
MOBA_W = MOBA_HEADS * HEAD_DIM
KV_W = 2 * NSA_KV * HEAD_DIM
MOBA_KV_W = 2 * MOBA_W
Z_AZ, Z_BZ, Z_MA, Z_MB, Z_AG = 0, 512, 1024, 2048, 3072
AG_PAD = 128
Z_W = Z_AG + AG_PAD


def _cparams(n_axes, vmem_mb=None):
    kw = dict(dimension_semantics=("arbitrary",) * n_axes)
    if vmem_mb is not None:
        kw["vmem_limit_bytes"] = vmem_mb * 1024 * 1024
    return pltpu.CompilerParams(**kw)


def _dot(a, b):
    return jnp.dot(a, b, preferred_element_type=F32)


def _dot_nt(a, b):
    return lax.dot_general(a, b, (((1,), (1,)), ((), ())), preferred_element_type=F32)


def _sigmoid(x):
    return 1.0 / (1.0 + jnp.exp(-x))


def _silu(x):
    return x * _sigmoid(x)


def _bucket(dist):
    n = jnp.maximum(dist, 0)
    nf = jnp.maximum(n, 1).astype(F32)
    large = MAX_EXACT + (jnp.log(nf / MAX_EXACT) / math.log(MAX_DISTANCE / MAX_EXACT)
                         * (N_BUCKETS - MAX_EXACT)).astype(jnp.int32)
    return jnp.where(n < MAX_EXACT, n, jnp.minimum(large, N_BUCKETS - 1))


def _bias_rows(bkt, tab):
    bias = jnp.zeros(bkt.shape, F32)
    for b in range(N_BUCKETS):
        bias = jnp.where(bkt == b, tab[:, b:b + 1], bias)
    return bias


def _rank_rows(blocks, n):
    nb = len(blocks)
    cnt = [jnp.zeros(blocks[0].shape, F32) for _ in range(nb)]
    sub = lax.broadcasted_iota(jnp.int32, blocks[0].shape, 0)
    for jp in range(n):
        rb0, r0 = divmod(jp, SUBLANES)
        row = blocks[rb0][r0:r0 + 1, :]
        for rb in range(nb):
            a = blocks[rb]
            if rb < rb0:
                ahead = jnp.where(row > a, 1.0, 0.0)
            elif rb > rb0:
                ahead = jnp.where(row >= a, 1.0, 0.0)
            else:
                ahead = jnp.where(sub > r0, jnp.where(row >= a, 1.0, 0.0), jnp.where(row > a, 1.0, 0.0))
            cnt[rb] = cnt[rb] + ahead
    return cnt


def _topk_drop_rows(score_t, n, k):
    jl, q = score_t.shape
    nb = -(-n // SUBLANES)
    blocks = [score_t[rb * SUBLANES:(rb + 1) * SUBLANES, :] for rb in range(nb)]
    cnt = _rank_rows(blocks, n)
    drop = [jnp.where(c < k, jnp.where(a > NEG_INF, 0.0, -MASK_BIG), -MASK_BIG) for c, a in zip(cnt, blocks)]
    if nb * SUBLANES < jl:
        drop.append(jnp.full((jl - nb * SUBLANES, q), -MASK_BIG, F32))
    return jnp.concatenate(drop, axis=0)


def _softmax_init(m_scr, l_scr, acc_scr):
    m_scr[...] = jnp.full(m_scr.shape, -MASK_BIG, F32)
    l_scr[...] = jnp.zeros(l_scr.shape, F32)
    acc_scr[...] = jnp.zeros(acc_scr.shape, F32)


def _flash_tile(qs, ks, v_ts, states, adds):
    n_s = len(qs)
    ss = [_dot_nt(ks[c], qs[c]) for c in range(n_s)]
    ps, alphas = [], []
    for c in range(n_s):
        s = ss[c]
        for a in adds[c]:
            s = s + a
        m_scr, l_scr, _ = states[c]
        m_prev = m_scr[...]
        m_next = jnp.maximum(m_prev, jnp.max(s, axis=0, keepdims=True))
        alpha = jnp.exp2(m_prev - m_next)
        p = jnp.exp2(s - m_next)
        l_scr[...] = alpha * l_scr[...] + jnp.sum(p, axis=0, keepdims=True)
        m_scr[...] = m_next
        ps.append(p.astype(BF16))
        alphas.append(alpha)
    for c in range(n_s):
        acc_scr = states[c][2]
        acc_scr[...] = acc_scr[...] * alphas[c] + _dot(v_ts[c], ps[c])


def _far_tiles(n, tile):
    def pair(j, c):
        tile(2 * j)
        tile(2 * j + 1)
        return c

    lax.fori_loop(0, n // 2, pair, 0)

    @pl.when(n % 2 == 1)
    def _():
        tile(n - 1)


def _flash_scratch(n_streams):
    per = [pltpu.VMEM((1, TQ), F32), pltpu.VMEM((1, TQ), F32), pltpu.VMEM((HEAD_DIM, TQ), F32)]
    return per * n_streams


def _flash_states(scr):
    return [tuple(scr[3 * c:3 * c + 3]) for c in range(len(scr) // 3)]


def _softmax_finish(l_scr, acc_scr):
    return (acc_scr[...] / jnp.maximum(l_scr[...], 1e-30)).T


def _full_spec(a, n_grid, single=True):
    kw = dict(pipeline_mode=pl.Buffered(1)) if single else {}
    return pl.BlockSpec(a.shape, lambda *_: (0,) * a.ndim, **kw)


def _ada_kernel(c_ref, w_ref, b_ref, o_ref):
    a = _silu(c_ref[...]).astype(BF16)
    o_ref[...] = _dot(a, w_ref[...].astype(BF16)) + b_ref[...]


def _ada(c_all, w_ada, b_ada):
    depth, d, n3 = w_ada.shape
    mc = c_all.shape[0]
    tn = 1024
    return pl.pallas_call(
        _ada_kernel,
        grid=(depth, n3 // tn),
        in_specs=[pl.BlockSpec((mc, d), lambda l, j: (0, 0)),
                  pl.BlockSpec((None, d, tn), lambda l, j: (l, 0, j)),
                  pl.BlockSpec((None, 1, tn), lambda l, j: (l, 0, j))],
        out_specs=pl.BlockSpec((None, mc, tn), lambda l, j: (l, 0, j)),
        out_shape=jax.ShapeDtypeStruct((depth, mc, n3), F32),
        compiler_params=_cparams(2, 40),
        name="ada_mod",
    )(c_all, w_ada, b_ada.reshape(depth, 1, n3))


def _inproj_kernel(x_ref, sc_ref, sh_ref, wq_ref, wz_ref, wkv_ref, *refs):
    q_ref, z_ref, cmp_ref, slc_ref, win_ref, moba_ref = refs[-6:]
    h = (x_ref[...] * (1.0 + sc_ref[...]) + sh_ref[...]).astype(BF16)
    q = _dot(h, wq_ref[...])
    for hd in range(N_QK):
        q_ref[hd] = q[:, hd * HEAD_DIM:(hd + 1) * HEAD_DIM].astype(BF16)
    z_ref[...] = _dot(h, wz_ref[...])
    r = 0
    for o_ref in (cmp_ref, slc_ref, win_ref, moba_ref):
        n = o_ref.shape[0]
        o_ref[...] = _dot_nt(wkv_ref[r:r + n, :], h)
        r += n


def _mod_spec(mod, tm, nt):
    if mod.shape[1] == 1:
        return pl.BlockSpec((None, 1, mod.shape[2]), lambda m: (m // nt, 0, 0))
    return pl.BlockSpec((None, tm, mod.shape[2]), lambda m: (m // nt, m % nt, 0))


def _inproj(x, scale, shift, wq, wz, wkv, layer, depth, kv_prev=None):
    bx, t, d = x.shape
    tm = min(t, 512)
    nt = t // tm

    def kv_spec(rows):
        return pl.BlockSpec((None, None, rows, tm), lambda m: (m // nt, layer, 0, m % nt))

    def kv_shape(rows):
        return jax.ShapeDtypeStruct((bx, depth, rows, t), F32)

    prev = () if kv_prev is None else tuple(kv_prev)
    n_in = 6
    return pl.pallas_call(
        _inproj_kernel,
        grid=(bx * nt,),
        in_specs=[pl.BlockSpec((None, tm, d), lambda m: (m // nt, m % nt, 0)),
                  _mod_spec(scale, tm, nt), _mod_spec(shift, tm, nt),
                  _full_spec(wq, 1), _full_spec(wz, 1), _full_spec(wkv, 1)] + [
                      pl.BlockSpec(memory_space=pl.ANY) for _ in prev],
        out_specs=[pl.BlockSpec((None, N_QK, tm, HEAD_DIM), lambda m: (m // nt, 0, m % nt, 0)),
                   pl.BlockSpec((None, tm, Z_W), lambda m: (m // nt, m % nt, 0)),
                   kv_spec(KV_W), kv_spec(KV_W), kv_spec(KV_W), kv_spec(MOBA_KV_W)],
        out_shape=[jax.ShapeDtypeStruct((bx, N_QK, t, HEAD_DIM), BF16),
                   jax.ShapeDtypeStruct((bx, t, Z_W), F32),
                   kv_shape(KV_W), kv_shape(KV_W), kv_shape(KV_W), kv_shape(MOBA_KV_W)],
        input_output_aliases={n_in + k: 2 + k for k in range(len(prev))},
        compiler_params=_cparams(1, 56),
        name="in_proj",
    )(x, scale, shift, wq, wz, wkv, *prev)


def _cmp_proj_kernel(*refs, n_x, n_prefetch=0):
    refs = refs[n_prefetch:]
    perm_ref, wk_ref, wv_ref, abk_ref, abv_ref = refs[n_x:]
    perm = perm_ref[...]
    chunks = [x_ref[:, c * LANES:(c + 1) * LANES].astype(BF16)
              for x_ref in refs[:n_x] for c in range(x_ref.shape[1] // LANES)]
    groups = LANES // CMP_STRIDE
    rows_k, rows_v = [], []
    for c0 in range(0, len(chunks), 2):
        pair = chunks[c0:c0 + 2]
        xp = _dot_nt(perm[:len(pair) * LANES, :len(pair) * LANES], jnp.concatenate(pair, axis=1))
        for ci in range(len(pair)):
            parts = [xp[ci * LANES + l * groups:ci * LANES + (l + 1) * groups] for l in range(CMP_STRIDE)]
            rows_k.append(jnp.concatenate([p[:, :KV_W // 2] for p in parts], axis=1))
            rows_v.append(jnp.concatenate([p[:, KV_W // 2:] for p in parts], axis=1))
    for rows, w_ref, ab_ref in ((rows_k, wk_ref, abk_ref), (rows_v, wv_ref, abv_ref)):
        xr = jnp.concatenate(rows, axis=0).astype(BF16)
        ab_ref[...] = _dot(xr, w_ref[...])


def _cmp_proj_prompt(cmp_t, layer, perm, wk, wv):
    bx, _, _, t = cmp_t.shape
    tc = min(t, PAGES_PER_STEP * LANES)
    m = tc // CMP_STRIDE
    ospec = pl.BlockSpec((None, m, 256), lambda b, i: (b, i, 0))
    oshape = jax.ShapeDtypeStruct((bx, t // CMP_STRIDE, 256), F32)
    return pl.pallas_call(
        functools.partial(_cmp_proj_kernel, n_x=1),
        grid=(bx, t // tc),
        in_specs=[pl.BlockSpec((None, None, KV_W, tc), lambda b, i: (b, layer, 0, i)),
                  _full_spec(perm, 2), _full_spec(wk, 2), _full_spec(wv, 2)],
        out_specs=[ospec, ospec],
        out_shape=[oshape, oshape],
        compiler_params=_cparams(2),
        name="cmp_proj_prompt",
    )(cmp_t, perm, wk, wv)


def _cmp_proj_paged(cache_t, page_table, layer, perm, wk, wv):
    db, n_pages = page_table.shape
    pps = PAGES_PER_STEP
    m = PAGE_SIZE // CMP_STRIDE

    def page_spec(k):
        return pl.BlockSpec((None, None, KV_W, PAGE_SIZE),
                            lambda b, i, pt: (pt[b * n_pages + i * pps + k], layer, 0, 0))

    ospec = pl.BlockSpec((None, pps * m, 256), lambda b, i, pt: (b, i, 0))
    oshape = jax.ShapeDtypeStruct((db, n_pages * m, 256), F32)
    return pl.pallas_call(
        functools.partial(_cmp_proj_kernel, n_x=pps, n_prefetch=1),
        grid_spec=pltpu.PrefetchScalarGridSpec(
            num_scalar_prefetch=1,
            grid=(db, n_pages // pps),
            in_specs=[page_spec(k) for k in range(pps)] + [
                _full_spec(perm, 3), _full_spec(wk, 3), _full_spec(wv, 3)],
            out_specs=[ospec, ospec]),
        out_shape=[oshape, oshape],
        compiler_params=_cparams(2),
        name="cmp_proj_paged",
    )(page_table.reshape(-1), *([cache_t] * pps), perm, wk, wv)


def _cmp_mlp_kernel(abk_ref, abv_ref, pos_ref, w1_ref, b1_ref, w2_ref, b2_ref, ck_ref, cv_ref, ckr_ref):
    m = abk_ref.shape[0]
    col = lax.broadcasted_iota(jnp.int32, (HEAD_DIM, m), 1)
    for kv, (ab_ref, o_ref) in enumerate(((abk_ref, ck_ref), (abv_ref, cv_ref))):
        pos = jnp.broadcast_to(pos_ref[kv], (SUBLANES, CMP_LEN * HEAD_DIM)).astype(BF16)
        c0 = _dot(pos, w1_ref[kv].astype(BF16))[0:1, :] + b1_ref[kv]
        c0 = jnp.concatenate([c0] * NSA_KV, axis=1)
        ab = ab_ref[...]
        nxt = pltpu.roll(ab[:, LANES:], m - 1, 0)
        h = jax.nn.gelu(ab[:, :LANES] + nxt + c0).astype(BF16)
        for g in range(NSA_KV):
            y_t = jnp.where(col < m - 1, _dot_nt(w2_ref[kv, g], h) + b2_ref[kv], 0.0)
            o_ref[g] = y_t.astype(BF16)
            if kv == 0:
                ckr_ref[g] = y_t.T.astype(BF16)


def _cmp_mlp(abk, abv, pos_flat, w1, b1, w2t, b2col):
    bx, m, _ = abk.shape
    abspec = pl.BlockSpec((None, m, 256), lambda b: (b, 0, 0))
    ospec = pl.BlockSpec((None, NSA_KV, HEAD_DIM, m), lambda b: (b, 0, 0, 0))
    oshape = jax.ShapeDtypeStruct((bx, NSA_KV, HEAD_DIM, m), BF16)
    return pl.pallas_call(
        _cmp_mlp_kernel,
        grid=(bx,),
        in_specs=[abspec, abspec, _full_spec(pos_flat, 1), _full_spec(w1, 1), _full_spec(b1, 1),
                  _full_spec(w2t, 1), _full_spec(b2col, 1)],
        out_specs=[ospec, ospec, pl.BlockSpec((None, NSA_KV, m, HEAD_DIM), lambda b: (b, 0, 0, 0))],
        out_shape=[oshape, oshape, jax.ShapeDtypeStruct((bx, NSA_KV, m, HEAD_DIM), BF16)],
        compiler_params=_cparams(1),
        name="cmp_mlp",
    )(abk, abv, pos_flat, w1, b1, w2t, b2col)


def _bias_tiles_kernel(tab_ref, o_ref):
    h = pl.program_id(0)
    key = lax.broadcasted_iota(jnp.int32, (TQ, TQ), 0)
    qry = lax.broadcasted_iota(jnp.int32, (TQ, TQ), 1)
    far = tab_ref[N_BUCKETS - 1, h]
    for kind in range(2):
        dist = qry - key + kind * TQ
        bkt = _bucket(dist)
        bias = jnp.zeros((TQ, TQ), F32)
        for b in range(N_BUCKETS - 1):
            bias = jnp.where(bkt == b, tab_ref[b, h] - far, bias)
        o_ref[kind] = jnp.where(dist >= 0, bias, NEG_INF)


def _bias_tiles(rel_bias):
    nh = rel_bias.shape[1]
    return pl.pallas_call(
        _bias_tiles_kernel,
        grid=(nh,),
        in_specs=[pl.BlockSpec(memory_space=pltpu.SMEM)],
        out_specs=pl.BlockSpec((2, None, TQ, TQ), lambda h: (0, h, 0, 0)),
        out_shape=jax.ShapeDtypeStruct((2, nh, TQ, TQ), F32),
        compiler_params=_cparams(1),
        name="bias_tiles",
    )(rel_bias)


def _nsa_cmp_kernel(tab_ref, q_ref, ck_ref, cv_ref, ov_ref, oc_ref, dn_ref, *s_scr, n_slc, k_sel):
    i = pl.program_id(1)
    ncp = ck_ref.shape[1]
    jl = ov_ref.shape[0]
    q0 = i * TQ
    qpos = q0 + lax.broadcasted_iota(jnp.int32, (ncp, TQ), 1)
    cend = lax.broadcasted_iota(jnp.int32, (ncp, TQ), 0) * CMP_STRIDE + (CMP_LEN - 1)
    valid = qpos >= cend
    n0 = jnp.maximum(q0 - (MAX_DISTANCE + CMP_LEN - 1), 0) // CMP_STRIDE
    n0 = pl.multiple_of(jnp.minimum(n0 // SUBLANES * SUBLANES, ncp - CMP_NEAR), SUBLANES)
    near_end = (n0 + lax.broadcasted_iota(jnp.int32, (CMP_NEAR, TQ), 0)) * CMP_STRIDE + (CMP_LEN - 1)
    bkt = _bucket(q0 + lax.broadcasted_iota(jnp.int32, (CMP_NEAR, TQ), 1) - near_end)
    j = lax.broadcasted_iota(jnp.int32, (jl, TQ), 0)
    cur = (q0 + lax.broadcasted_iota(jnp.int32, (jl, TQ), 1)) // SLC_BLOCK
    forced = (j == 0) | (j == cur) | (j == cur - 1)
    outs = []
    for g in range(NSA_KV):
        heads = range(g * NSA_REP, (g + 1) * NSA_REP)
        bias = [jnp.zeros((CMP_NEAR, TQ), F32) for _ in heads]
        for b in range(N_BUCKETS - 1):
            hit = bkt == b
            bias = [jnp.where(hit, tab_ref[b, h] - tab_ref[N_BUCKETS - 1, h], bb) for h, bb in zip(heads, bias)]
        ck = ck_ref[g]
        cv = cv_ref[g]
        imp = jnp.zeros((jl, TQ), F32)
        for r, h in enumerate(heads):
            scr = s_scr[r]
            scr[...] = _dot_nt(ck, q_ref[h])
            scr[pl.ds(n0, CMP_NEAR), :] = scr[pl.ds(n0, CMP_NEAR), :] + bias[r]
            s = jnp.where(valid, scr[...], NEG_INF)
            m = jnp.max(s, axis=0, keepdims=True)
            m = jnp.where(m == NEG_INF, 0.0, m)
            e = jnp.exp2(s - m)
            p = (e / jnp.maximum(jnp.sum(e, axis=0, keepdims=True), 1e-30)).astype(BF16)
            outs.append(_dot(cv, p).T)
            imp = imp + _dot(ov_ref[...], p)
        imp = jnp.where(forced, jnp.inf, imp)
        imp = jnp.where(j <= cur, imp, NEG_INF)
        dn_ref[g] = _topk_drop_rows(imp, n_slc, k_sel)
    oc_ref[...] = jnp.concatenate(outs, axis=1)


def _nsa_cmp_prompt(rel_bias, qk, ck, cv_t, overlap_t, n_slc, k_sel):
    b, _, t, _ = qk.shape
    ncp = ck.shape[2]
    jl = overlap_t.shape[0]
    return pl.pallas_call(
        functools.partial(_nsa_cmp_kernel, n_slc=n_slc, k_sel=k_sel),
        grid=(b, t // TQ),
        in_specs=[pl.BlockSpec(memory_space=pltpu.SMEM),
                  pl.BlockSpec((None, NSA_HEADS, TQ, HEAD_DIM), lambda bb, i: (bb, H_QA // NSA_HEADS, i, 0)),
                  pl.BlockSpec((None, NSA_KV, ncp, HEAD_DIM), lambda bb, i: (bb, 0, 0, 0)),
                  pl.BlockSpec((None, NSA_KV, HEAD_DIM, ncp), lambda bb, i: (bb, 0, 0, 0)),
                  _full_spec(overlap_t, 2)],
        out_specs=[pl.BlockSpec((None, TQ, NSA_W), lambda bb, i: (bb, i, 0)),
                   pl.BlockSpec((None, NSA_KV, jl, TQ), lambda bb, i: (bb, 0, 0, i))],
        out_shape=[jax.ShapeDtypeStruct((b, t, NSA_W), F32),
                   jax.ShapeDtypeStruct((b, NSA_KV, jl, t), F32)],
        scratch_shapes=[pltpu.VMEM((ncp, TQ), F32)] * NSA_REP,
        compiler_params=_cparams(2, 40),
        name="nsa_cmp_prompt",
    )(rel_bias, qk, ck, cv_t, overlap_t)


def _k_tile(ref, kt):
    return ref[pl.ds(pl.multiple_of(kt * TQ, TQ), TQ), :]


def _v_tile(ref, kt):
    return ref[:, pl.ds(pl.multiple_of(kt * TQ, TQ), TQ)].astype(BF16)


def _nsa_slc_kernel(q_ref, k0_ref, k1_ref, v0_ref, v1_ref, dn_ref, bt_ref, o_ref, *scr):
    i = pl.program_id(1)
    states = _flash_states(scr)
    qs = [q_ref[h] for h in range(NSA_HEADS)]
    per_tile = TQ // SLC_BLOCK
    for st in states:
        _softmax_init(*st)

    def tile(kt, kind):
        ks, v_ts, adds = [], [], []
        for g, (k_ref, v_ref) in enumerate(((k0_ref, v0_ref), (k1_ref, v1_ref))):
            mask = jnp.concatenate(
                [jnp.broadcast_to(dn_ref[g, pl.ds(kt * per_tile + jb, 1), :], (SLC_BLOCK, TQ))
                 for jb in range(per_tile)], axis=0)
            ks += [_k_tile(k_ref, kt)] * NSA_REP
            v_ts += [_v_tile(v_ref, kt)] * NSA_REP
            adds += [[mask] if kind is None else [mask, bt_ref[kind, g * NSA_REP + r]] for r in range(NSA_REP)]
        _flash_tile(qs, ks, v_ts, states, adds)

    _far_tiles(jnp.maximum(i - 1, 0), lambda kt: tile(kt, None))

    @pl.when(i >= 1)
    def _():
        tile(i - 1, 1)

    tile(i, 0)
    o_ref[...] = jnp.concatenate([_softmax_finish(st[1], st[2]) for st in states], axis=1)


def _nsa_slc_prompt(qk, kv_t, layer, dn, btiles):
    b, _, t, _ = qk.shape
    jl = dn.shape[2]
    return pl.pallas_call(
        _nsa_slc_kernel,
        grid=(b, t // TQ),
        in_specs=_nsa_dense_specs(t, H_KS, layer) + [
            pl.BlockSpec((None, NSA_KV, jl, TQ), lambda bb, i: (bb, 0, 0, i)),
            pl.BlockSpec((2, NSA_HEADS, TQ, TQ), lambda bb, i: (0, 0, 0, 0))],
        out_specs=pl.BlockSpec((None, TQ, NSA_W), lambda bb, i: (bb, i, 0)),
        out_shape=jax.ShapeDtypeStruct((b, t, NSA_W), F32),
        scratch_shapes=_flash_scratch(NSA_HEADS),
        compiler_params=_cparams(2, 40),
        name="nsa_slc_prompt",
    )(qk, qk, qk, kv_t, kv_t, dn, btiles)


def _nsa_dense_specs(t, k_head, layer):
    return [pl.BlockSpec((None, NSA_HEADS, TQ, HEAD_DIM), lambda bb, i: (bb, H_QA // NSA_HEADS, i, 0)),
            pl.BlockSpec((None, None, t, HEAD_DIM), lambda bb, i: (bb, k_head, 0, 0)),
            pl.BlockSpec((None, None, t, HEAD_DIM), lambda bb, i: (bb, k_head + 1, 0, 0)),
            pl.BlockSpec((None, None, HEAD_DIM, t), lambda bb, i: (bb, layer, NSA_KV, 0)),
            pl.BlockSpec((None, None, HEAD_DIM, t), lambda bb, i: (bb, layer, NSA_KV + 1, 0))]


def _nsa_win_kernel(q_ref, k0_ref, k1_ref, v0_ref, v1_ref, bt_ref, o_ref, *scr):
    i = pl.program_id(1)
    states = _flash_states(scr)
    qs = [q_ref[h] for h in range(NSA_HEADS)]
    for st in states:
        _softmax_init(*st)

    def tile(kt, adds):
        ks = [_k_tile(k0_ref, kt)] * NSA_REP + [_k_tile(k1_ref, kt)] * NSA_REP
        v_ts = [_v_tile(v0_ref, kt)] * NSA_REP + [_v_tile(v1_ref, kt)] * NSA_REP
        _flash_tile(qs, ks, v_ts, states, adds)

    @pl.when(i >= WINDOW // TQ)
    def _():
        key = lax.broadcasted_iota(jnp.int32, (TQ, TQ), 0)
        qry = lax.broadcasted_iota(jnp.int32, (TQ, TQ), 1)
        tile(i - WINDOW // TQ, [[jnp.where(key > qry, 0.0, NEG_INF)]] * NSA_HEADS)

    @pl.when(i >= 1)
    def _():
        tile(i - 1, [[bt_ref[1, h]] for h in range(NSA_HEADS)])

    tile(i, [[bt_ref[0, h]] for h in range(NSA_HEADS)])
    o_ref[...] = jnp.concatenate([_softmax_finish(st[1], st[2]) for st in states], axis=1)


def _nsa_win_prompt(qk, kv_t, layer, btiles):
    b, _, t, _ = qk.shape
    return pl.pallas_call(
        _nsa_win_kernel,
        grid=(b, t // TQ),
        in_specs=_nsa_dense_specs(t, H_KW, layer) + [
            pl.BlockSpec((2, NSA_HEADS, TQ, TQ), lambda bb, i: (0, 0, 0, 0))],
        out_specs=pl.BlockSpec((None, TQ, NSA_W), lambda bb, i: (bb, i, 0)),
        out_shape=jax.ShapeDtypeStruct((b, t, NSA_W), F32),
        scratch_shapes=_flash_scratch(NSA_HEADS),
        compiler_params=_cparams(2, 40),
        name="nsa_win_prompt",
    )(qk, qk, qk, kv_t, kv_t, btiles)


def _kmean_kernel(k_ref, o_ref, *, nblk):
    lane = lax.broadcasted_iota(jnp.int32, (MOBA_W, LANES), 1)
    acc = jnp.zeros((MOBA_W, LANES), F32)
    for blk in range(nblk):
        mean = jnp.sum(k_ref[:, blk * MOBA_BLOCK:(blk + 1) * MOBA_BLOCK], axis=1, keepdims=True) / MOBA_BLOCK
        acc = jnp.where(lane == blk, mean, acc)
    for h in range(MOBA_HEADS):
        o_ref[h] = acc[h * HEAD_DIM:(h + 1) * HEAD_DIM, :].T.astype(BF16)


def _kmean_prompt(moba_t, layer):
    bx, _, _, t = moba_t.shape
    return pl.pallas_call(
        functools.partial(_kmean_kernel, nblk=t // MOBA_BLOCK),
        grid=(bx,),
        in_specs=[pl.BlockSpec((None, None, MOBA_W, t), lambda b: (b, layer, 0, 0))],
        out_specs=pl.BlockSpec((None, MOBA_HEADS, LANES, HEAD_DIM), lambda b: (b, 0, 0, 0)),
        out_shape=jax.ShapeDtypeStruct((bx, MOBA_HEADS, LANES, HEAD_DIM), BF16),
        compiler_params=_cparams(1, 40),
        name="moba_kmean_prompt",
    )(moba_t)


def _moba_kernel(q_ref, k_ref, v_ref, km_ref, bt_ref, o_ref, *scr, nblk, k_m):
    i = pl.program_id(2)
    hb = q_ref.shape[0]
    states = _flash_states(scr[:3 * hb])
    dn_scr = scr[3 * hb:]
    qs = [q_ref[hh] for hh in range(hb)]
    for hh in range(hb):
        gs = _dot_nt(km_ref[hh], qs[hh])
        blk = lax.broadcasted_iota(jnp.int32, gs.shape, 0)
        dn_scr[hh][...] = _topk_drop_rows(jnp.where(blk < i, gs, NEG_INF), nblk, k_m)
        _softmax_init(*states[hh])

    def tile(kt, bias_kind, masked):
        start = pl.multiple_of(kt * TQ, TQ)
        ks = [k_ref[hh, pl.ds(start, TQ), :] for hh in range(hb)]
        v_ts = [v_ref[hh * HEAD_DIM:(hh + 1) * HEAD_DIM, pl.ds(start, TQ)].astype(BF16) for hh in range(hb)]
        adds = [[] for _ in range(hb)]
        if masked:
            adds = [[dn_scr[hh][pl.ds(kt, 1), :]] for hh in range(hb)]
        if bias_kind is not None:
            adds = [a + [bt_ref[bias_kind, hh]] for hh, a in enumerate(adds)]
        _flash_tile(qs, ks, v_ts, states, adds)

    _far_tiles(jnp.maximum(i - 1, 0), lambda kt: tile(kt, None, True))

    @pl.when(i >= 1)
    def _():
        tile(i - 1, 1, True)

    tile(i, 0, False)
    o_ref[...] = jnp.concatenate([_softmax_finish(st[1], st[2]) for st in states], axis=1)


def _moba_prompt(qk, moba_t, layer, kmean, btiles, k_m):
    b, _, t, _ = qk.shape
    hb = MOBA_HB
    nq = t // TQ
    vb = MOBA_W // (hb * HEAD_DIM)
    once = dict(pipeline_mode=pl.Buffered(1))
    return pl.pallas_call(
        functools.partial(_moba_kernel, nblk=t // MOBA_BLOCK, k_m=k_m),
        grid=(b, MOBA_HEADS // hb, nq),
        in_specs=[pl.BlockSpec((None, hb, TQ, HEAD_DIM), lambda bb, hp, i: (bb, H_QB // hb + hp, i, 0)),
                  pl.BlockSpec((None, hb, t, HEAD_DIM), lambda bb, hp, i: (bb, H_KB // hb + hp, 0, 0), **once),
                  pl.BlockSpec((None, None, hb * HEAD_DIM, t), lambda bb, hp, i: (bb, layer, vb + hp, 0), **once),
                  pl.BlockSpec((None, hb, LANES, HEAD_DIM), lambda bb, hp, i: (bb, hp, 0, 0)),
                  pl.BlockSpec((2, hb, TQ, TQ), lambda bb, hp, i: (0, NSA_HEADS // hb + hp, 0, 0), **once)],
        out_specs=pl.BlockSpec((None, TQ, hb * HEAD_DIM), lambda bb, hp, i: (bb, i, hp)),
        out_shape=jax.ShapeDtypeStruct((b, t, MOBA_W), F32),
        scratch_shapes=_flash_scratch(hb) + [pltpu.VMEM((LANES, TQ), F32)] * hb,
        compiler_params=_cparams(3, 48),
        name="moba_prompt",
    )(qk, qk, moba_t, kmean, btiles)


def _outproj_kernel(x_ref, gate_ref, az_ref, bz_ref, ma_ref, mb_ref, ag_ref, oc_ref, os_ref, ow_ref, ob_ref,
                    eg_ref, wua_ref, wub_ref, wo_ref, lng_ref, lnb_ref, y_ref, *, alpha):
    w = NSA_W
    g = _sigmoid(ag_ref[...])
    g_hi = g.astype(BF16)
    g_lo = (g - g_hi.astype(F32)).astype(BF16)
    ge = _dot(g_hi, eg_ref[...]) + _dot(g_lo, eg_ref[...])
    o_a = ge[:, 0:w] * oc_ref[...] + ge[:, w:2 * w] * os_ref[...] + ge[:, 2 * w:3 * w] * ow_ref[...]
    y_a = _dot((o_a * _silu(az_ref[...])).astype(BF16), wua_ref[...])
    y_b = _dot((ob_ref[...] * _silu(bz_ref[...])).astype(BF16), wub_ref[...])
    mixed = _dot((_sigmoid(ma_ref[...]) * y_a + _sigmoid(mb_ref[...]) * y_b).astype(BF16), wo_ref[...])
    z = alpha * x_ref[...] + gate_ref[...] * mixed
    mu = jnp.mean(z, axis=-1, keepdims=True)
    var = jnp.mean(jnp.square(z - mu), axis=-1, keepdims=True)
    y_ref[...] = (z - mu) * lax.rsqrt(var + LN_EPS) * lng_ref[...] + lnb_ref[...]


def _outproj(x, gate, z, o_c, o_s, o_w, o_b, eg, wua, wub, wo, ln_g, ln_b, alpha):
    bx, t, d = x.shape
    tm = min(t, 256)
    nt = t // tm

    def tok(width, col):
        return pl.BlockSpec((None, tm, width), lambda m: (m // nt, m % nt, col))

    return pl.pallas_call(
        functools.partial(_outproj_kernel, alpha=alpha),
        grid=(bx * nt,),
        in_specs=[tok(d, 0), _mod_spec(gate, tm, nt),
                  tok(NSA_W, Z_AZ // NSA_W), tok(MOBA_W, Z_BZ // MOBA_W), tok(d, Z_MA // d), tok(d, Z_MB // d),
                  tok(AG_PAD, Z_AG // AG_PAD), tok(NSA_W, 0), tok(NSA_W, 0), tok(NSA_W, 0), tok(MOBA_W, 0),
                  _full_spec(eg, 1), _full_spec(wua, 1), _full_spec(wub, 1), _full_spec(wo, 1),
                  _full_spec(ln_g, 1), _full_spec(ln_b, 1)],
        out_specs=tok(d, 0),
        out_shape=jax.ShapeDtypeStruct((bx, t, d), F32),
        compiler_params=_cparams(1, 48),
        name="out_proj",
    )(x, gate, z, z, z, z, z, o_c, o_s, o_w, o_b, eg, wua, wub, wo, ln_g, ln_b)


def _dec_attend(s, v_t, s_self, v_self):
    m = jnp.maximum(jnp.max(s, axis=1, keepdims=True), s_self)
    e = jnp.exp2(s - m)
    e_self = jnp.exp2(s_self - m)
    den = jnp.maximum(jnp.sum(e, axis=1, keepdims=True) + e_self, 1e-30)
    return (_dot_nt(e.astype(BF16), v_t) + e_self * v_self) / den


def _dec_cmp_kernel(tab_ref, q_ref, ck_ref, cv_ref, ov_ref, oc_ref, imp_ref, *, pos):
    ncp = ck_ref.shape[2]
    jl = ov_ref.shape[1]
    cend = lax.broadcasted_iota(jnp.int32, (SUBLANES, ncp), 1) * CMP_STRIDE + (CMP_LEN - 1)
    dist = pos - cend
    valid = dist >= 0
    bkt = _bucket(dist)
    j = lax.broadcasted_iota(jnp.int32, (SUBLANES, jl), 1)
    cur = pos // SLC_BLOCK
    for g in range(NSA_KV):
        s = jnp.where(valid, _dot(q_ref[g], ck_ref[g]) + _bias_rows(bkt, tab_ref[g]), NEG_INF)
        m = jnp.max(s, axis=1, keepdims=True)
        m = jnp.where(m == NEG_INF, 0.0, m)
        e = jnp.exp2(s - m)
        p = (e / jnp.maximum(jnp.sum(e, axis=1, keepdims=True), 1e-30)).astype(BF16)
        oc_ref[g] = _dot_nt(p, cv_ref[g])
        imp4 = _dot(p, ov_ref[...])
        imp = imp4[0:1]
        for r in range(1, NSA_REP):
            imp = imp + imp4[r:r + 1]
        imp = jnp.broadcast_to(imp, (SUBLANES, jl))
        imp = jnp.where((j == 0) | (j == cur) | (j == cur - 1), jnp.inf, imp)
        imp_ref[g] = jnp.where(j <= cur, imp, NEG_INF)


def _dec_cmp(tab_g, q8, ck_t, cv_t, overlap, pos):
    db = q8.shape[0]
    ncp = ck_t.shape[3]
    jl = overlap.shape[1]
    cspec = pl.BlockSpec((None, NSA_KV, HEAD_DIM, ncp), lambda b: (b, 0, 0, 0))
    return pl.pallas_call(
        functools.partial(_dec_cmp_kernel, pos=pos),
        grid=(db,),
        in_specs=[_full_spec(tab_g, 1),
                  pl.BlockSpec((None, NSA_KV, SUBLANES, HEAD_DIM), lambda b: (b, 0, 0, 0)),
                  cspec, cspec, _full_spec(overlap, 1)],
        out_specs=[pl.BlockSpec((None, NSA_KV, SUBLANES, HEAD_DIM), lambda b: (b, 0, 0, 0)),
                   pl.BlockSpec((None, NSA_KV, SUBLANES, jl), lambda b: (b, 0, 0, 0))],
        out_shape=[jax.ShapeDtypeStruct((db, NSA_KV, SUBLANES, HEAD_DIM), F32),
                   jax.ShapeDtypeStruct((db, NSA_KV, SUBLANES, jl), F32)],
        compiler_params=_cparams(1),
        name="dec_nsa_cmp",
    )(tab_g, q8, ck_t, cv_t, overlap)


def _topk_idx_kernel(s_ref, idx_ref, *, n, k):
    st = s_ref[...].T
    nb = -(-n // SUBLANES)
    blocks = [st[rb * SUBLANES:(rb + 1) * SUBLANES, :] for rb in range(nb)]
    cnt = _rank_rows(blocks, n)
    sub = lax.broadcasted_iota(jnp.int32, blocks[0].shape, 0)
    rows = []
    for r in range(idx_ref.shape[0]):
        if r >= k:
            rows.append(jnp.full((1, st.shape[1]), -1, jnp.int32))
            continue
        acc = jnp.zeros(blocks[0].shape, jnp.int32)
        for rb in range(nb):
            hit = jnp.where(cnt[rb] == float(r), jnp.where(blocks[rb] > NEG_INF, 1, 0), 0)
            acc = acc + hit * (sub + (rb * SUBLANES + 1))
        rows.append(jnp.sum(acc, axis=0, keepdims=True) - 1)
    idx_ref[...] = jnp.concatenate(rows, axis=0)


def _topk_idx(scores, n, k):
    nrow, jl = scores.shape
    kp = -(-k // SUBLANES) * SUBLANES
    return pl.pallas_call(
        functools.partial(_topk_idx_kernel, n=n, k=k),
        grid=(1,),
        in_specs=[pl.BlockSpec((nrow, jl), lambda i: (0, 0))],
        out_specs=pl.BlockSpec((kp, nrow), lambda i: (0, 0)),
        out_shape=jax.ShapeDtypeStruct((kp, nrow), jnp.int32),
        compiler_params=_cparams(1),
        name="topk_idx",
    )(scores)


def _dec_slc_kernel(sel_ref, pt_ref, *refs, pos, n_slc, k_sel):
    tab_ref, q_ref, kvn_ref, o_ref = refs[NSA_KV * 2 * k_sel:]
    b = pl.program_id(0)
    n_keys = k_sel * PAGE_SIZE
    lane = lax.broadcasted_iota(jnp.int32, (SUBLANES, n_keys), 1)
    half = PAGE_SIZE // SLC_BLOCK
    for g in range(NSA_KV):
        k_refs = refs[g * 2 * k_sel:g * 2 * k_sel + k_sel]
        v_refs = refs[g * 2 * k_sel + k_sel:(g + 1) * 2 * k_sel]
        kpos = jnp.zeros((SUBLANES, n_keys), jnp.int32)
        for k in range(k_sel):
            jk = sel_ref[(b * NSA_KV + g) * k_sel + k]
            ok = (jk >= 0) & (jk < n_slc - 1)
            in_blk = (lane % PAGE_SIZE) // SLC_BLOCK == jk % half
            here = jnp.where(in_blk, (jk // half) * PAGE_SIZE + lane % PAGE_SIZE, pos + 1)
            kpos = jnp.where(lane // PAGE_SIZE == k, jnp.where(ok, here, pos + 1), kpos)
        dist = pos - kpos
        tab = tab_ref[g]
        q = q_ref[g]
        k_t = jnp.concatenate([r[...] for r in k_refs], axis=1).astype(BF16)
        v_t = jnp.concatenate([r[...] for r in v_refs], axis=1).astype(BF16)
        s = jnp.where(dist >= 0, _dot(q, k_t) + _bias_rows(_bucket(dist), tab), NEG_INF)
        s_self = jnp.sum(q.astype(F32) * kvn_ref[g:g + 1, :], axis=1, keepdims=True) + tab[:, 0:1]
        o_ref[g] = _dec_attend(s, v_t, s_self, kvn_ref[NSA_KV + g:NSA_KV + g + 1, :])


def _dec_slc(sel_flat, pt_flat, cache_t, layer, tab_g, q8, kv_new, pos, n_slc, k_sel, n_pages):
    db = q8.shape[0]
    half = PAGE_SIZE // SLC_BLOCK

    def blk_spec(g, k, kv):
        def imap(b, sel, pt):
            j = jnp.clip(sel[(b * NSA_KV + g) * k_sel + k], 0, n_slc - 2)
            return (pt[b * n_pages + j // half], layer, kv, g, 0, 0)
        return pl.BlockSpec((None, None, None, None, HEAD_DIM, PAGE_SIZE), imap)

    qspec = pl.BlockSpec((None, NSA_KV, SUBLANES, HEAD_DIM), lambda b, sel, pt: (b, 0, 0, 0))
    return pl.pallas_call(
        functools.partial(_dec_slc_kernel, pos=pos, n_slc=n_slc, k_sel=k_sel),
        grid_spec=pltpu.PrefetchScalarGridSpec(
            num_scalar_prefetch=2,
            grid=(db,),
            in_specs=[blk_spec(g, k, kv) for g in range(NSA_KV) for kv in range(2) for k in range(k_sel)] + [
                _full_spec(tab_g, 3), qspec,
                pl.BlockSpec((None, 2 * NSA_KV, HEAD_DIM), lambda b, sel, pt: (b, 0, 0))],
            out_specs=qspec),
        out_shape=jax.ShapeDtypeStruct((db, NSA_KV, SUBLANES, HEAD_DIM), F32),
        compiler_params=_cparams(1),
        name="dec_nsa_slc",
    )(sel_flat, pt_flat, *([cache_t] * (NSA_KV * 2 * k_sel)), tab_g, q8, kv_new)


def _dec_win_kernel(k0_ref, k1_ref, v0_ref, v1_ref, tab_ref, q_ref, kvn_ref, o_ref):
    n = k0_ref.shape[1]
    dist = n - lax.broadcasted_iota(jnp.int32, (SUBLANES, n), 1)
    bkt = _bucket(dist)
    for g, (k_ref, v_ref) in enumerate(((k0_ref, v0_ref), (k1_ref, v1_ref))):
        tab = tab_ref[g]
        q = q_ref[g]
        s = jnp.where(dist < WINDOW, _dot(q, k_ref[...].astype(BF16)) + _bias_rows(bkt, tab), NEG_INF)
        s_self = jnp.sum(q.astype(F32) * kvn_ref[g:g + 1, :], axis=1, keepdims=True) + tab[:, 0:1]
        o_ref[g] = _dec_attend(s, v_ref[...].astype(BF16), s_self, kvn_ref[NSA_KV + g:NSA_KV + g + 1, :])


def _dec_win(state_t, layer, tab_g, q8, kv_new):
    db = q8.shape[0]
    n = state_t.shape[5]

    def st_spec(kv, g):
        return pl.BlockSpec((None, None, None, None, HEAD_DIM, n), lambda b: (layer, b, kv, g, 0, 0))

    return pl.pallas_call(
        _dec_win_kernel,
        grid=(db,),
        in_specs=[st_spec(0, 0), st_spec(0, 1), st_spec(1, 0), st_spec(1, 1),
                  _full_spec(tab_g, 1),
                  pl.BlockSpec((None, NSA_KV, SUBLANES, HEAD_DIM), lambda b: (b, 0, 0, 0)),
                  pl.BlockSpec((None, 2 * NSA_KV, HEAD_DIM), lambda b: (b, 0, 0))],
        out_specs=pl.BlockSpec((None, NSA_KV, SUBLANES, HEAD_DIM), lambda b: (b, 0, 0, 0)),
        out_shape=jax.ShapeDtypeStruct((db, NSA_KV, SUBLANES, HEAD_DIM), F32),
        compiler_params=_cparams(1),
        name="dec_nsa_win",
    )(state_t, state_t, state_t, state_t, tab_g, q8, kv_new)


def _dec_moba_sweep_kernel(pt_ref, *refs, n_x):
    k_refs = refs[:n_x]
    qb_ref, s_ref = refs[n_x:]
    qb = qb_ref[...]
    for k, k_ref in enumerate(k_refs):
        prod = k_ref[...] * qb
        s_ref[:, k * PAGE_SIZE:(k + 1) * PAGE_SIZE] = jnp.concatenate(
            [jnp.sum(prod[h * HEAD_DIM:(h + 1) * HEAD_DIM], axis=0, keepdims=True) for h in range(MOBA_HEADS)],
            axis=0)


def _dec_moba_sweep(cache_t, page_table, layer, q_lanes):
    db, n_pages = page_table.shape
    pps = PAGES_PER_STEP

    def page_spec(k):
        return pl.BlockSpec((None, None, None, MOBA_W, PAGE_SIZE),
                            lambda b, i, pt: (pt[b * n_pages + i * pps + k], layer, 0, 0, 0))

    return pl.pallas_call(
        functools.partial(_dec_moba_sweep_kernel, n_x=pps),
        grid_spec=pltpu.PrefetchScalarGridSpec(
            num_scalar_prefetch=1,
            grid=(db, n_pages // pps),
            in_specs=[page_spec(k) for k in range(pps)] + [
                pl.BlockSpec((None, MOBA_W, PAGE_SIZE), lambda b, i, pt: (b, 0, 0))],
            out_specs=pl.BlockSpec((None, MOBA_HEADS, pps * PAGE_SIZE), lambda b, i, pt: (b, 0, i))),
        out_shape=jax.ShapeDtypeStruct((db, MOBA_HEADS, n_pages * PAGE_SIZE), F32),
        compiler_params=_cparams(2, 40),
        name="dec_moba_sweep",
    )(page_table.reshape(-1), *([cache_t] * pps), q_lanes)


def _dec_moba_gate_kernel(s_ref, gs_ref, *, nblk):
    lane = lax.broadcasted_iota(jnp.int32, gs_ref.shape, 1)
    gs = jnp.full(gs_ref.shape, NEG_INF, F32)
    for blk in range(nblk):
        mean = jnp.sum(s_ref[:, blk * MOBA_BLOCK:(blk + 1) * MOBA_BLOCK], axis=1, keepdims=True) / MOBA_BLOCK
        gs = jnp.where(lane == blk, mean, gs)
    gs_ref[...] = gs


def _dec_moba_gate(s_all, nblk):
    db, _, p = s_all.shape
    return pl.pallas_call(
        functools.partial(_dec_moba_gate_kernel, nblk=nblk),
        grid=(db,),
        in_specs=[pl.BlockSpec((None, MOBA_HEADS, p), lambda b: (b, 0, 0))],
        out_specs=pl.BlockSpec((None, MOBA_HEADS, LANES), lambda b: (b, 0, 0)),
        out_shape=jax.ShapeDtypeStruct((db, MOBA_HEADS, LANES), F32),
        compiler_params=_cparams(1),
        name="dec_moba_gate",
    )(s_all)


def _dec_moba_attend_kernel(sel_ref, pt_ref, *refs, pos, k_m):
    ppb = MOBA_BLOCK // PAGE_SIZE
    per_h = k_m + k_m * ppb
    tab_ref, q_ref, kn_ref, vn_ref, o_ref = refs[MOBA_HEADS * per_h:]
    b = pl.program_id(0)
    n_keys = k_m * MOBA_BLOCK
    lane = lax.broadcasted_iota(jnp.int32, (SUBLANES, n_keys), 1)
    tab = tab_ref[...]
    s_self = jnp.sum(q_ref[...] * kn_ref[...], axis=1, keepdims=True) + tab[:, 0:1]
    outs = []
    for h in range(MOBA_HEADS):
        s_refs = refs[h * per_h:h * per_h + k_m]
        v_refs = refs[h * per_h + k_m:(h + 1) * per_h]
        kpos = jnp.zeros((SUBLANES, n_keys), jnp.int32)
        for k in range(k_m):
            jk = sel_ref[(b * MOBA_HEADS + h) * k_m + k]
            kpos = jnp.where(lane // MOBA_BLOCK == k,
                             jnp.where(jk >= 0, jk * MOBA_BLOCK + lane % MOBA_BLOCK, pos + 1), kpos)
        dist = pos - kpos
        tab_h = jnp.broadcast_to(tab[h:h + 1], (SUBLANES, N_BUCKETS))
        s = jnp.broadcast_to(jnp.concatenate([r[h:h + 1, :] for r in s_refs], axis=1), (SUBLANES, n_keys))
        s = jnp.where(dist >= 0, s + _bias_rows(_bucket(dist), tab_h), NEG_INF)
        v_t = jnp.concatenate([r[...] for r in v_refs], axis=1).astype(BF16)
        o = _dec_attend(s, v_t, jnp.broadcast_to(s_self[h:h + 1], (SUBLANES, 1)), vn_ref[h:h + 1, :])
        outs.append(o[0:1])
    o_ref[...] = jnp.concatenate(outs, axis=0)


def _dec_moba_attend(sel_flat, pt_flat, s_all, cache_t, layer, tab_h, q, k_new, v_new, pos, k_m, n_pages, nblk):
    db = q.shape[0]
    ppb = MOBA_BLOCK // PAGE_SIZE

    def sel_of(b, h, sel, k):
        return jnp.clip(sel[(b * MOBA_HEADS + h) * k_m + k], 0, nblk - 1)

    def s_spec(h, k):
        return pl.BlockSpec((None, MOBA_HEADS, MOBA_BLOCK), lambda b, sel, pt: (b, 0, sel_of(b, h, sel, k)))

    def v_spec(h, k, pg):
        return pl.BlockSpec(
            (None, None, None, None, HEAD_DIM, PAGE_SIZE),
            lambda b, sel, pt: (pt[b * n_pages + sel_of(b, h, sel, k) * ppb + pg], layer, 1, h, 0, 0))

    head_specs, head_args = [], []
    for h in range(MOBA_HEADS):
        head_specs += [s_spec(h, k) for k in range(k_m)] + [v_spec(h, k, pg) for k in range(k_m) for pg in range(ppb)]
        head_args += [s_all] * k_m + [cache_t] * (k_m * ppb)
    row_spec = pl.BlockSpec((None, MOBA_HEADS, HEAD_DIM), lambda b, sel, pt: (b, 0, 0))
    return pl.pallas_call(
        functools.partial(_dec_moba_attend_kernel, pos=pos, k_m=k_m),
        grid_spec=pltpu.PrefetchScalarGridSpec(
            num_scalar_prefetch=2,
            grid=(db,),
            in_specs=head_specs + [_full_spec(tab_h, 3), row_spec, row_spec, row_spec],
            out_specs=row_spec),
        out_shape=jax.ShapeDtypeStruct((db, MOBA_HEADS, HEAD_DIM), F32),
        compiler_params=_cparams(1),
        name="dec_moba_attend",
    )(sel_flat, pt_flat, *head_args, tab_h, q, k_new, v_new)


def _overlap_matrix(n_cmp, n_slc, rows, cols):
    i = np.arange(n_cmp)[:, None]
    j = np.arange(n_slc)[None, :]
    units = SLC_BLOCK // CMP_STRIDE
    m = sum(((i + u) // units == j).astype(np.float32) for u in range(CMP_LEN // CMP_STRIDE))
    out = np.zeros((rows, cols), np.float32)
    out[:n_cmp, :n_slc] = m
    return jnp.asarray(out, dtype=BF16)


def _token_group_permutation():
    groups = LANES // CMP_STRIDE
    p = np.zeros((2 * LANES, 2 * LANES), np.float32)
    for c in range(2):
        for l in range(CMP_STRIDE):
            for m in range(groups):
                p[c * LANES + l * groups + m, c * LANES + CMP_STRIDE * m + l] = 1.0
    return jnp.asarray(p, dtype=BF16)


def _gate_expand_matrix():
    e = np.zeros((AG_PAD, 3 * NSA_W), np.float32)
    for h in range(NSA_HEADS):
        for br in range(3):
            e[h * 3 + br, br * NSA_W + h * HEAD_DIM: br * NSA_W + (h + 1) * HEAD_DIM] = 1.0
    return jnp.asarray(e, dtype=BF16)


def _split_w_in(w_in):
    scale = HEAD_DIM ** -0.5 * LOG2E
    o = np.cumsum([0, 512, 128, 128, 128, 128, 128, 128, 24, 512, 512, 512, 512, 512, 1024, 1024])
    a_q, kv3, a_g, a_z, b_q, b_kv, b_z, m_ab = (
        w_in[..., o[0]:o[1]], w_in[..., o[1]:o[7]], w_in[..., o[7]:o[8]], w_in[..., o[8]:o[9]],
        w_in[..., o[9]:o[10]], w_in[..., o[10]:o[12]], w_in[..., o[12]:o[13]], w_in[..., o[13]:o[15]])
    pad = jnp.zeros(w_in.shape[:-1] + (AG_PAD - a_g.shape[-1],), w_in.dtype)
    k_rm = jnp.concatenate([w_in[..., o[10]:o[11]], w_in[..., o[3]:o[4]], w_in[..., o[5]:o[6]]], axis=-1)
    wq = jnp.concatenate([a_q * scale, b_q * scale, k_rm], axis=-1).astype(BF16)
    wz = jnp.concatenate([a_z, b_z, m_ab, a_g, pad], axis=-1).astype(BF16)
    wkv = jnp.swapaxes(jnp.concatenate([kv3, b_kv], axis=-1), 1, 2).astype(BF16)
    return wq, wz, wkv


def _cmp_stage1_weights(w1):
    w1r = w1.reshape(2, CMP_STRIDE, HEAD_DIM, HEAD_DIM)
    eye = jnp.eye(NSA_KV, dtype=w1.dtype)
    w = jnp.einsum("hlde,gf->lgdhfe", w1r, eye)
    return w.reshape(CMP_STRIDE * NSA_KV * HEAD_DIM, 2 * NSA_KV * HEAD_DIM).astype(BF16)


def _cmp_stage2_weights(w2):
    w2t = jnp.swapaxes(w2, 1, 2)
    z = jnp.zeros_like(w2t)
    return jnp.stack([jnp.concatenate([w2t, z], axis=2), jnp.concatenate([z, w2t], axis=2)], axis=1).astype(BF16)


def _cache_view(c):
    return jnp.transpose(c, (0, 1, 3, 4, 5, 2))


def _kv_output(kv_t, heads):
    b, depth, _, t = kv_t.shape
    return jnp.transpose(kv_t.reshape(b, depth, 2, heads, HEAD_DIM, t), (0, 1, 5, 2, 3, 4))


def _layer_prompt(x, mod, lw, consts, layer, depth, kv_prev):
    b, t, d = x.shape
    shift, scale, gate = mod
    q, z, cmp_t, slc_t, win_t, moba_t = _inproj(x, scale, shift, lw["wq"], lw["wz"], lw["wkv"], layer, depth,
                                                kv_prev)
    n_slc = t // SLC_BLOCK
    k_sel = min(SLC_TOPK, n_slc)
    abk, abv = _cmp_proj_prompt(cmp_t, layer, consts["perm"], lw["cmp_wk"], lw["cmp_wv"])
    _, cv_t, ck = _cmp_mlp(abk, abv, lw["pos_flat"], lw["phi_w1"], lw["phi_b1"], lw["cmp_w2t"], lw["phi_b2"])
    o_c, dn = _nsa_cmp_prompt(consts["rel_bias"], q, ck, cv_t, consts["overlap_p"], n_slc, k_sel)
    o_s = _nsa_slc_prompt(q, slc_t, layer, dn, consts["btiles"])
    o_w = _nsa_win_prompt(q, win_t, layer, consts["btiles"])
    nblk = t // MOBA_BLOCK
    o_b = _moba_prompt(q, moba_t, layer, _kmean_prompt(moba_t, layer), consts["btiles"], min(MOBA_TOPK, nblk - 1))
    y = _outproj(x, gate, z, o_c, o_s, o_w, o_b, consts["eg"], lw["w_up_a"], lw["w_up_b"], lw["w_out"],
                 lw["ln_g"], lw["ln_b"], consts["alpha"])
    return y, (cmp_t, slc_t, win_t, moba_t)


def _layer_sample(x, mod, lw, consts, layer, caches, page_table):
    _, db, d = x.shape
    shift, scale, gate = mod
    cache_cmp, cache_slc, cache_moba, state_win = caches
    n_phys, depth = cache_cmp.shape[:2]
    n_pages = page_table.shape[1]
    pos = n_pages * PAGE_SIZE
    pt_flat = page_table.reshape(-1)
    q, z, cmp_t, slc_t, win_t, moba_t = _inproj(x, scale, shift, lw["wq"], lw["wz"], lw["wkv"], 0, 1)
    cmp_n, slc_n, win_n = (a[0, 0].T.reshape(db, 2 * NSA_KV, HEAD_DIM) for a in (cmp_t, slc_t, win_t))
    moba_n = moba_t[0, 0].T.reshape(db, 2, MOBA_HEADS, HEAD_DIM)
    qa = jnp.transpose(q[0, H_QA:H_QA + NSA_HEADS], (1, 0, 2)).reshape(db, NSA_KV, NSA_REP, HEAD_DIM)
    q8 = jnp.pad(qa, ((0, 0), (0, 0), (0, SUBLANES - NSA_REP), (0, 0)))
    qb = jnp.transpose(q[0, H_QB:H_QB + MOBA_HEADS], (1, 0, 2)).astype(F32)
    abk, abv = _cmp_proj_paged(cache_cmp.reshape(n_phys, depth, KV_W, PAGE_SIZE), page_table, layer,
                               consts["perm"], lw["cmp_wk"], lw["cmp_wv"])
    ck_t, cv_t, _ = _cmp_mlp(abk, abv, lw["pos_flat"], lw["phi_w1"], lw["phi_b1"], lw["cmp_w2t"], lw["phi_b2"])
    n_slc = pos // SLC_BLOCK + 1
    k_sel = min(SLC_TOPK, n_slc)
    o_c8, imp = _dec_cmp(consts["tab_g"], q8, ck_t, cv_t, consts["overlap_s"], pos)
    imp2 = imp[:, :, 0].reshape(db * NSA_KV, -1)
    imp2 = jnp.pad(imp2, ((0, LANES - db * NSA_KV), (0, 0)), constant_values=NEG_INF)
    sel = _topk_idx(imp2, n_slc, k_sel)[:k_sel, :db * NSA_KV].T.reshape(-1)
    o_s8 = _dec_slc(sel, pt_flat, cache_slc, layer, consts["tab_g"], q8, slc_n, pos, n_slc, k_sel, n_pages)
    o_w8 = _dec_win(state_win, layer, consts["tab_g"], q8, win_n)
    o_c, o_s, o_w = (a[:, :, :NSA_REP].reshape(1, db, NSA_W) for a in (o_c8, o_s8, o_w8))
    nblk = pos // MOBA_BLOCK
    k_m = min(MOBA_TOPK, nblk)
    q_lanes = jnp.broadcast_to(qb.reshape(db, MOBA_W, 1), (db, MOBA_W, PAGE_SIZE))
    s_all = _dec_moba_sweep(cache_moba.reshape(n_phys, depth, 2, MOBA_W, PAGE_SIZE), page_table, layer, q_lanes)
    gs = _dec_moba_gate(s_all, nblk).reshape(db * MOBA_HEADS, LANES)
    sel_m = _topk_idx(gs, nblk, k_m)[:k_m].T.reshape(-1)
    o_b8 = _dec_moba_attend(sel_m, pt_flat, s_all, cache_moba, layer, consts["tab_h"], qb, moba_n[:, 0],
                            moba_n[:, 1], pos, k_m, n_pages, nblk)
    o_b = o_b8.reshape(1, db, MOBA_W)
    y = _outproj(x, gate, z, o_c, o_s, o_w, o_b, consts["eg"], lw["w_up_a"], lw["w_up_b"], lw["w_out"],
                 lw["ln_g"], lw["ln_b"], consts["alpha"])
    new = dict(cmp=cmp_n.reshape(db, 1, 2, NSA_KV, HEAD_DIM), slc=slc_n.reshape(db, 1, 2, NSA_KV, HEAD_DIM),
               win=win_n.reshape(db, 1, 2, NSA_KV, HEAD_DIM), moba=moba_n.reshape(db, 1, 2, MOBA_HEADS, HEAD_DIM))
    return y, new


def kernel(x_prompt, x_sample, cache_nsa_cmp, cache_nsa_slc, cache_moba, state_nsa_win, page_table, c_prompt, c_sample, rel_bias, w_ada, b_ada, w_in, phi_pos, phi_w1, phi_b1, phi_w2, phi_b2, w_up_a, w_up_b, w_out, ln_g, ln_b):
    b, t, d = x_prompt.shape
    db = x_sample.shape[0]
    depth = w_ada.shape[0]
    n_pages = page_table.shape[1]
    pos = n_pages * PAGE_SIZE
    assert x_sample.shape[1] == 1 and t % TQ == 0 and t >= WINDOW and n_pages % PAGES_PER_STEP == 0
    assert db * NSA_KV <= LANES and state_nsa_win.shape[2] == WINDOW and pos // MOBA_BLOCK >= 1
    assert t // MOBA_BLOCK <= LANES and t // SLC_BLOCK <= LANES and t // CMP_STRIDE >= CMP_NEAR

    mc = -(-(b + db) // SUBLANES) * SUBLANES
    c_all = jnp.pad(jnp.concatenate([c_prompt, c_sample], axis=0), ((0, mc - b - db), (0, 0)))
    mod = _ada(c_all, w_ada, b_ada)

    wq, wz, wkv = _split_w_in(w_in)
    rel_bias = rel_bias * LOG2E
    tab_rel = (rel_bias - rel_bias[N_BUCKETS - 1][None, :]).T
    tab_g = jnp.pad(tab_rel[:NSA_HEADS].reshape(NSA_KV, NSA_REP, N_BUCKETS),
                    ((0, 0), (0, SUBLANES - NSA_REP), (0, 0)))
    n_slc_s = pos // SLC_BLOCK + 1
    consts = dict(
        rel_bias=rel_bias,
        alpha=float((2 * depth) ** 0.25),
        btiles=_bias_tiles(rel_bias),
        perm=_token_group_permutation(),
        overlap_p=_overlap_matrix(t // CMP_STRIDE - 1, t // SLC_BLOCK, t // CMP_STRIDE, LANES).T,
        overlap_s=_overlap_matrix(pos // CMP_STRIDE - 1, n_slc_s, pos // CMP_STRIDE, -(-n_slc_s // LANES) * LANES),
        eg=_gate_expand_matrix(),
        tab_g=tab_g,
        tab_h=tab_rel[NSA_HEADS:],
    )
    caches = (_cache_view(cache_nsa_cmp), _cache_view(cache_nsa_slc), _cache_view(cache_moba),
              jnp.transpose(state_nsa_win, (0, 1, 3, 4, 5, 2)))

    yp, ys = x_prompt, x_sample.reshape(1, db, d)
    kv_p, new_s = None, []
    for l in range(depth):
        lw = dict(
            wq=wq[l], wz=wz[l], wkv=wkv[l],
            cmp_wk=_cmp_stage1_weights(phi_w1[l, 0]), cmp_wv=_cmp_stage1_weights(phi_w1[l, 1]),
            pos_flat=phi_pos[l].reshape(2, 1, CMP_LEN * HEAD_DIM),
            phi_w1=phi_w1[l], phi_b1=phi_b1[l].reshape(2, 1, HEAD_DIM),
            cmp_w2t=_cmp_stage2_weights(phi_w2[l]), phi_b2=phi_b2[l].reshape(2, HEAD_DIM, 1),
            w_up_a=w_up_a[l].astype(BF16), w_up_b=w_up_b[l].astype(BF16), w_out=w_out[l].astype(BF16),
            ln_g=ln_g[l].reshape(1, d), ln_b=ln_b[l].reshape(1, d))
        shift, scale, gate = jnp.split(mod[l], 3, axis=-1)
        mod_p = tuple(a[:b, None, :] for a in (shift, scale, gate))
        mod_s = tuple(a[None, b:b + db, :] for a in (shift, scale, gate))
        yp, kv_p = _layer_prompt(yp, mod_p, lw, consts, l, depth, kv_p)
        ys, ns_ = _layer_sample(ys, mod_s, lw, consts, l, caches, page_table)
        new_s.append(ns_)

    def stack_s(key):
        return jnp.stack([it[key] for it in new_s], axis=1)

    cmp_p, slc_p, win_p, moba_p = kv_p
    win_p = jnp.moveaxis(_kv_output(win_p[..., t - min(WINDOW, t):], NSA_KV), 1, 0)
    return (yp, ys.reshape(db, 1, d),
            _kv_output(cmp_p, NSA_KV), stack_s("cmp"),
            _kv_output(slc_p, NSA_KV), stack_s("slc"),
            _kv_output(moba_p, MOBA_HEADS), stack_s("moba"),
            win_p, jnp.moveaxis(stack_s("win"), 1, 0))
```

```python
import functools
import math

import numpy as np
import jax
import jax.numpy as jnp
from jax import lax
from jax.experimental import pallas as pl
from jax.experimental.pallas import tpu as pltpu

F32 = jnp.float32
BF16 = jnp.bfloat16
NEG_INF = float("-inf")
MASK_BIG = 2.0 ** 127
LOG2E = math.log2(math.e)

HEAD_DIM = 64
NSA_HEADS = 8
NSA_KV = 2
NSA_REP = NSA_HEADS // NSA_KV
CMP_LEN = 32
CMP_STRIDE = 16
SLC_BLOCK = 64
SLC_TOPK = 16
WINDOW = 512
MOBA_HEADS = 8
MOBA_BLOCK = 256
MOBA_TOPK = 3
N_HEADS = NSA_HEADS + MOBA_HEADS
N_BUCKETS = 32
MAX_EXACT = N_BUCKETS // 2
MAX_DISTANCE = 128
LN_EPS = 1e-5
PAGE_SIZE = 128

LANES = 128
SUBLANES = 8
TQ = 256
MOBA_HB = 8
CMP_NEAR = 40
H_QA, H_QB, H_KB, H_KS, H_KW = 0, 8, 16, 24, 26
N_QK = 28
PAGES_PER_STEP = 16

NSA_W = NSA_HEADS * HEAD_DIM
MOBA_W = MOBA_HEADS * HEAD_DIM
KV_W = 2 * NSA_KV * HEAD_DIM
MOBA_KV_W = 2 * MOBA_W
Z_AZ, Z_BZ, Z_MA, Z_MB, Z_AG = 0, 512, 1024, 2048, 3072
AG_PAD = 128
Z_W = Z_AG + AG_PAD


def _cparams(n_axes, vmem_mb=None):
    kw = dict(dimension_semantics=("arbitrary",) * n_axes)
    if vmem_mb is not None:
        kw["vmem_limit_bytes"] = vmem_mb * 1024 * 1024
    return pltpu.CompilerParams(**kw)


def _dot(a, b):
    return jnp.dot(a, b, preferred_element_type=F32)


def _dot_nt(a, b):
    return lax.dot_general(a, b, (((1,), (1,)), ((), ())), preferred_element_type=F32)


def _sigmoid(x):
    return 1.0 / (1.0 + jnp.exp(-x))


def _silu(x):
    return x * _sigmoid(x)


def _bucket(dist):
    n = jnp.maximum(dist, 0)
    nf = jnp.maximum(n, 1).astype(F32)
    large = MAX_EXACT + (jnp.log(nf / MAX_EXACT) / math.log(MAX_DISTANCE / MAX_EXACT)
                         * (N_BUCKETS - MAX_EXACT)).astype(jnp.int32)
    return jnp.where(n < MAX_EXACT, n, jnp.minimum(large, N_BUCKETS - 1))


def _bias_rows(bkt, tab):
    bias = jnp.zeros(bkt.shape, F32)
    for b in range(N_BUCKETS):
        bias = jnp.where(bkt == b, tab[:, b:b + 1], bias)
    return bias


def _rank_rows(blocks, n):
    nb = len(blocks)
    cnt = [jnp.zeros(blocks[0].shape, F32) for _ in range(nb)]
    sub = lax.broadcasted_iota(jnp.int32, blocks[0].shape, 0)
    for jp in range(n):
        rb0, r0 = divmod(jp, SUBLANES)
        row = blocks[rb0][r0:r0 + 1, :]
        for rb in range(nb):
            a = blocks[rb]
            if rb < rb0:
                ahead = jnp.where(row > a, 1.0, 0.0)
            elif rb > rb0:
                ahead = jnp.where(row >= a, 1.0, 0.0)
            else:
                ahead = jnp.where(sub > r0, jnp.where(row >= a, 1.0, 0.0), jnp.where(row > a, 1.0, 0.0))
            cnt[rb] = cnt[rb] + ahead
    return cnt


def _topk_drop_rows(score_t, n, k):
    jl, q = score_t.shape
    nb = -(-n // SUBLANES)
    blocks = [score_t[rb * SUBLANES:(rb + 1) * SUBLANES, :] for rb in range(nb)]
    cnt = _rank_rows(blocks, n)
    drop = [jnp.where(c < k, jnp.where(a > NEG_INF, 0.0, -MASK_BIG), -MASK_BIG) for c, a in zip(cnt, blocks)]
    if nb * SUBLANES < jl:
        drop.append(jnp.full((jl - nb * SUBLANES, q), -MASK_BIG, F32))
    return jnp.concatenate(drop, axis=0)


def _softmax_init(m_scr, l_scr, acc_scr):
    m_scr[...] = jnp.full(m_scr.shape, -MASK_BIG, F32)
    l_scr[...] = jnp.zeros(l_scr.shape, F32)
    acc_scr[...] = jnp.zeros(acc_scr.shape, F32)


def _flash_tile(qs, ks, v_ts, states, adds, masks=None):
    n_s = len(qs)
    ss = [_dot_nt(ks[c], qs[c]) for c in range(n_s)]
    ps, alphas = [], []
    for c in range(n_s):
        s = ss[c]
        for a in adds[c]:
            s = s + a
        m_scr, l_scr, _ = states[c]
        m_prev = m_scr[...]
        if masks is None:
            m_next = jnp.maximum(m_prev, jnp.max(s, axis=0, keepdims=True))
            p = jnp.exp2(s - m_next)
        else:
            rows = s.shape[0] // len(masks[c])
            slabs = [s[j * rows:(j + 1) * rows] for j in range(len(masks[c]))]
            tops = [jnp.max(sl, axis=0, keepdims=True) for sl in slabs]
            m_next = m_prev
            for top, mask in zip(tops, masks[c]):
                m_next = jnp.maximum(m_next, top + mask)
            p = jnp.concatenate([jnp.exp2(sl - jnp.maximum(m_next - mask, top))
                                 for sl, top, mask in zip(slabs, tops, masks[c])], axis=0)
        alpha = jnp.exp2(m_prev - m_next)
        l_scr[...] = alpha * l_scr[...] + jnp.sum(p, axis=0, keepdims=True)
        m_scr[...] = m_next
        ps.append(p.astype(BF16))
        alphas.append(alpha)
    for c in range(n_s):
        acc_scr = states[c][2]
        acc_scr[...] = acc_scr[...] * alphas[c] + _dot(v_ts[c], ps[c])


def _far_tiles(n, tile):
    def pair(j, c):
        tile(2 * j)
        tile(2 * j + 1)
        return c

    lax.fori_loop(0, n // 2, pair, 0)

    @pl.when(n % 2 == 1)
    def _():
        tile(n - 1)


def _flash_scratch(n_streams):
    per = [pltpu.VMEM((1, TQ), F32), pltpu.VMEM((1, TQ), F32), pltpu.VMEM((HEAD_DIM, TQ), F32)]
    return per * n_streams


def _flash_states(scr):
    return [tuple(scr[3 * c:3 * c + 3]) for c in range(len(scr) // 3)]


def _softmax_finish(l_scr, acc_scr):
    return (acc_scr[...] / jnp.maximum(l_scr[...], 1e-30)).T


def _full_spec(a, n_grid, single=True):
    kw = dict(pipeline_mode=pl.Buffered(1)) if single else {}
    return pl.BlockSpec(a.shape, lambda *_: (0,) * a.ndim, **kw)


def _ada_kernel(c_ref, w_ref, b_ref, o_ref):
    a = _silu(c_ref[...]).astype(BF16)
    o_ref[...] = _dot(a, w_ref[...].astype(BF16)) + b_ref[...]


def _ada(c_all, w_ada, b_ada):
    depth, d, n3 = w_ada.shape
    mc = c_all.shape[0]
    tn = 1024
    return pl.pallas_call(
        _ada_kernel,
        grid=(depth, n3 // tn),
        in_specs=[pl.BlockSpec((mc, d), lambda l, j: (0, 0)),
                  pl.BlockSpec((None, d, tn), lambda l, j: (l, 0, j)),
                  pl.BlockSpec((None, 1, tn), lambda l, j: (l, 0, j))],
        out_specs=pl.BlockSpec((None, mc, tn), lambda l, j: (l, 0, j)),
        out_shape=jax.ShapeDtypeStruct((depth, mc, n3), F32),
        compiler_params=_cparams(2, 40),
        name="ada_mod",
    )(c_all, w_ada, b_ada.reshape(depth, 1, n3))


def _inproj_kernel(x_ref, sc_ref, sh_ref, wq_ref, wz_ref, wkv_ref, *refs):
    q_ref, z_ref, cmp_ref, slc_ref, win_ref, moba_ref = refs[-6:]
    h = (x_ref[...] * (1.0 + sc_ref[...]) + sh_ref[...]).astype(BF16)
    q = _dot(h, wq_ref[...])
    for hd in range(N_QK):
        q_ref[hd] = q[:, hd * HEAD_DIM:(hd + 1) * HEAD_DIM].astype(BF16)
    z_ref[...] = _dot(h, wz_ref[...])
    r = 0
    for o_ref in (cmp_ref, slc_ref, win_ref, moba_ref):
        n = o_ref.shape[0]
        o_ref[...] = _dot_nt(wkv_ref[r:r + n, :], h)
        r += n


def _mod_spec(mod, tm, nt):
    if mod.shape[1] == 1:
        return pl.BlockSpec((None, 1, mod.shape[2]), lambda m: (m // nt, 0, 0))
    return pl.BlockSpec((None, tm, mod.shape[2]), lambda m: (m // nt, m % nt, 0))


def _inproj(x, scale, shift, wq, wz, wkv, layer, depth, kv_prev=None):
    bx, t, d = x.shape
    tm = min(t, 512)
    nt = t // tm

    def kv_spec(rows):
        return pl.BlockSpec((None, None, rows, tm), lambda m: (m // nt, layer, 0, m % nt))

    def kv_shape(rows):
        return jax.ShapeDtypeStruct((bx, depth, rows, t), F32)

    prev = () if kv_prev is None else tuple(kv_prev)
    n_in = 6
    return pl.pallas_call(
        _inproj_kernel,
        grid=(bx * nt,),
        in_specs=[pl.BlockSpec((None, tm, d), lambda m: (m // nt, m % nt, 0)),
                  _mod_spec(scale, tm, nt), _mod_spec(shift, tm, nt),
                  _full_spec(wq, 1), _full_spec(wz, 1), _full_spec(wkv, 1)] + [
                      pl.BlockSpec(memory_space=pl.ANY) for _ in prev],
        out_specs=[pl.BlockSpec((None, N_QK, tm, HEAD_DIM), lambda m: (m // nt, 0, m % nt, 0)),
                   pl.BlockSpec((None, tm, Z_W), lambda m: (m // nt, m % nt, 0)),
                   kv_spec(KV_W), kv_spec(KV_W), kv_spec(KV_W), kv_spec(MOBA_KV_W)],
        out_shape=[jax.ShapeDtypeStruct((bx, N_QK, t, HEAD_DIM), BF16),
                   jax.ShapeDtypeStruct((bx, t, Z_W), F32),
                   kv_shape(KV_W), kv_shape(KV_W), kv_shape(KV_W), kv_shape(MOBA_KV_W)],
        input_output_aliases={n_in + k: 2 + k for k in range(len(prev))},
        compiler_params=_cparams(1, 56),
        name="in_proj",
    )(x, scale, shift, wq, wz, wkv, *prev)


def _cmp_proj_kernel(*refs, n_x, n_prefetch=0):
    refs = refs[n_prefetch:]
    perm_ref, wk_ref, wv_ref, abk_ref, abv_ref = refs[n_x:]
    perm = perm_ref[...]
    chunks = [x_ref[:, c * LANES:(c + 1) * LANES].astype(BF16)
              for x_ref in refs[:n_x] for c in range(x_ref.shape[1] // LANES)]
    groups = LANES // CMP_STRIDE
    rows_k, rows_v = [], []
    for c0 in range(0, len(chunks), 2):
        pair = chunks[c0:c0 + 2]
        xp = _dot_nt(perm[:len(pair) * LANES, :len(pair) * LANES], jnp.concatenate(pair, axis=1))
        for ci in range(len(pair)):
            parts = [xp[ci * LANES + l * groups:ci * LANES + (l + 1) * groups] for l in range(CMP_STRIDE)]
            rows_k.append(jnp.concatenate([p[:, :KV_W // 2] for p in parts], axis=1))
            rows_v.append(jnp.concatenate([p[:, KV_W // 2:] for p in parts], axis=1))
    for rows, w_ref, ab_ref in ((rows_k, wk_ref, abk_ref), (rows_v, wv_ref, abv_ref)):
        xr = jnp.concatenate(rows, axis=0).astype(BF16)
        ab_ref[...] = _dot(xr, w_ref[...])


def _cmp_proj_prompt(cmp_t, layer, perm, wk, wv):
    bx, _, _, t = cmp_t.shape
    tc = min(t, PAGES_PER_STEP * LANES)
    m = tc // CMP_STRIDE
    ospec = pl.BlockSpec((None, m, 256), lambda b, i: (b, i, 0))
    oshape = jax.ShapeDtypeStruct((bx, t // CMP_STRIDE, 256), F32)
    return pl.pallas_call(
        functools.partial(_cmp_proj_kernel, n_x=1),
        grid=(bx, t // tc),
        in_specs=[pl.BlockSpec((None, None, KV_W, tc), lambda b, i: (b, layer, 0, i)),
                  _full_spec(perm, 2), _full_spec(wk, 2), _full_spec(wv, 2)],
        out_specs=[ospec, ospec],
        out_shape=[oshape, oshape],
        compiler_params=_cparams(2),
        name="cmp_proj_prompt",
    )(cmp_t, perm, wk, wv)


def _cmp_proj_paged(cache_t, page_table, layer, perm, wk, wv):
    db, n_pages = page_table.shape
    pps = PAGES_PER_STEP
    m = PAGE_SIZE // CMP_STRIDE

    def page_spec(k):
        return pl.BlockSpec((None, None, KV_W, PAGE_SIZE),
                            lambda b, i, pt: (pt[b * n_pages + i * pps + k], layer, 0, 0))

    ospec = pl.BlockSpec((None, pps * m, 256), lambda b, i, pt: (b, i, 0))
    oshape = jax.ShapeDtypeStruct((db, n_pages * m, 256), F32)
    return pl.pallas_call(
        functools.partial(_cmp_proj_kernel, n_x=pps, n_prefetch=1),
        grid_spec=pltpu.PrefetchScalarGridSpec(
            num_scalar_prefetch=1,
            grid=(db, n_pages // pps),
            in_specs=[page_spec(k) for k in range(pps)] + [
                _full_spec(perm, 3), _full_spec(wk, 3), _full_spec(wv, 3)],
            out_specs=[ospec, ospec]),
        out_shape=[oshape, oshape],
        compiler_params=_cparams(2),
        name="cmp_proj_paged",
    )(page_table.reshape(-1), *([cache_t] * pps), perm, wk, wv)


def _cmp_mlp_kernel(abk_ref, abv_ref, pos_ref, w1_ref, b1_ref, w2_ref, b2_ref, ck_ref, cv_ref, ckr_ref):
    m = abk_ref.shape[0]
    col = lax.broadcasted_iota(jnp.int32, (HEAD_DIM, m), 1)
    for kv, (ab_ref, o_ref) in enumerate(((abk_ref, ck_ref), (abv_ref, cv_ref))):
        pos = jnp.broadcast_to(pos_ref[kv], (SUBLANES, CMP_LEN * HEAD_DIM)).astype(BF16)
        c0 = _dot(pos, w1_ref[kv].astype(BF16))[0:1, :] + b1_ref[kv]
        c0 = jnp.concatenate([c0] * NSA_KV, axis=1)
        ab = ab_ref[...]
        nxt = pltpu.roll(ab[:, LANES:], m - 1, 0)
        h = jax.nn.gelu(ab[:, :LANES] + nxt + c0).astype(BF16)
        for g in range(NSA_KV):
            y_t = jnp.where(col < m - 1, _dot_nt(w2_ref[kv, g], h) + b2_ref[kv], 0.0)
            o_ref[g] = y_t.astype(BF16)
            if kv == 0:
                ckr_ref[g] = y_t.T.astype(BF16)


def _cmp_mlp(abk, abv, pos_flat, w1, b1, w2t, b2col):
    bx, m, _ = abk.shape
    abspec = pl.BlockSpec((None, m, 256), lambda b: (b, 0, 0))
    ospec = pl.BlockSpec((None, NSA_KV, HEAD_DIM, m), lambda b: (b, 0, 0, 0))
    oshape = jax.ShapeDtypeStruct((bx, NSA_KV, HEAD_DIM, m), BF16)
    return pl.pallas_call(
        _cmp_mlp_kernel,
        grid=(bx,),
        in_specs=[abspec, abspec, _full_spec(pos_flat, 1), _full_spec(w1, 1), _full_spec(b1, 1),
                  _full_spec(w2t, 1), _full_spec(b2col, 1)],
        out_specs=[ospec, ospec, pl.BlockSpec((None, NSA_KV, m, HEAD_DIM), lambda b: (b, 0, 0, 0))],
        out_shape=[oshape, oshape, jax.ShapeDtypeStruct((bx, NSA_KV, m, HEAD_DIM), BF16)],
        compiler_params=_cparams(1),
        name="cmp_mlp",
    )(abk, abv, pos_flat, w1, b1, w2t, b2col)


def _bias_tiles_kernel(tab_ref, o_ref):
    h = pl.program_id(0)
    key = lax.broadcasted_iota(jnp.int32, (TQ, TQ), 0)
    qry = lax.broadcasted_iota(jnp.int32, (TQ, TQ), 1)
    far = tab_ref[N_BUCKETS - 1, h]
    for kind in range(2):
        dist = qry - key + kind * TQ
        bkt = _bucket(dist)
        bias = jnp.zeros((TQ, TQ), F32)
        for b in range(N_BUCKETS - 1):
            bias = jnp.where(bkt == b, tab_ref[b, h] - far, bias)
        o_ref[kind] = jnp.where(dist >= 0, bias, NEG_INF)


def _bias_tiles(rel_bias):
    nh = rel_bias.shape[1]
    return pl.pallas_call(
        _bias_tiles_kernel,
        grid=(nh,),
        in_specs=[pl.BlockSpec(memory_space=pltpu.SMEM)],
        out_specs=pl.BlockSpec((2, None, TQ, TQ), lambda h: (0, h, 0, 0)),
        out_shape=jax.ShapeDtypeStruct((2, nh, TQ, TQ), F32),
        compiler_params=_cparams(1),
        name="bias_tiles",
    )(rel_bias)


def _nsa_cmp_kernel(tab_ref, q_ref, ck_ref, cv_ref, ov_ref, oc_ref, dn_ref, *s_scr, n_slc, k_sel):
    i = pl.program_id(1)
    ncp = ck_ref.shape[1]
    jl = ov_ref.shape[0]
    q0 = i * TQ
    qpos = q0 + lax.broadcasted_iota(jnp.int32, (ncp, TQ), 1)
    cend = lax.broadcasted_iota(jnp.int32, (ncp, TQ), 0) * CMP_STRIDE + (CMP_LEN - 1)
    valid = qpos >= cend
    n0 = jnp.maximum(q0 - (MAX_DISTANCE + CMP_LEN - 1), 0) // CMP_STRIDE
    n0 = pl.multiple_of(jnp.minimum(n0 // SUBLANES * SUBLANES, ncp - CMP_NEAR), SUBLANES)
    near_end = (n0 + lax.broadcasted_iota(jnp.int32, (CMP_NEAR, TQ), 0)) * CMP_STRIDE + (CMP_LEN - 1)
    bkt = _bucket(q0 + lax.broadcasted_iota(jnp.int32, (CMP_NEAR, TQ), 1) - near_end)
    j = lax.broadcasted_iota(jnp.int32, (jl, TQ), 0)
    cur = (q0 + lax.broadcasted_iota(jnp.int32, (jl, TQ), 1)) // SLC_BLOCK
    forced = (j == 0) | (j == cur) | (j == cur - 1)
    outs = []
    for g in range(NSA_KV):
        heads = range(g * NSA_REP, (g + 1) * NSA_REP)
        bias = [jnp.zeros((CMP_NEAR, TQ), F32) for _ in heads]
        for b in range(N_BUCKETS - 1):
            hit = bkt == b
            bias = [jnp.where(hit, tab_ref[b, h] - tab_ref[N_BUCKETS - 1, h], bb) for h, bb in zip(heads, bias)]
        ck = ck_ref[g]
        cv = cv_ref[g]
        imp = jnp.zeros((jl, TQ), F32)
        for r, h in enumerate(heads):
            scr = s_scr[r]
            scr[...] = _dot_nt(ck, q_ref[h])
            scr[pl.ds(n0, CMP_NEAR), :] = scr[pl.ds(n0, CMP_NEAR), :] + bias[r]
            s = jnp.where(valid, scr[...], NEG_INF)
            m = jnp.max(s, axis=0, keepdims=True)
            m = jnp.where(m == NEG_INF, 0.0, m)
            e = jnp.exp2(s - m)
            p = (e / jnp.maximum(jnp.sum(e, axis=0, keepdims=True), 1e-30)).astype(BF16)
            outs.append(_dot(cv, p).T)
            imp = imp + _dot(ov_ref[...], p)
        imp = jnp.where(forced, jnp.inf, imp)
        imp = jnp.where(j <= cur, imp, NEG_INF)
        dn_ref[g] = _topk_drop_rows(imp, n_slc, k_sel)
    oc_ref[...] = jnp.concatenate(outs, axis=1)


def _nsa_cmp_prompt(rel_bias, qk, ck, cv_t, overlap_t, n_slc, k_sel):
    b, _, t, _ = qk.shape
    ncp = ck.shape[2]
    jl = overlap_t.shape[0]
    return pl.pallas_call(
        functools.partial(_nsa_cmp_kernel, n_slc=n_slc, k_sel=k_sel),
        grid=(b, t // TQ),
        in_specs=[pl.BlockSpec(memory_space=pltpu.SMEM),
                  pl.BlockSpec((None, NSA_HEADS, TQ, HEAD_DIM), lambda bb, i: (bb, H_QA // NSA_HEADS, i, 0)),
                  pl.BlockSpec((None, NSA_KV, ncp, HEAD_DIM), lambda bb, i: (bb, 0, 0, 0)),
                  pl.BlockSpec((None, NSA_KV, HEAD_DIM, ncp), lambda bb, i: (bb, 0, 0, 0)),
                  _full_spec(overlap_t, 2)],
        out_specs=[pl.BlockSpec((None, TQ, NSA_W), lambda bb, i: (bb, i, 0)),
                   pl.BlockSpec((None, NSA_KV, jl, TQ), lambda bb, i: (bb, 0, 0, i))],
        out_shape=[jax.ShapeDtypeStruct((b, t, NSA_W), F32),
                   jax.ShapeDtypeStruct((b, NSA_KV, jl, t), F32)],
        scratch_shapes=[pltpu.VMEM((ncp, TQ), F32)] * NSA_REP,
        compiler_params=_cparams(2, 40),
        name="nsa_cmp_prompt",
    )(rel_bias, qk, ck, cv_t, overlap_t)


def _k_tile(ref, kt):
    return ref[pl.ds(pl.multiple_of(kt * TQ, TQ), TQ), :]


def _v_tile(ref, kt):
    return ref[:, pl.ds(pl.multiple_of(kt * TQ, TQ), TQ)].astype(BF16)


def _nsa_slc_kernel(q_ref, k0_ref, k1_ref, v0_ref, v1_ref, dn_ref, bt_ref, o_ref, *scr):
    i = pl.program_id(1)
    states = _flash_states(scr)
    qs = [q_ref[h] for h in range(NSA_HEADS)]
    per_tile = TQ // SLC_BLOCK
    for st in states:
        _softmax_init(*st)

    def tile(kt, kind):
        ks, v_ts, adds, masks = [], [], [], []
        for g, (k_ref, v_ref) in enumerate(((k0_ref, v0_ref), (k1_ref, v1_ref))):
            rows = [dn_ref[g, pl.ds(kt * per_tile + jb, 1), :] for jb in range(per_tile)]
            ks += [_k_tile(k_ref, kt)] * NSA_REP
            v_ts += [_v_tile(v_ref, kt)] * NSA_REP
            adds += [[] if kind is None else [bt_ref[kind, g * NSA_REP + r]] for r in range(NSA_REP)]
            masks += [rows] * NSA_REP
        _flash_tile(qs, ks, v_ts, states, adds, masks)

    _far_tiles(jnp.maximum(i - 1, 0), lambda kt: tile(kt, None))

    @pl.when(i >= 1)
    def _():
        tile(i - 1, 1)

    tile(i, 0)
    o_ref[...] = jnp.concatenate([_softmax_finish(st[1], st[2]) for st in states], axis=1)


def _nsa_slc_prompt(qk, kv_t, layer, dn, btiles):
    b, _, t, _ = qk.shape
    jl = dn.shape[2]
    return pl.pallas_call(
        _nsa_slc_kernel,
        grid=(b, t // TQ),
        in_specs=_nsa_dense_specs(t, H_KS, layer) + [
            pl.BlockSpec((None, NSA_KV, jl, TQ), lambda bb, i: (bb, 0, 0, i)),
            pl.BlockSpec((2, NSA_HEADS, TQ, TQ), lambda bb, i: (0, 0, 0, 0))],
        out_specs=pl.BlockSpec((None, TQ, NSA_W), lambda bb, i: (bb, i, 0)),
        out_shape=jax.ShapeDtypeStruct((b, t, NSA_W), F32),
        scratch_shapes=_flash_scratch(NSA_HEADS),
        compiler_params=_cparams(2, 40),
        name="nsa_slc_prompt",
    )(qk, qk, qk, kv_t, kv_t, dn, btiles)


def _nsa_dense_specs(t, k_head, layer):
    return [pl.BlockSpec((None, NSA_HEADS, TQ, HEAD_DIM), lambda bb, i: (bb, H_QA // NSA_HEADS, i, 0)),
            pl.BlockSpec((None, None, t, HEAD_DIM), lambda bb, i: (bb, k_head, 0, 0)),
            pl.BlockSpec((None, None, t, HEAD_DIM), lambda bb, i: (bb, k_head + 1, 0, 0)),
            pl.BlockSpec((None, None, HEAD_DIM, t), lambda bb, i: (bb, layer, NSA_KV, 0)),
            pl.BlockSpec((None, None, HEAD_DIM, t), lambda bb, i: (bb, layer, NSA_KV + 1, 0))]


def _nsa_win_kernel(q_ref, k0_ref, k1_ref, v0_ref, v1_ref, bt_ref, o_ref, *scr):
    i = pl.program_id(1)
    states = _flash_states(scr)
    qs = [q_ref[h] for h in range(NSA_HEADS)]
    for st in states:
        _softmax_init(*st)

    def tile(kt, adds):
        ks = [_k_tile(k0_ref, kt)] * NSA_REP + [_k_tile(k1_ref, kt)] * NSA_REP
        v_ts = [_v_tile(v0_ref, kt)] * NSA_REP + [_v_tile(v1_ref, kt)] * NSA_REP
        _flash_tile(qs, ks, v_ts, states, adds)

    @pl.when(i >= WINDOW // TQ)
    def _():
        key = lax.broadcasted_iota(jnp.int32, (TQ, TQ), 0)
        qry = lax.broadcasted_iota(jnp.int32, (TQ, TQ), 1)
        tile(i - WINDOW // TQ, [[jnp.where(key > qry, 0.0, NEG_INF)]] * NSA_HEADS)

    @pl.when(i >= 1)
    def _():
        tile(i - 1, [[bt_ref[1, h]] for h in range(NSA_HEADS)])

    tile(i, [[bt_ref[0, h]] for h in range(NSA_HEADS)])
    o_ref[...] = jnp.concatenate([_softmax_finish(st[1], st[2]) for st in states], axis=1)


def _nsa_win_prompt(qk, kv_t, layer, btiles):
    b, _, t, _ = qk.shape
    return pl.pallas_call(
        _nsa_win_kernel,
        grid=(b, t // TQ),
        in_specs=_nsa_dense_specs(t, H_KW, layer) + [
            pl.BlockSpec((2, NSA_HEADS, TQ, TQ), lambda bb, i: (0, 0, 0, 0))],
        out_specs=pl.BlockSpec((None, TQ, NSA_W), lambda bb, i: (bb, i, 0)),
        out_shape=jax.ShapeDtypeStruct((b, t, NSA_W), F32),
        scratch_shapes=_flash_scratch(NSA_HEADS),
        compiler_params=_cparams(2, 40),
        name="nsa_win_prompt",
    )(qk, qk, qk, kv_t, kv_t, btiles)


def _kmean_kernel(k_ref, o_ref, *, nblk):
    lane = lax.broadcasted_iota(jnp.int32, (MOBA_W, LANES), 1)
    acc = jnp.zeros((MOBA_W, LANES), F32)
    for blk in range(nblk):
        mean = jnp.sum(k_ref[:, blk * MOBA_BLOCK:(blk + 1) * MOBA_BLOCK], axis=1, keepdims=True) / MOBA_BLOCK
        acc = jnp.where(lane == blk, mean, acc)
    for h in range(MOBA_HEADS):
        o_ref[h] = acc[h * HEAD_DIM:(h + 1) * HEAD_DIM, :].T.astype(BF16)


def _kmean_prompt(moba_t, layer):
    bx, _, _, t = moba_t.shape
    return pl.pallas_call(
        functools.partial(_kmean_kernel, nblk=t // MOBA_BLOCK),
        grid=(bx,),
        in_specs=[pl.BlockSpec((None, None, MOBA_W, t), lambda b: (b, layer, 0, 0))],
        out_specs=pl.BlockSpec((None, MOBA_HEADS, LANES, HEAD_DIM), lambda b: (b, 0, 0, 0)),
        out_shape=jax.ShapeDtypeStruct((bx, MOBA_HEADS, LANES, HEAD_DIM), BF16),
        compiler_params=_cparams(1, 40),
        name="moba_kmean_prompt",
    )(moba_t)


def _moba_kernel(q_ref, k_ref, v_ref, km_ref, bt_ref, o_ref, *scr, nblk, k_m):
    i = pl.program_id(2)
    hb = q_ref.shape[0]
    states = _flash_states(scr[:3 * hb])
    dn_scr = scr[3 * hb:]
    qs = [q_ref[hh] for hh in range(hb)]
    for hh in range(hb):
        gs = _dot_nt(km_ref[hh], qs[hh])
        blk = lax.broadcasted_iota(jnp.int32, gs.shape, 0)
        dn_scr[hh][...] = _topk_drop_rows(jnp.where(blk < i, gs, NEG_INF), nblk, k_m)
        _softmax_init(*states[hh])

    def tile(kt, bias_kind, masked):
        start = pl.multiple_of(kt * TQ, TQ)
        ks = [k_ref[hh, pl.ds(start, TQ), :] for hh in range(hb)]
        v_ts = [v_ref[hh * HEAD_DIM:(hh + 1) * HEAD_DIM, pl.ds(start, TQ)].astype(BF16) for hh in range(hb)]
        adds = [[] if bias_kind is None else [bt_ref[bias_kind, hh]] for hh in range(hb)]
        masks = [[dn_scr[hh][pl.ds(kt, 1), :]] for hh in range(hb)] if masked else None
        _flash_tile(qs, ks, v_ts, states, adds, masks)

    _far_tiles(jnp.maximum(i - 1, 0), lambda kt: tile(kt, None, True))

    @pl.when(i >= 1)
    def _():
        tile(i - 1, 1, True)

    tile(i, 0, False)
    o_ref[...] = jnp.concatenate([_softmax_finish(st[1], st[2]) for st in states], axis=1)


def _moba_prompt(qk, moba_t, layer, kmean, btiles, k_m):
    b, _, t, _ = qk.shape
    hb = MOBA_HB
    nq = t // TQ
    vb = MOBA_W // (hb * HEAD_DIM)
    once = dict(pipeline_mode=pl.Buffered(1))
    return pl.pallas_call(
        functools.partial(_moba_kernel, nblk=t // MOBA_BLOCK, k_m=k_m),
        grid=(b, MOBA_HEADS // hb, nq),
        in_specs=[pl.BlockSpec((None, hb, TQ, HEAD_DIM), lambda bb, hp, i: (bb, H_QB // hb + hp, i, 0)),
                  pl.BlockSpec((None, hb, t, HEAD_DIM), lambda bb, hp, i: (bb, H_KB // hb + hp, 0, 0), **once),
                  pl.BlockSpec((None, None, hb * HEAD_DIM, t), lambda bb, hp, i: (bb, layer, vb + hp, 0), **once),
                  pl.BlockSpec((None, hb, LANES, HEAD_DIM), lambda bb, hp, i: (bb, hp, 0, 0)),
                  pl.BlockSpec((2, hb, TQ, TQ), lambda bb, hp, i: (0, NSA_HEADS // hb + hp, 0, 0), **once)],
        out_specs=pl.BlockSpec((None, TQ, hb * HEAD_DIM), lambda bb, hp, i: (bb, i, hp)),
        out_shape=jax.ShapeDtypeStruct((b, t, MOBA_W), F32),
        scratch_shapes=_flash_scratch(hb) + [pltpu.VMEM((LANES, TQ), F32)] * hb,
        compiler_params=_cparams(3, 48),
        name="moba_prompt",
    )(qk, qk, moba_t, kmean, btiles)


def _outproj_kernel(x_ref, gate_ref, az_ref, bz_ref, ma_ref, mb_ref, ag_ref, oc_ref, os_ref, ow_ref, ob_ref,
                    eg_ref, wua_ref, wub_ref, wo_ref, lng_ref, lnb_ref, y_ref, *, alpha):
    w = NSA_W
    g = _sigmoid(ag_ref[...])
    g_hi = g.astype(BF16)
    g_lo = (g - g_hi.astype(F32)).astype(BF16)
    ge = _dot(g_hi, eg_ref[...]) + _dot(g_lo, eg_ref[...])
    o_a = ge[:, 0:w] * oc_ref[...] + ge[:, w:2 * w] * os_ref[...] + ge[:, 2 * w:3 * w] * ow_ref[...]
    y_a = _dot((o_a * _silu(az_ref[...])).astype(BF16), wua_ref[...])
    y_b = _dot((ob_ref[...] * _silu(bz_ref[...])).astype(BF16), wub_ref[...])
    mixed = _dot((_sigmoid(ma_ref[...]) * y_a + _sigmoid(mb_ref[...]) * y_b).astype(BF16), wo_ref[...])
    z = alpha * x_ref[...] + gate_ref[...] * mixed
    mu = jnp.mean(z, axis=-1, keepdims=True)
    var = jnp.mean(jnp.square(z - mu), axis=-1, keepdims=True)
    y_ref[...] = (z - mu) * lax.rsqrt(var + LN_EPS) * lng_ref[...] + lnb_ref[...]


def _outproj(x, gate, z, o_c, o_s, o_w, o_b, eg, wua, wub, wo, ln_g, ln_b, alpha):
    bx, t, d = x.shape
    tm = min(t, 512)
    nt = t // tm

    def tok(width, col):
        return pl.BlockSpec((None, tm, width), lambda m: (m // nt, m % nt, col))

    return pl.pallas_call(
        functools.partial(_outproj_kernel, alpha=alpha),
        grid=(bx * nt,),
        in_specs=[tok(d, 0), _mod_spec(gate, tm, nt),
                  tok(NSA_W, Z_AZ // NSA_W), tok(MOBA_W, Z_BZ // MOBA_W), tok(d, Z_MA // d), tok(d, Z_MB // d),
                  tok(AG_PAD, Z_AG // AG_PAD), tok(NSA_W, 0), tok(NSA_W, 0), tok(NSA_W, 0), tok(MOBA_W, 0),
                  _full_spec(eg, 1), _full_spec(wua, 1), _full_spec(wub, 1), _full_spec(wo, 1),
                  _full_spec(ln_g, 1), _full_spec(ln_b, 1)],
        out_specs=tok(d, 0),
        out_shape=jax.ShapeDtypeStruct((bx, t, d), F32),
        compiler_params=_cparams(1, 48),
        name="out_proj",
    )(x, gate, z, z, z, z, z, o_c, o_s, o_w, o_b, eg, wua, wub, wo, ln_g, ln_b)


def _dec_attend(s, v_t, s_self, v_self):
    m = jnp.maximum(jnp.max(s, axis=1, keepdims=True), s_self)
    e = jnp.exp2(s - m)
    e_self = jnp.exp2(s_self - m)
    den = jnp.maximum(jnp.sum(e, axis=1, keepdims=True) + e_self, 1e-30)
    return (_dot_nt(e.astype(BF16), v_t) + e_self * v_self) / den


def _dec_cmp_kernel(tab_ref, q_ref, ck_ref, cv_ref, ov_ref, oc_ref, imp_ref, *, pos):
    ncp = ck_ref.shape[2]
    jl = ov_ref.shape[1]
    cend = lax.broadcasted_iota(jnp.int32, (SUBLANES, ncp), 1) * CMP_STRIDE + (CMP_LEN - 1)
    dist = pos - cend
    valid = dist >= 0
    bkt = _bucket(dist)
    j = lax.broadcasted_iota(jnp.int32, (SUBLANES, jl), 1)
    cur = pos // SLC_BLOCK
    for g in range(NSA_KV):
        s = jnp.where(valid, _dot(q_ref[g], ck_ref[g]) + _bias_rows(bkt, tab_ref[g]), NEG_INF)
        m = jnp.max(s, axis=1, keepdims=True)
        m = jnp.where(m == NEG_INF, 0.0, m)
        e = jnp.exp2(s - m)
        p = (e / jnp.maximum(jnp.sum(e, axis=1, keepdims=True), 1e-30)).astype(BF16)
        oc_ref[g] = _dot_nt(p, cv_ref[g])
        imp4 = _dot(p, ov_ref[...])
        imp = imp4[0:1]
        for r in range(1, NSA_REP):
            imp = imp + imp4[r:r + 1]
        imp = jnp.broadcast_to(imp, (SUBLANES, jl))
        imp = jnp.where((j == 0) | (j == cur) | (j == cur - 1), jnp.inf, imp)
        imp_ref[g] = jnp.where(j <= cur, imp, NEG_INF)


def _dec_cmp(tab_g, q8, ck_t, cv_t, overlap, pos):
    db = q8.shape[0]
    ncp = ck_t.shape[3]
    jl = overlap.shape[1]
    cspec = pl.BlockSpec((None, NSA_KV, HEAD_DIM, ncp), lambda b: (b, 0, 0, 0))
    return pl.pallas_call(
        functools.partial(_dec_cmp_kernel, pos=pos),
        grid=(db,),
        in_specs=[_full_spec(tab_g, 1),
                  pl.BlockSpec((None, NSA_KV, SUBLANES, HEAD_DIM), lambda b: (b, 0, 0, 0)),
                  cspec, cspec, _full_spec(overlap, 1)],
        out_specs=[pl.BlockSpec((None, NSA_KV, SUBLANES, HEAD_DIM), lambda b: (b, 0, 0, 0)),
                   pl.BlockSpec((None, NSA_KV, SUBLANES, jl), lambda b: (b, 0, 0, 0))],
        out_shape=[jax.ShapeDtypeStruct((db, NSA_KV, SUBLANES, HEAD_DIM), F32),
                   jax.ShapeDtypeStruct((db, NSA_KV, SUBLANES, jl), F32)],
        compiler_params=_cparams(1),
        name="dec_nsa_cmp",
    )(tab_g, q8, ck_t, cv_t, overlap)


def _topk_idx_kernel(s_ref, idx_ref, *, n, k):
    st = s_ref[...].T
    nb = -(-n // SUBLANES)
    blocks = [st[rb * SUBLANES:(rb + 1) * SUBLANES, :] for rb in range(nb)]
    cnt = _rank_rows(blocks, n)
    sub = lax.broadcasted_iota(jnp.int32, blocks[0].shape, 0)
    rows = []
    for r in range(idx_ref.shape[0]):
        if r >= k:
            rows.append(jnp.full((1, st.shape[1]), -1, jnp.int32))
            continue
        acc = jnp.zeros(blocks[0].shape, jnp.int32)
        for rb in range(nb):
            hit = jnp.where(cnt[rb] == float(r), jnp.where(blocks[rb] > NEG_INF, 1, 0), 0)
            acc = acc + hit * (sub + (rb * SUBLANES + 1))
        rows.append(jnp.sum(acc, axis=0, keepdims=True) - 1)
    idx_ref[...] = jnp.concatenate(rows, axis=0)


def _topk_idx(scores, n, k):
    nrow, jl = scores.shape
    kp = -(-k // SUBLANES) * SUBLANES
    return pl.pallas_call(
        functools.partial(_topk_idx_kernel, n=n, k=k),
        grid=(1,),
        in_specs=[pl.BlockSpec((nrow, jl), lambda i: (0, 0))],
        out_specs=pl.BlockSpec((kp, nrow), lambda i: (0, 0)),
        out_shape=jax.ShapeDtypeStruct((kp, nrow), jnp.int32),
        compiler_params=_cparams(1),
        name="topk_idx",
    )(scores)


def _dec_slc_kernel(sel_ref, pt_ref, *refs, pos, n_slc, k_sel):
    tab_ref, q_ref, kvn_ref, o_ref = refs[NSA_KV * 2 * k_sel:]
    b = pl.program_id(0)
    n_keys = k_sel * PAGE_SIZE
    lane = lax.broadcasted_iota(jnp.int32, (SUBLANES, n_keys), 1)
    half = PAGE_SIZE // SLC_BLOCK
    for g in range(NSA_KV):
        k_refs = refs[g * 2 * k_sel:g * 2 * k_sel + k_sel]
        v_refs = refs[g * 2 * k_sel + k_sel:(g + 1) * 2 * k_sel]
        kpos = jnp.zeros((SUBLANES, n_keys), jnp.int32)
        for k in range(k_sel):
            jk = sel_ref[(b * NSA_KV + g) * k_sel + k]
            ok = (jk >= 0) & (jk < n_slc - 1)
            in_blk = (lane % PAGE_SIZE) // SLC_BLOCK == jk % half
            here = jnp.where(in_blk, (jk // half) * PAGE_SIZE + lane % PAGE_SIZE, pos + 1)
            kpos = jnp.where(lane // PAGE_SIZE == k, jnp.where(ok, here, pos + 1), kpos)
        dist = pos - kpos
        tab = tab_ref[g]
        q = q_ref[g]
        k_t = jnp.concatenate([r[...] for r in k_refs], axis=1).astype(BF16)
        v_t = jnp.concatenate([r[...] for r in v_refs], axis=1).astype(BF16)
        s = jnp.where(dist >= 0, _dot(q, k_t) + _bias_rows(_bucket(dist), tab), NEG_INF)
        s_self = jnp.sum(q.astype(F32) * kvn_ref[g:g + 1, :], axis=1, keepdims=True) + tab[:, 0:1]
        o_ref[g] = _dec_attend(s, v_t, s_self, kvn_ref[NSA_KV + g:NSA_KV + g + 1, :])


def _dec_slc(sel_flat, pt_flat, cache_t, layer, tab_g, q8, kv_new, pos, n_slc, k_sel, n_pages):
    db = q8.shape[0]
    half = PAGE_SIZE // SLC_BLOCK

    def blk_spec(g, k, kv):
        def imap(b, sel, pt):
            j = jnp.clip(sel[(b * NSA_KV + g) * k_sel + k], 0, n_slc - 2)
            return (pt[b * n_pages + j // half], layer, kv, g, 0, 0)
        return pl.BlockSpec((None, None, None, None, HEAD_DIM, PAGE_SIZE), imap)

    qspec = pl.BlockSpec((None, NSA_KV, SUBLANES, HEAD_DIM), lambda b, sel, pt: (b, 0, 0, 0))
    return pl.pallas_call(
        functools.partial(_dec_slc_kernel, pos=pos, n_slc=n_slc, k_sel=k_sel),
        grid_spec=pltpu.PrefetchScalarGridSpec(
            num_scalar_prefetch=2,
            grid=(db,),
            in_specs=[blk_spec(g, k, kv) for g in range(NSA_KV) for kv in range(2) for k in range(k_sel)] + [
                _full_spec(tab_g, 3), qspec,
                pl.BlockSpec((None, 2 * NSA_KV, HEAD_DIM), lambda b, sel, pt: (b, 0, 0))],
            out_specs=qspec),
        out_shape=jax.ShapeDtypeStruct((db, NSA_KV, SUBLANES, HEAD_DIM), F32),
        compiler_params=_cparams(1),
        name="dec_nsa_slc",
    )(sel_flat, pt_flat, *([cache_t] * (NSA_KV * 2 * k_sel)), tab_g, q8, kv_new)


def _dec_win_kernel(k0_ref, k1_ref, v0_ref, v1_ref, tab_ref, q_ref, kvn_ref, o_ref):
    n = k0_ref.shape[1]
    dist = n - lax.broadcasted_iota(jnp.int32, (SUBLANES, n), 1)
    bkt = _bucket(dist)
    for g, (k_ref, v_ref) in enumerate(((k0_ref, v0_ref), (k1_ref, v1_ref))):
        tab = tab_ref[g]
        q = q_ref[g]
        s = jnp.where(dist < WINDOW, _dot(q, k_ref[...].astype(BF16)) + _bias_rows(bkt, tab), NEG_INF)
        s_self = jnp.sum(q.astype(F32) * kvn_ref[g:g + 1, :], axis=1, keepdims=True) + tab[:, 0:1]
        o_ref[g] = _dec_attend(s, v_ref[...].astype(BF16), s_self, kvn_ref[NSA_KV + g:NSA_KV + g + 1, :])


def _dec_win(state_t, layer, tab_g, q8, kv_new):
    db = q8.shape[0]
    n = state_t.shape[5]

    def st_spec(kv, g):
        return pl.BlockSpec((None, None, None, None, HEAD_DIM, n), lambda b: (layer, b, kv, g, 0, 0))

    return pl.pallas_call(
        _dec_win_kernel,
        grid=(db,),
        in_specs=[st_spec(0, 0), st_spec(0, 1), st_spec(1, 0), st_spec(1, 1),
                  _full_spec(tab_g, 1),
                  pl.BlockSpec((None, NSA_KV, SUBLANES, HEAD_DIM), lambda b: (b, 0, 0, 0)),
                  pl.BlockSpec((None, 2 * NSA_KV, HEAD_DIM), lambda b: (b, 0, 0))],
        out_specs=pl.BlockSpec((None, NSA_KV, SUBLANES, HEAD_DIM), lambda b: (b, 0, 0, 0)),
        out_shape=jax.ShapeDtypeStruct((db, NSA_KV, SUBLANES, HEAD_DIM), F32),
        compiler_params=_cparams(1),
        name="dec_nsa_win",
    )(state_t, state_t, state_t, state_t, tab_g, q8, kv_new)


def _dec_moba_sweep_kernel(pt_ref, *refs, n_x):
    k_refs = refs[:n_x]
    qb_ref, s_ref = refs[n_x:]
    qb = qb_ref[...]
    for k, k_ref in enumerate(k_refs):
        prod = k_ref[...] * qb
        s_ref[:, k * PAGE_SIZE:(k + 1) * PAGE_SIZE] = jnp.concatenate(
            [jnp.sum(prod[h * HEAD_DIM:(h + 1) * HEAD_DIM], axis=0, keepdims=True) for h in range(MOBA_HEADS)],
            axis=0)


def _dec_moba_sweep(cache_t, page_table, layer, q_lanes):
    db, n_pages = page_table.shape
    pps = PAGES_PER_STEP

    def page_spec(k):
        return pl.BlockSpec((None, None, None, MOBA_W, PAGE_SIZE),
                            lambda b, i, pt: (pt[b * n_pages + i * pps + k], layer, 0, 0, 0))

    return pl.pallas_call(
        functools.partial(_dec_moba_sweep_kernel, n_x=pps),
        grid_spec=pltpu.PrefetchScalarGridSpec(
            num_scalar_prefetch=1,
            grid=(db, n_pages // pps),
            in_specs=[page_spec(k) for k in range(pps)] + [
                pl.BlockSpec((None, MOBA_W, PAGE_SIZE), lambda b, i, pt: (b, 0, 0))],
            out_specs=pl.BlockSpec((None, MOBA_HEADS, pps * PAGE_SIZE), lambda b, i, pt: (b, 0, i))),
        out_shape=jax.ShapeDtypeStruct((db, MOBA_HEADS, n_pages * PAGE_SIZE), F32),
        compiler_params=_cparams(2, 40),
        name="dec_moba_sweep",
    )(page_table.reshape(-1), *([cache_t] * pps), q_lanes)


def _dec_moba_gate_kernel(s_ref, gs_ref, *, nblk):
    lane = lax.broadcasted_iota(jnp.int32, gs_ref.shape, 1)
    gs = jnp.full(gs_ref.shape, NEG_INF, F32)
    for blk in range(nblk):
        mean = jnp.sum(s_ref[:, blk * MOBA_BLOCK:(blk + 1) * MOBA_BLOCK], axis=1, keepdims=True) / MOBA_BLOCK
        gs = jnp.where(lane == blk, mean, gs)
    gs_ref[...] = gs


def _dec_moba_gate(s_all, nblk):
    db, _, p = s_all.shape
    return pl.pallas_call(
        functools.partial(_dec_moba_gate_kernel, nblk=nblk),
        grid=(db,),
        in_specs=[pl.BlockSpec((None, MOBA_HEADS, p), lambda b: (b, 0, 0))],
        out_specs=pl.BlockSpec((None, MOBA_HEADS, LANES), lambda b: (b, 0, 0)),
        out_shape=jax.ShapeDtypeStruct((db, MOBA_HEADS, LANES), F32),
        compiler_params=_cparams(1),
        name="dec_moba_gate",
    )(s_all)


def _dec_moba_attend_kernel(sel_ref, pt_ref, *refs, pos, k_m):
    ppb = MOBA_BLOCK // PAGE_SIZE
    per_h = k_m + k_m * ppb
    tab_ref, q_ref, kn_ref, vn_ref, o_ref = refs[MOBA_HEADS * per_h:]
    b = pl.program_id(0)
    n_keys = k_m * MOBA_BLOCK
    lane = lax.broadcasted_iota(jnp.int32, (SUBLANES, n_keys), 1)
    tab = tab_ref[...]
    s_self = jnp.sum(q_ref[...] * kn_ref[...], axis=1, keepdims=True) + tab[:, 0:1]
    outs = []
    for h in range(MOBA_HEADS):
        s_refs = refs[h * per_h:h * per_h + k_m]
        v_refs = refs[h * per_h + k_m:(h + 1) * per_h]
        kpos = jnp.zeros((SUBLANES, n_keys), jnp.int32)
        for k in range(k_m):
            jk = sel_ref[(b * MOBA_HEADS + h) * k_m + k]
            kpos = jnp.where(lane // MOBA_BLOCK == k,
                             jnp.where(jk >= 0, jk * MOBA_BLOCK + lane % MOBA_BLOCK, pos + 1), kpos)
        dist = pos - kpos
        tab_h = jnp.broadcast_to(tab[h:h + 1], (SUBLANES, N_BUCKETS))
        s = jnp.broadcast_to(jnp.concatenate([r[h:h + 1, :] for r in s_refs], axis=1), (SUBLANES, n_keys))
        s = jnp.where(dist >= 0, s + _bias_rows(_bucket(dist), tab_h), NEG_INF)
        v_t = jnp.concatenate([r[...] for r in v_refs], axis=1).astype(BF16)
        o = _dec_attend(s, v_t, jnp.broadcast_to(s_self[h:h + 1], (SUBLANES, 1)), vn_ref[h:h + 1, :])
        outs.append(o[0:1])
    o_ref[...] = jnp.concatenate(outs, axis=0)


def _dec_moba_attend(sel_flat, pt_flat, s_all, cache_t, layer, tab_h, q, k_new, v_new, pos, k_m, n_pages, nblk):
    db = q.shape[0]
    ppb = MOBA_BLOCK // PAGE_SIZE

    def sel_of(b, h, sel, k):
        return jnp.clip(sel[(b * MOBA_HEADS + h) * k_m + k], 0, nblk - 1)

    def s_spec(h, k):
        return pl.BlockSpec((None, MOBA_HEADS, MOBA_BLOCK), lambda b, sel, pt: (b, 0, sel_of(b, h, sel, k)))

    def v_spec(h, k, pg):
        return pl.BlockSpec(
            (None, None, None, None, HEAD_DIM, PAGE_SIZE),
            lambda b, sel, pt: (pt[b * n_pages + sel_of(b, h, sel, k) * ppb + pg], layer, 1, h, 0, 0))

    head_specs, head_args = [], []
    for h in range(MOBA_HEADS):
        head_specs += [s_spec(h, k) for k in range(k_m)] + [v_spec(h, k, pg) for k in range(k_m) for pg in range(ppb)]
        head_args += [s_all] * k_m + [cache_t] * (k_m * ppb)
    row_spec = pl.BlockSpec((None, MOBA_HEADS, HEAD_DIM), lambda b, sel, pt: (b, 0, 0))
    return pl.pallas_call(
        functools.partial(_dec_moba_attend_kernel, pos=pos, k_m=k_m),
        grid_spec=pltpu.PrefetchScalarGridSpec(
            num_scalar_prefetch=2,
            grid=(db,),
            in_specs=head_specs + [_full_spec(tab_h, 3), row_spec, row_spec, row_spec],
            out_specs=row_spec),
        out_shape=jax.ShapeDtypeStruct((db, MOBA_HEADS, HEAD_DIM), F32),
        compiler_params=_cparams(1),
        name="dec_moba_attend",
    )(sel_flat, pt_flat, *head_args, tab_h, q, k_new, v_new)


def _overlap_matrix(n_cmp, n_slc, rows, cols):
    i = np.arange(n_cmp)[:, None]
    j = np.arange(n_slc)[None, :]
    units = SLC_BLOCK // CMP_STRIDE
    m = sum(((i + u) // units == j).astype(np.float32) for u in range(CMP_LEN // CMP_STRIDE))
    out = np.zeros((rows, cols), np.float32)
    out[:n_cmp, :n_slc] = m
    return jnp.asarray(out, dtype=BF16)


def _token_group_permutation():
    groups = LANES // CMP_STRIDE
    p = np.zeros((2 * LANES, 2 * LANES), np.float32)
    for c in range(2):
        for l in range(CMP_STRIDE):
            for m in range(groups):
                p[c * LANES + l * groups + m, c * LANES + CMP_STRIDE * m + l] = 1.0
    return jnp.asarray(p, dtype=BF16)


def _gate_expand_matrix():
    e = np.zeros((AG_PAD, 3 * NSA_W), np.float32)
    for h in range(NSA_HEADS):
        for br in range(3):
            e[h * 3 + br, br * NSA_W + h * HEAD_DIM: br * NSA_W + (h + 1) * HEAD_DIM] = 1.0
    return jnp.asarray(e, dtype=BF16)


def _split_w_in(w_in):
    scale = HEAD_DIM ** -0.5 * LOG2E
    o = np.cumsum([0, 512, 128, 128, 128, 128, 128, 128, 24, 512, 512, 512, 512, 512, 1024, 1024])
    a_q, kv3, a_g, a_z, b_q, b_kv, b_z, m_ab = (
        w_in[..., o[0]:o[1]], w_in[..., o[1]:o[7]], w_in[..., o[7]:o[8]], w_in[..., o[8]:o[9]],
        w_in[..., o[9]:o[10]], w_in[..., o[10]:o[12]], w_in[..., o[12]:o[13]], w_in[..., o[13]:o[15]])
    pad = jnp.zeros(w_in.shape[:-1] + (AG_PAD - a_g.shape[-1],), w_in.dtype)
    k_rm = jnp.concatenate([w_in[..., o[10]:o[11]], w_in[..., o[3]:o[4]], w_in[..., o[5]:o[6]]], axis=-1)
    wq = jnp.concatenate([a_q * scale, b_q * scale, k_rm], axis=-1).astype(BF16)
    wz = jnp.concatenate([a_z, b_z, m_ab, a_g, pad], axis=-1).astype(BF16)
    wkv = jnp.swapaxes(jnp.concatenate([kv3, b_kv], axis=-1), 1, 2).astype(BF16)
    return wq, wz, wkv


def _cmp_stage1_weights(w1):
    w1r = w1.reshape(2, CMP_STRIDE, HEAD_DIM, HEAD_DIM)
    eye = jnp.eye(NSA_KV, dtype=w1.dtype)
    w = jnp.einsum("hlde,gf->lgdhfe", w1r, eye)
    return w.reshape(CMP_STRIDE * NSA_KV * HEAD_DIM, 2 * NSA_KV * HEAD_DIM).astype(BF16)


def _cmp_stage2_weights(w2):
    w2t = jnp.swapaxes(w2, 1, 2)
    z = jnp.zeros_like(w2t)
    return jnp.stack([jnp.concatenate([w2t, z], axis=2), jnp.concatenate([z, w2t], axis=2)], axis=1).astype(BF16)


def _cache_view(c):
    return jnp.transpose(c, (0, 1, 3, 4, 5, 2))


def _kv_output(kv_t, heads):
    b, depth, _, t = kv_t.shape
    return jnp.transpose(kv_t.reshape(b, depth, 2, heads, HEAD_DIM, t), (0, 1, 5, 2, 3, 4))


def _layer_prompt(x, mod, lw, consts, layer, depth, kv_prev):
    b, t, d = x.shape
    shift, scale, gate = mod
    q, z, cmp_t, slc_t, win_t, moba_t = _inproj(x, scale, shift, lw["wq"], lw["wz"], lw["wkv"], layer, depth,
                                                kv_prev)
    n_slc = t // SLC_BLOCK
    k_sel = min(SLC_TOPK, n_slc)
    abk, abv = _cmp_proj_prompt(cmp_t, layer, consts["perm"], lw["cmp_wk"], lw["cmp_wv"])
    _, cv_t, ck = _cmp_mlp(abk, abv, lw["pos_flat"], lw["phi_w1"], lw["phi_b1"], lw["cmp_w2t"], lw["phi_b2"])
    o_c, dn = _nsa_cmp_prompt(consts["rel_bias"], q, ck, cv_t, consts["overlap_p"], n_slc, k_sel)
    o_s = _nsa_slc_prompt(q, slc_t, layer, dn, consts["btiles"])
    o_w = _nsa_win_prompt(q, win_t, layer, consts["btiles"])
    nblk = t // MOBA_BLOCK
    o_b = _moba_prompt(q, moba_t, layer, _kmean_prompt(moba_t, layer), consts["btiles"], min(MOBA_TOPK, nblk - 1))
    y = _outproj(x, gate, z, o_c, o_s, o_w, o_b, consts["eg"], lw["w_up_a"], lw["w_up_b"], lw["w_out"],
                 lw["ln_g"], lw["ln_b"], consts["alpha"])
    return y, (cmp_t, slc_t, win_t, moba_t)


def _layer_sample(x, mod, lw, consts, layer, caches, page_table):
    _, db, d = x.shape
    shift, scale, gate = mod
    cache_cmp, cache_slc, cache_moba, state_win = caches
    n_phys, depth = cache_cmp.shape[:2]
    n_pages = page_table.shape[1]
    pos = n_pages * PAGE_SIZE
    pt_flat = page_table.reshape(-1)
    q, z, cmp_t, slc_t, win_t, moba_t = _inproj(x, scale, shift, lw["wq"], lw["wz"], lw["wkv"], 0, 1)
    cmp_n, slc_n, win_n = (a[0, 0].T.reshape(db, 2 * NSA_KV, HEAD_DIM) for a in (cmp_t, slc_t, win_t))
    moba_n = moba_t[0, 0].T.reshape(db, 2, MOBA_HEADS, HEAD_DIM)
    qa = jnp.transpose(q[0, H_QA:H_QA + NSA_HEADS], (1, 0, 2)).reshape(db, NSA_KV, NSA_REP, HEAD_DIM)
    q8 = jnp.pad(qa, ((0, 0), (0, 0), (0, SUBLANES - NSA_REP), (0, 0)))
    qb = jnp.transpose(q[0, H_QB:H_QB + MOBA_HEADS], (1, 0, 2)).astype(F32)
    abk, abv = _cmp_proj_paged(cache_cmp.reshape(n_phys, depth, KV_W, PAGE_SIZE), page_table, layer,
                               consts["perm"], lw["cmp_wk"], lw["cmp_wv"])
    ck_t, cv_t, _ = _cmp_mlp(abk, abv, lw["pos_flat"], lw["phi_w1"], lw["phi_b1"], lw["cmp_w2t"], lw["phi_b2"])
    n_slc = pos // SLC_BLOCK + 1
    k_sel = min(SLC_TOPK, n_slc)
    o_c8, imp = _dec_cmp(consts["tab_g"], q8, ck_t, cv_t, consts["overlap_s"], pos)
    imp2 = imp[:, :, 0].reshape(db * NSA_KV, -1)
    imp2 = jnp.pad(imp2, ((0, LANES - db * NSA_KV), (0, 0)), constant_values=NEG_INF)
    sel = _topk_idx(imp2, n_slc, k_sel)[:k_sel, :db * NSA_KV].T.reshape(-1)
    o_s8 = _dec_slc(sel, pt_flat, cache_slc, layer, consts["tab_g"], q8, slc_n, pos, n_slc, k_sel, n_pages)
    o_w8 = _dec_win(state_win, layer, consts["tab_g"], q8, win_n)
    o_c, o_s, o_w = (a[:, :, :NSA_REP].reshape(1, db, NSA_W) for a in (o_c8, o_s8, o_w8))
    nblk = pos // MOBA_BLOCK
    k_m = min(MOBA_TOPK, nblk)
    q_lanes = jnp.broadcast_to(qb.reshape(db, MOBA_W, 1), (db, MOBA_W, PAGE_SIZE))
    s_all = _dec_moba_sweep(cache_moba.reshape(n_phys, depth, 2, MOBA_W, PAGE_SIZE), page_table, layer, q_lanes)
    gs = _dec_moba_gate(s_all, nblk).reshape(db * MOBA_HEADS, LANES)
    sel_m = _topk_idx(gs, nblk, k_m)[:k_m].T.reshape(-1)
    o_b8 = _dec_moba_attend(sel_m, pt_flat, s_all, cache_moba, layer, consts["tab_h"], qb, moba_n[:, 0],
                            moba_n[:, 1], pos, k_m, n_pages, nblk)
    o_b = o_b8.reshape(1, db, MOBA_W)
    y = _outproj(x, gate, z, o_c, o_s, o_w, o_b, consts["eg"], lw["w_up_a"], lw["w_up_b"], lw["w_out"],
                 lw["ln_g"], lw["ln_b"], consts["alpha"])
    new = dict(cmp=cmp_n.reshape(db, 1, 2, NSA_KV, HEAD_DIM), slc=slc_n.reshape(db, 1, 2, NSA_KV, HEAD_DIM),
               win=win_n.reshape(db, 1, 2, NSA_KV, HEAD_DIM), moba=moba_n.reshape(db, 1, 2, MOBA_HEADS, HEAD_DIM))
    return y, new


def kernel(x_prompt, x_sample, cache_nsa_cmp, cache_nsa_slc, cache_moba, state_nsa_win, page_table, c_prompt, c_sample, rel_bias, w_ada, b_ada, w_in, phi_pos, phi_w1, phi_b1, phi_w2, phi_b2, w_up_a, w_up_b, w_out, ln_g, ln_b):
    b, t, d = x_prompt.shape
    db = x_sample.shape[0]
    depth = w_ada.shape[0]
    n_pages = page_table.shape[1]
    pos = n_pages * PAGE_SIZE
    assert x_sample.shape[1] == 1 and t % TQ == 0 and t >= WINDOW and n_pages % PAGES_PER_STEP == 0
    assert db * NSA_KV <= LANES and state_nsa_win.shape[2] == WINDOW and pos // MOBA_BLOCK >= 1
    assert t // MOBA_BLOCK <= LANES and t // SLC_BLOCK <= LANES and t // CMP_STRIDE >= CMP_NEAR

    mc = -(-(b + db) // SUBLANES) * SUBLANES
    c_all = jnp.pad(jnp.concatenate([c_prompt, c_sample], axis=0), ((0, mc - b - db), (0, 0)))
    mod = _ada(c_all, w_ada, b_ada)

    wq, wz, wkv = _split_w_in(w_in)
    rel_bias = rel_bias * LOG2E
    tab_rel = (rel_bias - rel_bias[N_BUCKETS - 1][None, :]).T
    tab_g = jnp.pad(tab_rel[:NSA_HEADS].reshape(NSA_KV, NSA_REP, N_BUCKETS),
                    ((0, 0), (0, SUBLANES - NSA_REP), (0, 0)))
    n_slc_s = pos // SLC_BLOCK + 1
    consts = dict(
        rel_bias=rel_bias,
        alpha=float((2 * depth) ** 0.25),
        btiles=_bias_tiles(rel_bias),
        perm=_token_group_permutation(),
        overlap_p=_overlap_matrix(t // CMP_STRIDE - 1, t // SLC_BLOCK, t // CMP_STRIDE, LANES).T,
        overlap_s=_overlap_matrix(pos // CMP_STRIDE - 1, n_slc_s, pos // CMP_STRIDE, -(-n_slc_s // LANES) * LANES),
        eg=_gate_expand_matrix(),
        tab_g=tab_g,
        tab_h=tab_rel[NSA_HEADS:],
    )
    caches = (_cache_view(cache_nsa_cmp), _cache_view(cache_nsa_slc), _cache_view(cache_moba),
              jnp.transpose(state_nsa_win, (0, 1, 3, 4, 5, 2)))

    yp, ys = x_prompt, x_sample.reshape(1, db, d)
    kv_p, new_s = None, []
    for l in range(depth):
        lw = dict(
            wq=wq[l], wz=wz[l], wkv=wkv[l],
            cmp_wk=_cmp_stage1_weights(phi_w1[l, 0]), cmp_wv=_cmp_stage1_weights(phi_w1[l, 1]),
            pos_flat=phi_pos[l].reshape(2, 1, CMP_LEN * HEAD_DIM),
            phi_w1=phi_w1[l], phi_b1=phi_b1[l].reshape(2, 1, HEAD_DIM),
            cmp_w2t=_cmp_stage2_weights(phi_w2[l]), phi_b2=phi_b2[l].reshape(2, HEAD_DIM, 1),
            w_up_a=w_up_a[l].astype(BF16), w_up_b=w_up_b[l].astype(BF16), w_out=w_out[l].astype(BF16),
            ln_g=ln_g[l].reshape(1, d), ln_b=ln_b[l].reshape(1, d))
        shift, scale, gate = jnp.split(mod[l], 3, axis=-1)
        mod_p = tuple(a[:b, None, :] for a in (shift, scale, gate))
        mod_s = tuple(a[None, b:b + db, :] for a in (shift, scale, gate))
        yp, kv_p = _layer_prompt(yp, mod_p, lw, consts, l, depth, kv_p)
        ys, ns_ = _layer_sample(ys, mod_s, lw, consts, l, caches, page_table)
        new_s.append(ns_)

    def stack_s(key):
        return jnp.stack([it[key] for it in new_s], axis=1)

    cmp_p, slc_p, win_p, moba_p = kv_p
    win_p = jnp.moveaxis(_kv_output(win_p[..., t - min(WINDOW, t):], NSA_KV), 1, 0)
    return (yp, ys.reshape(db, 1, d),
            _kv_output(cmp_p, NSA_KV), stack_s("cmp"),
            _kv_output(slc_p, NSA_KV), stack_s("slc"),
            _kv_output(moba_p, MOBA_HEADS), stack_s("moba"),
            win_p, jnp.moveaxis(stack_s("win"), 1, 0))
```

```python
import functools
import math

import numpy as np
import jax
import jax.numpy as jnp
from jax import lax
from jax.experimental import pallas as pl
from jax.experimental.pallas import tpu as pltpu

F32 = jnp.float32
BF16 = jnp.bfloat16
NEG_INF = float("-inf")
MASK_BIG = 2.0 ** 127
LOG2E = math.log2(math.e)

HEAD_DIM = 64
NSA_HEADS = 8
NSA_KV = 2
NSA_REP = NSA_HEADS // NSA_KV
CMP_LEN = 32
CMP_STRIDE = 16
SLC_BLOCK = 64
SLC_TOPK = 16
WINDOW = 512
MOBA_HEADS = 8
MOBA_BLOCK = 256
MOBA_TOPK = 3
N_HEADS = NSA_HEADS + MOBA_HEADS
N_BUCKETS = 32
MAX_EXACT = N_BUCKETS // 2
MAX_DISTANCE = 128
LN_EPS = 1e-5
PAGE_SIZE = 128

LANES = 128
SUBLANES = 8
TQ = 256
MOBA_HB = 8
CMP_NEAR = 40
H_QA, H_QB, H_KB, H_KS, H_KW = 0, 8, 16, 24, 26
N_QK = 28
PAGES_PER_STEP = 16

NSA_W = NSA_HEADS * HEAD_DIM
MOBA_W = MOBA_HEADS * HEAD_DIM
KV_W = 2 * NSA_KV * HEAD_DIM
MOBA_KV_W = 2 * MOBA_W
Z_AZ, Z_BZ, Z_MA, Z_MB, Z_AG = 0, 512, 1024, 2048, 3072
AG_PAD = 128
Z_W = Z_AG + AG_PAD


def _cparams(n_axes, vmem_mb=None):
    kw = dict(dimension_semantics=("arbitrary",) * n_axes)
    if vmem_mb is not None:
        kw["vmem_limit_bytes"] = vmem_mb * 1024 * 1024
    return pltpu.CompilerParams(**kw)


def _dot(a, b):
    return jnp.dot(a, b, preferred_element_type=F32)


def _dot_nt(a, b):
    return lax.dot_general(a, b, (((1,), (1,)), ((), ())), preferred_element_type=F32)


def _sigmoid(x):
    return 1.0 / (1.0 + jnp.exp(-x))


def _silu(x):
    return x * _sigmoid(x)


def _bucket(dist):
    n = jnp.maximum(dist, 0)
    nf = jnp.maximum(n, 1).astype(F32)
    large = MAX_EXACT + (jnp.log(nf / MAX_EXACT) / math.log(MAX_DISTANCE / MAX_EXACT)
                         * (N_BUCKETS - MAX_EXACT)).astype(jnp.int32)
    return jnp.where(n < MAX_EXACT, n, jnp.minimum(large, N_BUCKETS - 1))


def _bias_rows(bkt, tab):
    bias = jnp.zeros(bkt.shape, F32)
    for b in range(N_BUCKETS):
        bias = jnp.where(bkt == b, tab[:, b:b + 1], bias)
    return bias


def _rank_rows(blocks, n):
    nb = len(blocks)
    cnt = [jnp.zeros(blocks[0].shape, F32) for _ in range(nb)]
    sub = lax.broadcasted_iota(jnp.int32, blocks[0].shape, 0)
    for jp in range(n):
        rb0, r0 = divmod(jp, SUBLANES)
        row = blocks[rb0][r0:r0 + 1, :]
        for rb in range(nb):
            a = blocks[rb]
            if rb < rb0:
                ahead = jnp.where(row > a, 1.0, 0.0)
            elif rb > rb0:
                ahead = jnp.where(row >= a, 1.0, 0.0)
            else:
                ahead = jnp.where(sub > r0, jnp.where(row >= a, 1.0, 0.0), jnp.where(row > a, 1.0, 0.0))
            cnt[rb] = cnt[rb] + ahead
    return cnt


def _topk_drop_rows(score_t, n, k):
    jl, q = score_t.shape
    nb = -(-n // SUBLANES)
    blocks = [score_t[rb * SUBLANES:(rb + 1) * SUBLANES, :] for rb in range(nb)]
    cnt = _rank_rows(blocks, n)
    drop = [jnp.where(c < k, jnp.where(a > NEG_INF, 0.0, -MASK_BIG), -MASK_BIG) for c, a in zip(cnt, blocks)]
    if nb * SUBLANES < jl:
        drop.append(jnp.full((jl - nb * SUBLANES, q), -MASK_BIG, F32))
    return jnp.concatenate(drop, axis=0)


def _softmax_init(m_scr, l_scr, acc_scr):
    m_scr[...] = jnp.full(m_scr.shape, -MASK_BIG, F32)
    l_scr[...] = jnp.zeros(l_scr.shape, F32)
    acc_scr[...] = jnp.zeros(acc_scr.shape, F32)


def _flash_tile(qs, ks, v_ts, states, adds, masks=None):
    n_s = len(qs)
    ss = [_dot_nt(ks[c], qs[c]) for c in range(n_s)]
    ps, alphas = [], []
    for c in range(n_s):
        s = ss[c]
        for a in adds[c]:
            s = s + a
        m_scr, l_scr, _ = states[c]
        m_prev = m_scr[...]
        if masks is None:
            m_next = jnp.maximum(m_prev, jnp.max(s, axis=0, keepdims=True))
            p = jnp.exp2(s - m_next)
        else:
            rows = s.shape[0] // len(masks[c])
            slabs = [s[j * rows:(j + 1) * rows] for j in range(len(masks[c]))]
            tops = [jnp.max(sl, axis=0, keepdims=True) for sl in slabs]
            m_next = m_prev
            for top, mask in zip(tops, masks[c]):
                m_next = jnp.maximum(m_next, top + mask)
            p = jnp.concatenate([jnp.exp2(sl - jnp.maximum(m_next - mask, top))
                                 for sl, top, mask in zip(slabs, tops, masks[c])], axis=0)
        alpha = jnp.exp2(m_prev - m_next)
        l_scr[...] = alpha * l_scr[...] + jnp.sum(p, axis=0, keepdims=True)
        m_scr[...] = m_next
        ps.append(p.astype(BF16))
        alphas.append(alpha)
    for c in range(n_s):
        acc_scr = states[c][2]
        acc_scr[...] = acc_scr[...] * alphas[c] + _dot(v_ts[c], ps[c])


def _far_tiles(n, tile):
    def pair(j, c):
        tile(2 * j)
        tile(2 * j + 1)
        return c

    lax.fori_loop(0, n // 2, pair, 0)

    @pl.when(n % 2 == 1)
    def _():
        tile(n - 1)


def _flash_scratch(n_streams):
    per = [pltpu.VMEM((1, TQ), F32), pltpu.VMEM((1, TQ), F32), pltpu.VMEM((HEAD_DIM, TQ), F32)]
    return per * n_streams


def _flash_states(scr):
    return [tuple(scr[3 * c:3 * c + 3]) for c in range(len(scr) // 3)]


def _softmax_finish(l_scr, acc_scr):
    return (acc_scr[...] / jnp.maximum(l_scr[...], 1e-30)).T


def _full_spec(a, n_grid, single=True):
    kw = dict(pipeline_mode=pl.Buffered(1)) if single else {}
    return pl.BlockSpec(a.shape, lambda *_: (0,) * a.ndim, **kw)


def _ada_kernel(c_ref, w_ref, b_ref, o_ref):
    a = _silu(c_ref[...]).astype(BF16)
    o_ref[...] = _dot(a, w_ref[...].astype(BF16)) + b_ref[...]


def _ada(c_all, w_ada, b_ada):
    depth, d, n3 = w_ada.shape
    mc = c_all.shape[0]
    tn = 1024
    return pl.pallas_call(
        _ada_kernel,
        grid=(depth, n3 // tn),
        in_specs=[pl.BlockSpec((mc, d), lambda l, j: (0, 0)),
                  pl.BlockSpec((None, d, tn), lambda l, j: (l, 0, j)),
                  pl.BlockSpec((None, 1, tn), lambda l, j: (l, 0, j))],
        out_specs=pl.BlockSpec((None, mc, tn), lambda l, j: (l, 0, j)),
        out_shape=jax.ShapeDtypeStruct((depth, mc, n3), F32),
        compiler_params=_cparams(2, 40),
        name="ada_mod",
    )(c_all, w_ada, b_ada.reshape(depth, 1, n3))


def _inproj_kernel(x_ref, sc_ref, sh_ref, wq_ref, wz_ref, wkv_ref, *refs):
    q_ref, z_ref, cmp_ref, slc_ref, win_ref, moba_ref = refs[-6:]
    h = (x_ref[...] * (1.0 + sc_ref[...]) + sh_ref[...]).astype(BF16)
    q = _dot(h, wq_ref[...])
    for hd in range(N_QK):
        q_ref[hd] = q[:, hd * HEAD_DIM:(hd + 1) * HEAD_DIM].astype(BF16)
    z_ref[...] = _dot(h, wz_ref[...])
    r = 0
    for o_ref in (cmp_ref, slc_ref, win_ref, moba_ref):
        n = o_ref.shape[0]
        o_ref[...] = _dot_nt(wkv_ref[r:r + n, :], h)
        r += n


def _mod_spec(mod, tm, nt):
    if mod.shape[1] == 1:
        return pl.BlockSpec((None, 1, mod.shape[2]), lambda m: (m // nt, 0, 0))
    return pl.BlockSpec((None, tm, mod.shape[2]), lambda m: (m // nt, m % nt, 0))


def _inproj(x, scale, shift, wq, wz, wkv, layer, depth, kv_prev=None):
    bx, t, d = x.shape
    tm = min(t, 512)
    nt = t // tm

    def kv_spec(rows):
        return pl.BlockSpec((None, None, rows, tm), lambda m: (m // nt, layer, 0, m % nt))

    def kv_shape(rows):
        return jax.ShapeDtypeStruct((bx, depth, rows, t), F32)

    prev = () if kv_prev is None else tuple(kv_prev)
    n_in = 6
    return pl.pallas_call(
        _inproj_kernel,
        grid=(bx * nt,),
        in_specs=[pl.BlockSpec((None, tm, d), lambda m: (m // nt, m % nt, 0)),
                  _mod_spec(scale, tm, nt), _mod_spec(shift, tm, nt),
                  _full_spec(wq, 1), _full_spec(wz, 1), _full_spec(wkv, 1)] + [
                      pl.BlockSpec(memory_space=pl.ANY) for _ in prev],
        out_specs=[pl.BlockSpec((None, N_QK, tm, HEAD_DIM), lambda m: (m // nt, 0, m % nt, 0)),
                   pl.BlockSpec((None, tm, Z_W), lambda m: (m // nt, m % nt, 0)),
                   kv_spec(KV_W), kv_spec(KV_W), kv_spec(KV_W), kv_spec(MOBA_KV_W)],
        out_shape=[jax.ShapeDtypeStruct((bx, N_QK, t, HEAD_DIM), BF16),
                   jax.ShapeDtypeStruct((bx, t, Z_W), F32),
                   kv_shape(KV_W), kv_shape(KV_W), kv_shape(KV_W), kv_shape(MOBA_KV_W)],
        input_output_aliases={n_in + k: 2 + k for k in range(len(prev))},
        compiler_params=_cparams(1, 56),
        name="in_proj",
    )(x, scale, shift, wq, wz, wkv, *prev)


def _cmp_proj_kernel(*refs, n_x, n_prefetch=0):
    refs = refs[n_prefetch:]
    perm_ref, wk_ref, wv_ref, abk_ref, abv_ref = refs[n_x:]
    perm = perm_ref[...]
    chunks = [x_ref[:, c * LANES:(c + 1) * LANES].astype(BF16)
              for x_ref in refs[:n_x] for c in range(x_ref.shape[1] // LANES)]
    groups = LANES // CMP_STRIDE
    rows_k, rows_v = [], []
    for c0 in range(0, len(chunks), 2):
        pair = chunks[c0:c0 + 2]
        xp = _dot_nt(perm[:len(pair) * LANES, :len(pair) * LANES], jnp.concatenate(pair, axis=1))
        for ci in range(len(pair)):
            parts = [xp[ci * LANES + l * groups:ci * LANES + (l + 1) * groups] for l in range(CMP_STRIDE)]
            rows_k.append(jnp.concatenate([p[:, :KV_W // 2] for p in parts], axis=1))
            rows_v.append(jnp.concatenate([p[:, KV_W // 2:] for p in parts], axis=1))
    for rows, w_ref, ab_ref in ((rows_k, wk_ref, abk_ref), (rows_v, wv_ref, abv_ref)):
        xr = jnp.concatenate(rows, axis=0).astype(BF16)
        ab_ref[...] = _dot(xr, w_ref[...])


def _cmp_proj_prompt(cmp_t, layer, perm, wk, wv):
    bx, _, _, t = cmp_t.shape
    tc = min(t, PAGES_PER_STEP * LANES)
    m = tc // CMP_STRIDE
    ospec = pl.BlockSpec((None, m, 256), lambda b, i: (b, i, 0))
    oshape = jax.ShapeDtypeStruct((bx, t // CMP_STRIDE, 256), F32)
    return pl.pallas_call(
        functools.partial(_cmp_proj_kernel, n_x=1),
        grid=(bx, t // tc),
        in_specs=[pl.BlockSpec((None, None, KV_W, tc), lambda b, i: (b, layer, 0, i)),
                  _full_spec(perm, 2), _full_spec(wk, 2), _full_spec(wv, 2)],
        out_specs=[ospec, ospec],
        out_shape=[oshape, oshape],
        compiler_params=_cparams(2),
        name="cmp_proj_prompt",
    )(cmp_t, perm, wk, wv)


def _cmp_proj_paged(cache_t, page_table, layer, perm, wk, wv):
    db, n_pages = page_table.shape
    pps = PAGES_PER_STEP
    m = PAGE_SIZE // CMP_STRIDE

    def page_spec(k):
        return pl.BlockSpec((None, None, KV_W, PAGE_SIZE),
                            lambda b, i, pt: (pt[b * n_pages + i * pps + k], layer, 0, 0))

    ospec = pl.BlockSpec((None, pps * m, 256), lambda b, i, pt: (b, i, 0))
    oshape = jax.ShapeDtypeStruct((db, n_pages * m, 256), F32)
    return pl.pallas_call(
        functools.partial(_cmp_proj_kernel, n_x=pps, n_prefetch=1),
        grid_spec=pltpu.PrefetchScalarGridSpec(
            num_scalar_prefetch=1,
            grid=(db, n_pages // pps),
            in_specs=[page_spec(k) for k in range(pps)] + [
                _full_spec(perm, 3), _full_spec(wk, 3), _full_spec(wv, 3)],
            out_specs=[ospec, ospec]),
        out_shape=[oshape, oshape],
        compiler_params=_cparams(2),
        name="cmp_proj_paged",
    )(page_table.reshape(-1), *([cache_t] * pps), perm, wk, wv)


def _cmp_mlp_kernel(abk_ref, abv_ref, pos_ref, w1_ref, b1_ref, w2_ref, b2_ref, ck_ref, cv_ref, ckr_ref):
    m = abk_ref.shape[0]
    col = lax.broadcasted_iota(jnp.int32, (HEAD_DIM, m), 1)
    for kv, (ab_ref, o_ref) in enumerate(((abk_ref, ck_ref), (abv_ref, cv_ref))):
        pos = jnp.broadcast_to(pos_ref[kv], (SUBLANES, CMP_LEN * HEAD_DIM)).astype(BF16)
        c0 = _dot(pos, w1_ref[kv].astype(BF16))[0:1, :] + b1_ref[kv]
        c0 = jnp.concatenate([c0] * NSA_KV, axis=1)
        ab = ab_ref[...]
        nxt = pltpu.roll(ab[:, LANES:], m - 1, 0)
        h = jax.nn.gelu(ab[:, :LANES] + nxt + c0).astype(BF16)
        for g in range(NSA_KV):
            y_t = jnp.where(col < m - 1, _dot_nt(w2_ref[kv, g], h) + b2_ref[kv], 0.0)
            o_ref[g] = y_t.astype(BF16)
            if kv == 0:
                ckr_ref[g] = y_t.T.astype(BF16)


def _cmp_mlp(abk, abv, pos_flat, w1, b1, w2t, b2col):
    bx, m, _ = abk.shape
    abspec = pl.BlockSpec((None, m, 256), lambda b: (b, 0, 0))
    ospec = pl.BlockSpec((None, NSA_KV, HEAD_DIM, m), lambda b: (b, 0, 0, 0))
    oshape = jax.ShapeDtypeStruct((bx, NSA_KV, HEAD_DIM, m), BF16)
    return pl.pallas_call(
        _cmp_mlp_kernel,
        grid=(bx,),
        in_specs=[abspec, abspec, _full_spec(pos_flat, 1), _full_spec(w1, 1), _full_spec(b1, 1),
                  _full_spec(w2t, 1), _full_spec(b2col, 1)],
        out_specs=[ospec, ospec, pl.BlockSpec((None, NSA_KV, m, HEAD_DIM), lambda b: (b, 0, 0, 0))],
        out_shape=[oshape, oshape, jax.ShapeDtypeStruct((bx, NSA_KV, m, HEAD_DIM), BF16)],
        compiler_params=_cparams(1),
        name="cmp_mlp",
    )(abk, abv, pos_flat, w1, b1, w2t, b2col)


def _bias_tiles_kernel(tab_ref, o_ref):
    h = pl.program_id(0)
    key = lax.broadcasted_iota(jnp.int32, (TQ, TQ), 0)
    qry = lax.broadcasted_iota(jnp.int32, (TQ, TQ), 1)
    far = tab_ref[N_BUCKETS - 1, h]
    for kind in range(2):
        dist = qry - key + kind * TQ
        bkt = _bucket(dist)
        bias = jnp.zeros((TQ, TQ), F32)
        for b in range(N_BUCKETS - 1):
            bias = jnp.where(bkt == b, tab_ref[b, h] - far, bias)
        o_ref[kind] = jnp.where(dist >= 0, bias, NEG_INF)


def _bias_tiles(rel_bias):
    nh = rel_bias.shape[1]
    return pl.pallas_call(
        _bias_tiles_kernel,
        grid=(nh,),
        in_specs=[pl.BlockSpec(memory_space=pltpu.SMEM)],
        out_specs=pl.BlockSpec((2, None, TQ, TQ), lambda h: (0, h, 0, 0)),
        out_shape=jax.ShapeDtypeStruct((2, nh, TQ, TQ), F32),
        compiler_params=_cparams(1),
        name="bias_tiles",
    )(rel_bias)


def _nsa_cmp_kernel(tab_ref, q_ref, ck_ref, cv_ref, ov_ref, oc_ref, dn_ref, *s_scr, n_slc, k_sel):
    i = pl.program_id(1)
    ncp = ck_ref.shape[1]
    jl = ov_ref.shape[0]
    q0 = i * TQ
    qpos = q0 + lax.broadcasted_iota(jnp.int32, (ncp, TQ), 1)
    cend = lax.broadcasted_iota(jnp.int32, (ncp, TQ), 0) * CMP_STRIDE + (CMP_LEN - 1)
    valid = qpos >= cend
    n0 = jnp.maximum(q0 - (MAX_DISTANCE + CMP_LEN - 1), 0) // CMP_STRIDE
    n0 = pl.multiple_of(jnp.minimum(n0 // SUBLANES * SUBLANES, ncp - CMP_NEAR), SUBLANES)
    near_end = (n0 + lax.broadcasted_iota(jnp.int32, (CMP_NEAR, TQ), 0)) * CMP_STRIDE + (CMP_LEN - 1)
    bkt = _bucket(q0 + lax.broadcasted_iota(jnp.int32, (CMP_NEAR, TQ), 1) - near_end)
    j = lax.broadcasted_iota(jnp.int32, (jl, TQ), 0)
    cur = (q0 + lax.broadcasted_iota(jnp.int32, (jl, TQ), 1)) // SLC_BLOCK
    forced = (j == 0) | (j == cur) | (j == cur - 1)
    outs = []
    for g in range(NSA_KV):
        heads = range(g * NSA_REP, (g + 1) * NSA_REP)
        bias = [jnp.zeros((CMP_NEAR, TQ), F32) for _ in heads]
        for b in range(N_BUCKETS - 1):
            hit = bkt == b
            bias = [jnp.where(hit, tab_ref[b, h] - tab_ref[N_BUCKETS - 1, h], bb) for h, bb in zip(heads, bias)]
        ck = ck_ref[g]
        cv = cv_ref[g]
        imp = jnp.zeros((jl, TQ), F32)
        for r, h in enumerate(heads):
            scr = s_scr[r]
            scr[...] = _dot_nt(ck, q_ref[h])
            scr[pl.ds(n0, CMP_NEAR), :] = scr[pl.ds(n0, CMP_NEAR), :] + bias[r]
            s = jnp.where(valid, scr[...], NEG_INF)
            m = jnp.max(s, axis=0, keepdims=True)
            m = jnp.where(m == NEG_INF, 0.0, m)
            e = jnp.exp2(s - m)
            p = (e / jnp.maximum(jnp.sum(e, axis=0, keepdims=True), 1e-30)).astype(BF16)
            outs.append(_dot(cv, p).T)
            imp = imp + _dot(ov_ref[...], p)
        imp = jnp.where(forced, jnp.inf, imp)
        imp = jnp.where(j <= cur, imp, NEG_INF)
        dn_ref[g] = _topk_drop_rows(imp, n_slc, k_sel)
    oc_ref[...] = jnp.concatenate(outs, axis=1)


def _nsa_cmp_prompt(rel_bias, qk, ck, cv_t, overlap_t, n_slc, k_sel):
    b, _, t, _ = qk.shape
    ncp = ck.shape[2]
    jl = overlap_t.shape[0]
    return pl.pallas_call(
        functools.partial(_nsa_cmp_kernel, n_slc=n_slc, k_sel=k_sel),
        grid=(b, t // TQ),
        in_specs=[pl.BlockSpec(memory_space=pltpu.SMEM),
                  pl.BlockSpec((None, NSA_HEADS, TQ, HEAD_DIM), lambda bb, i: (bb, H_QA // NSA_HEADS, i, 0)),
                  pl.BlockSpec((None, NSA_KV, ncp, HEAD_DIM), lambda bb, i: (bb, 0, 0, 0)),
                  pl.BlockSpec((None, NSA_KV, HEAD_DIM, ncp), lambda bb, i: (bb, 0, 0, 0)),
                  _full_spec(overlap_t, 2)],
        out_specs=[pl.BlockSpec((None, TQ, NSA_W), lambda bb, i: (bb, i, 0)),
                   pl.BlockSpec((None, NSA_KV, jl, TQ), lambda bb, i: (bb, 0, 0, i))],
        out_shape=[jax.ShapeDtypeStruct((b, t, NSA_W), F32),
                   jax.ShapeDtypeStruct((b, NSA_KV, jl, t), F32)],
        scratch_shapes=[pltpu.VMEM((ncp, TQ), F32)] * NSA_REP,
        compiler_params=_cparams(2, 40),
        name="nsa_cmp_prompt",
    )(rel_bias, qk, ck, cv_t, overlap_t)


def _k_tile(ref, kt):
    return ref[pl.ds(pl.multiple_of(kt * TQ, TQ), TQ), :]


def _v_tile(ref, kt):
    return ref[:, pl.ds(pl.multiple_of(kt * TQ, TQ), TQ)].astype(BF16)


def _nsa_slc_kernel(q_ref, k0_ref, k1_ref, v0_ref, v1_ref, dn_ref, bt_ref, o_ref, *scr):
    i = pl.program_id(1)
    states = _flash_states(scr)
    qs = [q_ref[h] for h in range(NSA_HEADS)]
    per_tile = TQ // SLC_BLOCK
    for st in states:
        _softmax_init(*st)

    def tile(kt, kind):
        ks, v_ts, adds, masks = [], [], [], []
        for g, (k_ref, v_ref) in enumerate(((k0_ref, v0_ref), (k1_ref, v1_ref))):
            rows = [dn_ref[g, pl.ds(kt * per_tile + jb, 1), :] for jb in range(per_tile)]
            ks += [_k_tile(k_ref, kt)] * NSA_REP
            v_ts += [_v_tile(v_ref, kt)] * NSA_REP
            adds += [[] if kind is None else [bt_ref[kind, g * NSA_REP + r]] for r in range(NSA_REP)]
            masks += [rows] * NSA_REP
        _flash_tile(qs, ks, v_ts, states, adds, masks)

    _far_tiles(jnp.maximum(i - 1, 0), lambda kt: tile(kt, None))

    @pl.when(i >= 1)
    def _():
        tile(i - 1, 1)

    tile(i, 0)
    o_ref[...] = jnp.concatenate([_softmax_finish(st[1], st[2]) for st in states], axis=1)


def _nsa_slc_prompt(qk, kv_t, layer, dn, btiles):
    b, _, t, _ = qk.shape
    jl = dn.shape[2]
    return pl.pallas_call(
        _nsa_slc_kernel,
        grid=(b, t // TQ),
        in_specs=_nsa_dense_specs(t, H_KS, layer) + [
            pl.BlockSpec((None, NSA_KV, jl, TQ), lambda bb, i: (bb, 0, 0, i)),
            pl.BlockSpec((2, NSA_HEADS, TQ, TQ), lambda bb, i: (0, 0, 0, 0))],
        out_specs=pl.BlockSpec((None, TQ, NSA_W), lambda bb, i: (bb, i, 0)),
        out_shape=jax.ShapeDtypeStruct((b, t, NSA_W), F32),
        scratch_shapes=_flash_scratch(NSA_HEADS),
        compiler_params=_cparams(2, 40),
        name="nsa_slc_prompt",
    )(qk, qk, qk, kv_t, kv_t, dn, btiles)


def _nsa_dense_specs(t, k_head, layer):
    return [pl.BlockSpec((None, NSA_HEADS, TQ, HEAD_DIM), lambda bb, i: (bb, H_QA // NSA_HEADS, i, 0)),
            pl.BlockSpec((None, None, t, HEAD_DIM), lambda bb, i: (bb, k_head, 0, 0)),
            pl.BlockSpec((None, None, t, HEAD_DIM), lambda bb, i: (bb, k_head + 1, 0, 0)),
            pl.BlockSpec((None, None, HEAD_DIM, t), lambda bb, i: (bb, layer, NSA_KV, 0)),
            pl.BlockSpec((None, None, HEAD_DIM, t), lambda bb, i: (bb, layer, NSA_KV + 1, 0))]


def _nsa_win_kernel(q_ref, k0_ref, k1_ref, v0_ref, v1_ref, bt_ref, o_ref, *scr):
    i = pl.program_id(1)
    states = _flash_states(scr)
    qs = [q_ref[h] for h in range(NSA_HEADS)]
    for st in states:
        _softmax_init(*st)

    def tile(kt, adds):
        ks = [_k_tile(k0_ref, kt)] * NSA_REP + [_k_tile(k1_ref, kt)] * NSA_REP
        v_ts = [_v_tile(v0_ref, kt)] * NSA_REP + [_v_tile(v1_ref, kt)] * NSA_REP
        _flash_tile(qs, ks, v_ts, states, adds)

    @pl.when(i >= WINDOW // TQ)
    def _():
        key = lax.broadcasted_iota(jnp.int32, (TQ, TQ), 0)
        qry = lax.broadcasted_iota(jnp.int32, (TQ, TQ), 1)
        tile(i - WINDOW // TQ, [[jnp.where(key > qry, 0.0, NEG_INF)]] * NSA_HEADS)

    @pl.when(i >= 1)
    def _():
        tile(i - 1, [[bt_ref[1, h]] for h in range(NSA_HEADS)])

    tile(i, [[bt_ref[0, h]] for h in range(NSA_HEADS)])
    o_ref[...] = jnp.concatenate([_softmax_finish(st[1], st[2]) for st in states], axis=1)


def _nsa_win_prompt(qk, kv_t, layer, btiles):
    b, _, t, _ = qk.shape
    return pl.pallas_call(
        _nsa_win_kernel,
        grid=(b, t // TQ),
        in_specs=_nsa_dense_specs(t, H_KW, layer) + [
            pl.BlockSpec((2, NSA_HEADS, TQ, TQ), lambda bb, i: (0, 0, 0, 0))],
        out_specs=pl.BlockSpec((None, TQ, NSA_W), lambda bb, i: (bb, i, 0)),
        out_shape=jax.ShapeDtypeStruct((b, t, NSA_W), F32),
        scratch_shapes=_flash_scratch(NSA_HEADS),
        compiler_params=_cparams(2, 40),
        name="nsa_win_prompt",
    )(qk, qk, qk, kv_t, kv_t, btiles)


def _kmean_kernel(k_ref, o_ref, *, nblk):
    lane = lax.broadcasted_iota(jnp.int32, (MOBA_W, LANES), 1)
    acc = jnp.zeros((MOBA_W, LANES), F32)
    for blk in range(nblk):
        mean = jnp.sum(k_ref[:, blk * MOBA_BLOCK:(blk + 1) * MOBA_BLOCK], axis=1, keepdims=True) / MOBA_BLOCK
        acc = jnp.where(lane == blk, mean, acc)
    for h in range(MOBA_HEADS):
        o_ref[h] = acc[h * HEAD_DIM:(h + 1) * HEAD_DIM, :].T.astype(BF16)


def _kmean_prompt(moba_t, layer):
    bx, _, _, t = moba_t.shape
    return pl.pallas_call(
        functools.partial(_kmean_kernel, nblk=t // MOBA_BLOCK),
        grid=(bx,),
        in_specs=[pl.BlockSpec((None, None, MOBA_W, t), lambda b: (b, layer, 0, 0))],
        out_specs=pl.BlockSpec((None, MOBA_HEADS, LANES, HEAD_DIM), lambda b: (b, 0, 0, 0)),
        out_shape=jax.ShapeDtypeStruct((bx, MOBA_HEADS, LANES, HEAD_DIM), BF16),
        compiler_params=_cparams(1, 40),
        name="moba_kmean_prompt",
    )(moba_t)


def _moba_kernel(q_ref, k_ref, v_ref, km_ref, bt_ref, o_ref, *scr, nblk, k_m):
    i = pl.program_id(2)
    hb = q_ref.shape[0]
    states = _flash_states(scr[:3 * hb])
    dn_scr = scr[3 * hb:]
    qs = [q_ref[hh] for hh in range(hb)]
    for hh in range(hb):
        gs = _dot_nt(km_ref[hh], qs[hh])
        blk = lax.broadcasted_iota(jnp.int32, gs.shape, 0)
        dn_scr[hh][...] = _topk_drop_rows(jnp.where(blk < i, gs, NEG_INF), nblk, k_m)
        _softmax_init(*states[hh])

    def tile(kt, bias_kind, masked):
        start = pl.multiple_of(kt * TQ, TQ)
        ks = [k_ref[hh, pl.ds(start, TQ), :] for hh in range(hb)]
        v_ts = [v_ref[hh * HEAD_DIM:(hh + 1) * HEAD_DIM, pl.ds(start, TQ)].astype(BF16) for hh in range(hb)]
        adds = [[] if bias_kind is None else [bt_ref[bias_kind, hh]] for hh in range(hb)]
        masks = [[dn_scr[hh][pl.ds(kt, 1), :]] for hh in range(hb)] if masked else None
        _flash_tile(qs, ks, v_ts, states, adds, masks)

    _far_tiles(jnp.maximum(i - 1, 0), lambda kt: tile(kt, None, True))

    @pl.when(i >= 1)
    def _():
        tile(i - 1, 1, True)

    tile(i, 0, False)
    o_ref[...] = jnp.concatenate([_softmax_finish(st[1], st[2]) for st in states], axis=1)


def _moba_prompt(qk, moba_t, layer, kmean, btiles, k_m):
    b, _, t, _ = qk.shape
    hb = MOBA_HB
    nq = t // TQ
    vb = MOBA_W // (hb * HEAD_DIM)
    once = dict(pipeline_mode=pl.Buffered(1))
    return pl.pallas_call(
        functools.partial(_moba_kernel, nblk=t // MOBA_BLOCK, k_m=k_m),
        grid=(b, MOBA_HEADS // hb, nq),
        in_specs=[pl.BlockSpec((None, hb, TQ, HEAD_DIM), lambda bb, hp, i: (bb, H_QB // hb + hp, i, 0)),
                  pl.BlockSpec((None, hb, t, HEAD_DIM), lambda bb, hp, i: (bb, H_KB // hb + hp, 0, 0), **once),
                  pl.BlockSpec((None, None, hb * HEAD_DIM, t), lambda bb, hp, i: (bb, layer, vb + hp, 0), **once),
                  pl.BlockSpec((None, hb, LANES, HEAD_DIM), lambda bb, hp, i: (bb, hp, 0, 0)),
                  pl.BlockSpec((2, hb, TQ, TQ), lambda bb, hp, i: (0, NSA_HEADS // hb + hp, 0, 0), **once)],
        out_specs=pl.BlockSpec((None, TQ, hb * HEAD_DIM), lambda bb, hp, i: (bb, i, hp)),
        out_shape=jax.ShapeDtypeStruct((b, t, MOBA_W), F32),
        scratch_shapes=_flash_scratch(hb) + [pltpu.VMEM((LANES, TQ), F32)] * hb,
        compiler_params=_cparams(3, 48),
        name="moba_prompt",
    )(qk, qk, moba_t, kmean, btiles)


def _outproj_kernel(x_ref, gate_ref, az_ref, bz_ref, ma_ref, mb_ref, ag_ref, oc_ref, os_ref, ow_ref, ob_ref,
                    eg_ref, wua_ref, wub_ref, wo_ref, lng_ref, lnb_ref, y_ref, *, alpha):
    w = NSA_W
    g = _sigmoid(ag_ref[...])
    g_hi = g.astype(BF16)
    g_lo = (g - g_hi.astype(F32)).astype(BF16)
    ge = _dot(g_hi, eg_ref[...]) + _dot(g_lo, eg_ref[...])
    o_a = ge[:, 0:w] * oc_ref[...] + ge[:, w:2 * w] * os_ref[...] + ge[:, 2 * w:3 * w] * ow_ref[...]
    y_a = _dot((o_a * _silu(az_ref[...])).astype(BF16), wua_ref[...])
    y_b = _dot((ob_ref[...] * _silu(bz_ref[...])).astype(BF16), wub_ref[...])
    mixed = _dot((_sigmoid(ma_ref[...]) * y_a + _sigmoid(mb_ref[...]) * y_b).astype(BF16), wo_ref[...])
    z = alpha * x_ref[...] + gate_ref[...] * mixed
    mu = jnp.mean(z, axis=-1, keepdims=True)
    var = jnp.mean(jnp.square(z - mu), axis=-1, keepdims=True)
    y_ref[...] = (z - mu) * lax.rsqrt(var + LN_EPS) * lng_ref[...] + lnb_ref[...]


def _outproj(x, gate, z, o_c, o_s, o_w, o_b, eg, wua, wub, wo, ln_g, ln_b, alpha):
    bx, t, d = x.shape
    tm = min(t, 512)
    nt = t // tm

    def tok(width, col):
        return pl.BlockSpec((None, tm, width), lambda m: (m // nt, m % nt, col))

    return pl.pallas_call(
        functools.partial(_outproj_kernel, alpha=alpha),
        grid=(bx * nt,),
        in_specs=[tok(d, 0), _mod_spec(gate, tm, nt),
                  tok(NSA_W, Z_AZ // NSA_W), tok(MOBA_W, Z_BZ // MOBA_W), tok(d, Z_MA // d), tok(d, Z_MB // d),
                  tok(AG_PAD, Z_AG // AG_PAD), tok(NSA_W, 0), tok(NSA_W, 0), tok(NSA_W, 0), tok(MOBA_W, 0),
                  _full_spec(eg, 1), _full_spec(wua, 1), _full_spec(wub, 1), _full_spec(wo, 1),
                  _full_spec(ln_g, 1), _full_spec(ln_b, 1)],
        out_specs=tok(d, 0),
        out_shape=jax.ShapeDtypeStruct((bx, t, d), F32),
        compiler_params=_cparams(1, 48),
        name="out_proj",
    )(x, gate, z, z, z, z, z, o_c, o_s, o_w, o_b, eg, wua, wub, wo, ln_g, ln_b)


def _dec_attend(s, v_t, s_self, v_self):
    m = jnp.maximum(jnp.max(s, axis=1, keepdims=True), s_self)
    e = jnp.exp2(s - m)
    e_self = jnp.exp2(s_self - m)
    den = jnp.maximum(jnp.sum(e, axis=1, keepdims=True) + e_self, 1e-30)
    return (_dot_nt(e.astype(BF16), v_t) + e_self * v_self) / den


def _dec_cmp_kernel(tab_ref, q_ref, ck_ref, cv_ref, ov_ref, oc_ref, imp_ref, *, pos):
    ncp = ck_ref.shape[2]
    jl = ov_ref.shape[1]
    cend = lax.broadcasted_iota(jnp.int32, (SUBLANES, ncp), 1) * CMP_STRIDE + (CMP_LEN - 1)
    dist = pos - cend
    valid = dist >= 0
    bkt = _bucket(dist)
    j = lax.broadcasted_iota(jnp.int32, (SUBLANES, jl), 1)
    cur = pos // SLC_BLOCK
    for g in range(NSA_KV):
        s = jnp.where(valid, _dot(q_ref[g], ck_ref[g]) + _bias_rows(bkt, tab_ref[g]), NEG_INF)
        m = jnp.max(s, axis=1, keepdims=True)
        m = jnp.where(m == NEG_INF, 0.0, m)
        e = jnp.exp2(s - m)
        p = (e / jnp.maximum(jnp.sum(e, axis=1, keepdims=True), 1e-30)).astype(BF16)
        oc_ref[g] = _dot_nt(p, cv_ref[g])
        imp4 = _dot(p, ov_ref[...])
        imp = imp4[0:1]
        for r in range(1, NSA_REP):
            imp = imp + imp4[r:r + 1]
        imp = jnp.broadcast_to(imp, (SUBLANES, jl))
        imp = jnp.where((j == 0) | (j == cur) | (j == cur - 1), jnp.inf, imp)
        imp_ref[g] = jnp.where(j <= cur, imp, NEG_INF)


def _dec_cmp(tab_g, q8, ck_t, cv_t, overlap, pos):
    db = q8.shape[0]
    ncp = ck_t.shape[3]
    jl = overlap.shape[1]
    cspec = pl.BlockSpec((None, NSA_KV, HEAD_DIM, ncp), lambda b: (b, 0, 0, 0))
    return pl.pallas_call(
        functools.partial(_dec_cmp_kernel, pos=pos),
        grid=(db,),
        in_specs=[_full_spec(tab_g, 1),
                  pl.BlockSpec((None, NSA_KV, SUBLANES, HEAD_DIM), lambda b: (b, 0, 0, 0)),
                  cspec, cspec, _full_spec(overlap, 1)],
        out_specs=[pl.BlockSpec((None, NSA_KV, SUBLANES, HEAD_DIM), lambda b: (b, 0, 0, 0)),
                   pl.BlockSpec((None, NSA_KV, SUBLANES, jl), lambda b: (b, 0, 0, 0))],
        out_shape=[jax.ShapeDtypeStruct((db, NSA_KV, SUBLANES, HEAD_DIM), F32),
                   jax.ShapeDtypeStruct((db, NSA_KV, SUBLANES, jl), F32)],
        compiler_params=_cparams(1),
        name="dec_nsa_cmp",
    )(tab_g, q8, ck_t, cv_t, overlap)


def _topk_idx_kernel(s_ref, idx_ref, *, n, k):
    st = s_ref[...].T
    nb = -(-n // SUBLANES)
    blocks = [st[rb * SUBLANES:(rb + 1) * SUBLANES, :] for rb in range(nb)]
    cnt = _rank_rows(blocks, n)
    sub = lax.broadcasted_iota(jnp.int32, blocks[0].shape, 0)
    rows = []
    for r in range(idx_ref.shape[0]):
        if r >= k:
            rows.append(jnp.full((1, st.shape[1]), -1, jnp.int32))
            continue
        acc = jnp.zeros(blocks[0].shape, jnp.int32)
        for rb in range(nb):
            hit = jnp.where(cnt[rb] == float(r), jnp.where(blocks[rb] > NEG_INF, 1, 0), 0)
            acc = acc + hit * (sub + (rb * SUBLANES + 1))
        rows.append(jnp.sum(acc, axis=0, keepdims=True) - 1)
    idx_ref[...] = jnp.concatenate(rows, axis=0)


def _topk_idx(scores, n, k):
    nrow, jl = scores.shape
    kp = -(-k // SUBLANES) * SUBLANES
    return pl.pallas_call(
        functools.partial(_topk_idx_kernel, n=n, k=k),
        grid=(1,),
        in_specs=[pl.BlockSpec((nrow, jl), lambda i: (0, 0))],
        out_specs=pl.BlockSpec((kp, nrow), lambda i: (0, 0)),
        out_shape=jax.ShapeDtypeStruct((kp, nrow), jnp.int32),
        compiler_params=_cparams(1),
        name="topk_idx",
    )(scores)


def _dec_slc_kernel(sel_ref, pt_ref, *refs, pos, n_slc, k_sel):
    tab_ref, q_ref, kvn_ref, o_ref = refs[NSA_KV * 2 * k_sel:]
    b = pl.program_id(0)
    n_keys = k_sel * PAGE_SIZE
    lane = lax.broadcasted_iota(jnp.int32, (SUBLANES, n_keys), 1)
    half = PAGE_SIZE // SLC_BLOCK
    for g in range(NSA_KV):
        k_refs = refs[g * 2 * k_sel:g * 2 * k_sel + k_sel]
        v_refs = refs[g * 2 * k_sel + k_sel:(g + 1) * 2 * k_sel]
        kpos = jnp.zeros((SUBLANES, n_keys), jnp.int32)
        for k in range(k_sel):
            jk = sel_ref[(b * NSA_KV + g) * k_sel + k]
            ok = (jk >= 0) & (jk < n_slc - 1)
            in_blk = (lane % PAGE_SIZE) // SLC_BLOCK == jk % half
            here = jnp.where(in_blk, (jk // half) * PAGE_SIZE + lane % PAGE_SIZE, pos + 1)
            kpos = jnp.where(lane // PAGE_SIZE == k, jnp.where(ok, here, pos + 1), kpos)
        dist = pos - kpos
        tab = tab_ref[g]
        q = q_ref[g]
        k_t = jnp.concatenate([r[...] for r in k_refs], axis=1).astype(BF16)
        v_t = jnp.concatenate([r[...] for r in v_refs], axis=1).astype(BF16)
        s = jnp.where(dist >= 0, _dot(q, k_t) + _bias_rows(_bucket(dist), tab), NEG_INF)
        s_self = jnp.sum(q.astype(F32) * kvn_ref[g:g + 1, :], axis=1, keepdims=True) + tab[:, 0:1]
        o_ref[g] = _dec_attend(s, v_t, s_self, kvn_ref[NSA_KV + g:NSA_KV + g + 1, :])


def _dec_slc(sel_flat, pt_flat, cache_t, layer, tab_g, q8, kv_new, pos, n_slc, k_sel, n_pages):
    db = q8.shape[0]
    half = PAGE_SIZE // SLC_BLOCK

    def blk_spec(g, k, kv):
        def imap(b, sel, pt):
            j = jnp.clip(sel[(b * NSA_KV + g) * k_sel + k], 0, n_slc - 2)
            return (pt[b * n_pages + j // half], layer, kv, g, 0, 0)
        return pl.BlockSpec((None, None, None, None, HEAD_DIM, PAGE_SIZE), imap)

    qspec = pl.BlockSpec((None, NSA_KV, SUBLANES, HEAD_DIM), lambda b, sel, pt: (b, 0, 0, 0))
    return pl.pallas_call(
        functools.partial(_dec_slc_kernel, pos=pos, n_slc=n_slc, k_sel=k_sel),
        grid_spec=pltpu.PrefetchScalarGridSpec(
            num_scalar_prefetch=2,
            grid=(db,),
            in_specs=[blk_spec(g, k, kv) for g in range(NSA_KV) for kv in range(2) for k in range(k_sel)] + [
                _full_spec(tab_g, 3), qspec,
                pl.BlockSpec((None, 2 * NSA_KV, HEAD_DIM), lambda b, sel, pt: (b, 0, 0))],
            out_specs=qspec),
        out_shape=jax.ShapeDtypeStruct((db, NSA_KV, SUBLANES, HEAD_DIM), F32),
        compiler_params=_cparams(1),
        name="dec_nsa_slc",
    )(sel_flat, pt_flat, *([cache_t] * (NSA_KV * 2 * k_sel)), tab_g, q8, kv_new)


def _dec_win_kernel(k0_ref, k1_ref, v0_ref, v1_ref, tab_ref, q_ref, kvn_ref, o_ref):
    n = k0_ref.shape[1]
    dist = n - lax.broadcasted_iota(jnp.int32, (SUBLANES, n), 1)
    bkt = _bucket(dist)
    for g, (k_ref, v_ref) in enumerate(((k0_ref, v0_ref), (k1_ref, v1_ref))):
        tab = tab_ref[g]
        q = q_ref[g]
        s = jnp.where(dist < WINDOW, _dot(q, k_ref[...].astype(BF16)) + _bias_rows(bkt, tab), NEG_INF)
        s_self = jnp.sum(q.astype(F32) * kvn_ref[g:g + 1, :], axis=1, keepdims=True) + tab[:, 0:1]
        o_ref[g] = _dec_attend(s, v_ref[...].astype(BF16), s_self, kvn_ref[NSA_KV + g:NSA_KV + g + 1, :])


def _dec_win(state_t, layer, tab_g, q8, kv_new):
    db = q8.shape[0]
    n = state_t.shape[5]

    def st_spec(kv, g):
        return pl.BlockSpec((None, None, None, None, HEAD_DIM, n), lambda b: (layer, b, kv, g, 0, 0))

    return pl.pallas_call(
        _dec_win_kernel,
        grid=(db,),
        in_specs=[st_spec(0, 0), st_spec(0, 1), st_spec(1, 0), st_spec(1, 1),
                  _full_spec(tab_g, 1),
                  pl.BlockSpec((None, NSA_KV, SUBLANES, HEAD_DIM), lambda b: (b, 0, 0, 0)),
                  pl.BlockSpec((None, 2 * NSA_KV, HEAD_DIM), lambda b: (b, 0, 0))],
        out_specs=pl.BlockSpec((None, NSA_KV, SUBLANES, HEAD_DIM), lambda b: (b, 0, 0, 0)),
        out_shape=jax.ShapeDtypeStruct((db, NSA_KV, SUBLANES, HEAD_DIM), F32),
        compiler_params=_cparams(1),
        name="dec_nsa_win",
    )(state_t, state_t, state_t, state_t, tab_g, q8, kv_new)


def _dec_moba_sweep_kernel(pt_ref, *refs, n_x):
    k_refs = refs[:n_x]
    qb_ref, s_ref = refs[n_x:]
    qb = qb_ref[...]
    for k, k_ref in enumerate(k_refs):
        prod = k_ref[...] * qb
        s_ref[:, k * PAGE_SIZE:(k + 1) * PAGE_SIZE] = jnp.concatenate(
            [jnp.sum(prod[h * HEAD_DIM:(h + 1) * HEAD_DIM], axis=0, keepdims=True) for h in range(MOBA_HEADS)],
            axis=0)


def _dec_moba_sweep(cache_t, page_table, layer, q_lanes):
    db, n_pages = page_table.shape
    pps = PAGES_PER_STEP

    def page_spec(k):
        return pl.BlockSpec((None, None, None, MOBA_W, PAGE_SIZE),
                            lambda b, i, pt: (pt[b * n_pages + i * pps + k], layer, 0, 0, 0))

    return pl.pallas_call(
        functools.partial(_dec_moba_sweep_kernel, n_x=pps),
        grid_spec=pltpu.PrefetchScalarGridSpec(
            num_scalar_prefetch=1,
            grid=(db, n_pages // pps),
            in_specs=[page_spec(k) for k in range(pps)] + [
                pl.BlockSpec((None, MOBA_W, PAGE_SIZE), lambda b, i, pt: (b, 0, 0))],
            out_specs=pl.BlockSpec((None, MOBA_HEADS, pps * PAGE_SIZE), lambda b, i, pt: (b, 0, i))),
        out_shape=jax.ShapeDtypeStruct((db, MOBA_HEADS, n_pages * PAGE_SIZE), F32),
        compiler_params=_cparams(2, 40),
        name="dec_moba_sweep",
    )(page_table.reshape(-1), *([cache_t] * pps), q_lanes)


def _dec_moba_gate_kernel(s_ref, gs_ref, *, nblk):
    lane = lax.broadcasted_iota(jnp.int32, gs_ref.shape, 1)
    gs = jnp.full(gs_ref.shape, NEG_INF, F32)
    for blk in range(nblk):
        mean = jnp.sum(s_ref[:, blk * MOBA_BLOCK:(blk + 1) * MOBA_BLOCK], axis=1, keepdims=True) / MOBA_BLOCK
        gs = jnp.where(lane == blk, mean, gs)
    gs_ref[...] = gs


def _dec_moba_gate(s_all, nblk):
    db, _, p = s_all.shape
    return pl.pallas_call(
        functools.partial(_dec_moba_gate_kernel, nblk=nblk),
        grid=(db,),
        in_specs=[pl.BlockSpec((None, MOBA_HEADS, p), lambda b: (b, 0, 0))],
        out_specs=pl.BlockSpec((None, MOBA_HEADS, LANES), lambda b: (b, 0, 0)),
        out_shape=jax.ShapeDtypeStruct((db, MOBA_HEADS, LANES), F32),
        compiler_params=_cparams(1),
        name="dec_moba_gate",
    )(s_all)


def _dec_moba_attend_kernel(sel_ref, pt_ref, *refs, pos, k_m):
    ppb = MOBA_BLOCK // PAGE_SIZE
    per_h = k_m + k_m * ppb
    tab_ref, q_ref, kn_ref, vn_ref, o_ref = refs[MOBA_HEADS * per_h:]
    b = pl.program_id(0)
    n_keys = k_m * MOBA_BLOCK
    lane = lax.broadcasted_iota(jnp.int32, (SUBLANES, n_keys), 1)
    tab = tab_ref[...]
    s_self = jnp.sum(q_ref[...] * kn_ref[...], axis=1, keepdims=True) + tab[:, 0:1]
    outs = []
    for h in range(MOBA_HEADS):
        s_refs = refs[h * per_h:h * per_h + k_m]
        v_refs = refs[h * per_h + k_m:(h + 1) * per_h]
        kpos = jnp.zeros((SUBLANES, n_keys), jnp.int32)
        for k in range(k_m):
            jk = sel_ref[(b * MOBA_HEADS + h) * k_m + k]
            kpos = jnp.where(lane // MOBA_BLOCK == k,
                             jnp.where(jk >= 0, jk * MOBA_BLOCK + lane % MOBA_BLOCK, pos + 1), kpos)
        dist = pos - kpos
        tab_h = jnp.broadcast_to(tab[h:h + 1], (SUBLANES, N_BUCKETS))
        s = jnp.broadcast_to(jnp.concatenate([r[h:h + 1, :] for r in s_refs], axis=1), (SUBLANES, n_keys))
        s = jnp.where(dist >= 0, s + _bias_rows(_bucket(dist), tab_h), NEG_INF)
        v_t = jnp.concatenate([r[...] for r in v_refs], axis=1).astype(BF16)
        o = _dec_attend(s, v_t, jnp.broadcast_to(s_self[h:h + 1], (SUBLANES, 1)), vn_ref[h:h + 1, :])
        outs.append(o[0:1])
    o_ref[...] = jnp.concatenate(outs, axis=0)


def _dec_moba_attend(sel_flat, pt_flat, s_all, cache_t, layer, tab_h, q, k_new, v_new, pos, k_m, n_pages, nblk):
    db = q.shape[0]
    ppb = MOBA_BLOCK // PAGE_SIZE

    def sel_of(b, h, sel, k):
        return jnp.clip(sel[(b * MOBA_HEADS + h) * k_m + k], 0, nblk - 1)

    def s_spec(h, k):
        return pl.BlockSpec((None, MOBA_HEADS, MOBA_BLOCK), lambda b, sel, pt: (b, 0, sel_of(b, h, sel, k)))

    def v_spec(h, k, pg):
        return pl.BlockSpec(
            (None, None, None, None, HEAD_DIM, PAGE_SIZE),
            lambda b, sel, pt: (pt[b * n_pages + sel_of(b, h, sel, k) * ppb + pg], layer, 1, h, 0, 0))

    head_specs, head_args = [], []
    for h in range(MOBA_HEADS):
        head_specs += [s_spec(h, k) for k in range(k_m)] + [v_spec(h, k, pg) for k in range(k_m) for pg in range(ppb)]
        head_args += [s_all] * k_m + [cache_t] * (k_m * ppb)
    row_spec = pl.BlockSpec((None, MOBA_HEADS, HEAD_DIM), lambda b, sel, pt: (b, 0, 0))
    return pl.pallas_call(
        functools.partial(_dec_moba_attend_kernel, pos=pos, k_m=k_m),
        grid_spec=pltpu.PrefetchScalarGridSpec(
            num_scalar_prefetch=2,
            grid=(db,),
            in_specs=head_specs + [_full_spec(tab_h, 3), row_spec, row_spec, row_spec],
            out_specs=row_spec),
        out_shape=jax.ShapeDtypeStruct((db, MOBA_HEADS, HEAD_DIM), F32),
        compiler_params=_cparams(1),
        name="dec_moba_attend",
    )(sel_flat, pt_flat, *head_args, tab_h, q, k_new, v_new)


def _overlap_matrix(n_cmp, n_slc, rows, cols):
    i = np.arange(n_cmp)[:, None]
    j = np.arange(n_slc)[None, :]
    units = SLC_BLOCK // CMP_STRIDE
    m = sum(((i + u) // units == j).astype(np.float32) for u in range(CMP_LEN // CMP_STRIDE))
    out = np.zeros((rows, cols), np.float32)
    out[:n_cmp, :n_slc] = m
    return jnp.asarray(out, dtype=BF16)


def _token_group_permutation():
    groups = LANES // CMP_STRIDE
    p = np.zeros((2 * LANES, 2 * LANES), np.float32)
    for c in range(2):
        for l in range(CMP_STRIDE):
            for m in range(groups):
                p[c * LANES + l * groups + m, c * LANES + CMP_STRIDE * m + l] = 1.0
    return jnp.asarray(p, dtype=BF16)


def _gate_expand_matrix():
    e = np.zeros((AG_PAD, 3 * NSA_W), np.float32)
    for h in range(NSA_HEADS):
        for br in range(3):
            e[h * 3 + br, br * NSA_W + h * HEAD_DIM: br * NSA_W + (h + 1) * HEAD_DIM] = 1.0
    return jnp.asarray(e, dtype=BF16)


def _split_w_in(w_in):
    scale = HEAD_DIM ** -0.5 * LOG2E
    o = np.cumsum([0, 512, 128, 128, 128, 128, 128, 128, 24, 512, 512, 512, 512, 512, 1024, 1024])
    a_q, kv3, a_g, a_z, b_q, b_kv, b_z, m_ab = (
        w_in[..., o[0]:o[1]], w_in[..., o[1]:o[7]], w_in[..., o[7]:o[8]], w_in[..., o[8]:o[9]],
        w_in[..., o[9]:o[10]], w_in[..., o[10]:o[12]], w_in[..., o[12]:o[13]], w_in[..., o[13]:o[15]])
    pad = jnp.zeros(w_in.shape[:-1] + (AG_PAD - a_g.shape[-1],), w_in.dtype)
    k_rm = jnp.concatenate([w_in[..., o[10]:o[11]], w_in[..., o[3]:o[4]], w_in[..., o[5]:o[6]]], axis=-1)
    wq = jnp.concatenate([a_q * scale, b_q * scale, k_rm], axis=-1).astype(BF16)
    wz = jnp.concatenate([a_z, b_z, m_ab, a_g, pad], axis=-1).astype(BF16)
    wkv = jnp.swapaxes(jnp.concatenate([kv3, b_kv], axis=-1), 1, 2).astype(BF16)
    return wq, wz, wkv


def _cmp_stage1_weights(w1):
    w1r = w1.reshape(2, CMP_STRIDE, HEAD_DIM, HEAD_DIM)
    eye = jnp.eye(NSA_KV, dtype=w1.dtype)
    w = jnp.einsum("hlde,gf->lgdhfe", w1r, eye)
    return w.reshape(CMP_STRIDE * NSA_KV * HEAD_DIM, 2 * NSA_KV * HEAD_DIM).astype(BF16)


def _cmp_stage2_weights(w2):
    w2t = jnp.swapaxes(w2, 1, 2)
    z = jnp.zeros_like(w2t)
    return jnp.stack([jnp.concatenate([w2t, z], axis=2), jnp.concatenate([z, w2t], axis=2)], axis=1).astype(BF16)


def _cache_view(c):
    return jnp.transpose(c, (0, 1, 3, 4, 5, 2))


def _kv_output(kv_t, heads):
    b, depth, _, t = kv_t.shape
    return jnp.transpose(kv_t.reshape(b, depth, 2, heads, HEAD_DIM, t), (0, 1, 5, 2, 3, 4))


def _layer_prompt(x, mod, lw, consts, layer, depth, kv_prev):
    b, t, d = x.shape
    shift, scale, gate = mod
    q, z, cmp_t, slc_t, win_t, moba_t = _inproj(x, scale, shift, lw["wq"], lw["wz"], lw["wkv"], layer, depth,
                                                kv_prev)
    n_slc = t // SLC_BLOCK
    k_sel = min(SLC_TOPK, n_slc)
    abk, abv = _cmp_proj_prompt(cmp_t, layer, consts["perm"], lw["cmp_wk"], lw["cmp_wv"])
    _, cv_t, ck = _cmp_mlp(abk, abv, lw["pos_flat"], lw["phi_w1"], lw["phi_b1"], lw["cmp_w2t"], lw["phi_b2"])
    o_c, dn = _nsa_cmp_prompt(consts["rel_bias"], q, ck, cv_t, consts["overlap_p"], n_slc, k_sel)
    o_s = _nsa_slc_prompt(q, slc_t, layer, dn, consts["btiles"])
    o_w = _nsa_win_prompt(q, win_t, layer, consts["btiles"])
    nblk = t // MOBA_BLOCK
    o_b = _moba_prompt(q, moba_t, layer, _kmean_prompt(moba_t, layer), consts["btiles"], min(MOBA_TOPK, nblk - 1))
    y = _outproj(x, gate, z, o_c, o_s, o_w, o_b, consts["eg"], lw["w_up_a"], lw["w_up_b"], lw["w_out"],
                 lw["ln_g"], lw["ln_b"], consts["alpha"])
    return y, (cmp_t, slc_t, win_t, moba_t)


def _layer_sample(x, mod, lw, consts, layer, caches, page_table):
    _, db, d = x.shape
    shift, scale, gate = mod
    cache_cmp, cache_slc, cache_moba, state_win = caches
    n_phys, depth = cache_cmp.shape[:2]
    n_pages = page_table.shape[1]
    pos = n_pages * PAGE_SIZE
    pt_flat = page_table.reshape(-1)
    q, z, cmp_t, slc_t, win_t, moba_t = _inproj(x, scale, shift, lw["wq"], lw["wz"], lw["wkv"], 0, 1)
    cmp_n, slc_n, win_n = (a[0, 0].T.reshape(db, 2 * NSA_KV, HEAD_DIM) for a in (cmp_t, slc_t, win_t))
    moba_n = moba_t[0, 0].T.reshape(db, 2, MOBA_HEADS, HEAD_DIM)
    qa = jnp.transpose(q[0, H_QA:H_QA + NSA_HEADS], (1, 0, 2)).reshape(db, NSA_KV, NSA_REP, HEAD_DIM)
    q8 = jnp.pad(qa, ((0, 0), (0, 0), (0, SUBLANES - NSA_REP), (0, 0)))
    qb = jnp.transpose(q[0, H_QB:H_QB + MOBA_HEADS], (1, 0, 2)).astype(F32)
    abk, abv = _cmp_proj_paged(cache_cmp.reshape(n_phys, depth, KV_W, PAGE_SIZE), page_table, layer,
                               consts["perm"], lw["cmp_wk"], lw["cmp_wv"])
    ck_t, cv_t, _ = _cmp_mlp(abk, abv, lw["pos_flat"], lw["phi_w1"], lw["phi_b1"], lw["cmp_w2t"], lw["phi_b2"])
    n_slc = pos // SLC_BLOCK + 1
    k_sel = min(SLC_TOPK, n_slc)
    o_c8, imp = _dec_cmp(consts["tab_g"], q8, ck_t, cv_t, consts["overlap_s"], pos)
    imp2 = imp[:, :, 0].reshape(db * NSA_KV, -1)
    imp2 = jnp.pad(imp2, ((0, LANES - db * NSA_KV), (0, 0)), constant_values=NEG_INF)
    sel = _topk_idx(imp2, n_slc, k_sel)[:k_sel, :db * NSA_KV].T.reshape(-1)
    o_s8 = _dec_slc(sel, pt_flat, cache_slc, layer, consts["tab_g"], q8, slc_n, pos, n_slc, k_sel, n_pages)
    o_w8 = _dec_win(state_win, layer, consts["tab_g"], q8, win_n)
    o_c, o_s, o_w = (a[:, :, :NSA_REP].reshape(1, db, NSA_W) for a in (o_c8, o_s8, o_w8))
    nblk = pos // MOBA_BLOCK
    k_m = min(MOBA_TOPK, nblk)
    q_lanes = jnp.broadcast_to(qb.reshape(db, MOBA_W, 1), (db, MOBA_W, PAGE_SIZE))
    s_all = _dec_moba_sweep(cache_moba.reshape(n_phys, depth, 2, MOBA_W, PAGE_SIZE), page_table, layer, q_lanes)
    gs = _dec_moba_gate(s_all, nblk).reshape(db * MOBA_HEADS, LANES)
    sel_m = _topk_idx(gs, nblk, k_m)[:k_m].T.reshape(-1)
    o_b8 = _dec_moba_attend(sel_m, pt_flat, s_all, cache_moba, layer, consts["tab_h"], qb, moba_n[:, 0],
                            moba_n[:, 1], pos, k_m, n_pages, nblk)
    o_b = o_b8.reshape(1, db, MOBA_W)
    y = _outproj(x, gate, z, o_c, o_s, o_w, o_b, consts["eg"], lw["w_up_a"], lw["w_up_b"], lw["w_out"],
                 lw["ln_g"], lw["ln_b"], consts["alpha"])
    new = dict(cmp=cmp_n.reshape(db, 1, 2, NSA_KV, HEAD_DIM), slc=slc_n.reshape(db, 1, 2, NSA_KV, HEAD_DIM),
               win=win_n.reshape(db, 1, 2, NSA_KV, HEAD_DIM), moba=moba_n.reshape(db, 1, 2, MOBA_HEADS, HEAD_DIM))
    return y, new


def kernel(x_prompt, x_sample, cache_nsa_cmp, cache_nsa_slc, cache_moba, state_nsa_win, page_table, c_prompt, c_sample, rel_bias, w_ada, b_ada, w_in, phi_pos, phi_w1, phi_b1, phi_w2, phi_b2, w_up_a, w_up_b, w_out, ln_g, ln_b):
    b, t, d = x_prompt.shape
    db = x_sample.shape[0]
    depth = w_ada.shape[0]
    n_pages = page_table.shape[1]
    pos = n_pages * PAGE_SIZE
    assert x_sample.shape[1] == 1 and t % TQ == 0 and t >= WINDOW and n_pages % PAGES_PER_STEP == 0
    assert db * NSA_KV <= LANES and state_nsa_win.shape[2] == WINDOW and pos // MOBA_BLOCK >= 1
    assert t // MOBA_BLOCK <= LANES and t // SLC_BLOCK <= LANES and t // CMP_STRIDE >= CMP_NEAR

    mc = -(-(b + db) // SUBLANES) * SUBLANES
    c_all = jnp.pad(jnp.concatenate([c_prompt, c_sample], axis=0), ((0, mc - b - db), (0, 0)))
    mod = _ada(c_all, w_ada, b_ada)

    wq, wz, wkv = _split_w_in(w_in)
    rel_bias = rel_bias * LOG2E
    tab_rel = (rel_bias - rel_bias[N_BUCKETS - 1][None, :]).T
    tab_g = jnp.pad(tab_rel[:NSA_HEADS].reshape(NSA_KV, NSA_REP, N_BUCKETS),
                    ((0, 0), (0, SUBLANES - NSA_REP), (0, 0)))
    n_slc_s = pos // SLC_BLOCK + 1
    consts = dict(
        rel_bias=rel_bias,
        alpha=float((2 * depth) ** 0.25),
        btiles=_bias_tiles(rel_bias),
        perm=_token_group_permutation(),
        overlap_p=_overlap_matrix(t // CMP_STRIDE - 1, t // SLC_BLOCK, t // CMP_STRIDE, LANES).T,
        overlap_s=_overlap_matrix(pos // CMP_STRIDE - 1, n_slc_s, pos // CMP_STRIDE, -(-n_slc_s // LANES) * LANES),
        eg=_gate_expand_matrix(),
        tab_g=tab_g,
        tab_h=tab_rel[NSA_HEADS:],
    )
    caches = (_cache_view(cache_nsa_cmp), _cache_view(cache_nsa_slc), _cache_view(cache_moba),
              jnp.transpose(state_nsa_win, (0, 1, 3, 4, 5, 2)))

    yp, ys = x_prompt, x_sample.reshape(1, db, d)
    kv_p = tuple(jnp.zeros((b, depth, rows, t), F32) for rows in (KV_W, KV_W, KV_W, MOBA_KV_W))
    new_s = []
    for l in range(depth):
        lw = dict(
            wq=wq[l], wz=wz[l], wkv=wkv[l],
            cmp_wk=_cmp_stage1_weights(phi_w1[l, 0]), cmp_wv=_cmp_stage1_weights(phi_w1[l, 1]),
            pos_flat=phi_pos[l].reshape(2, 1, CMP_LEN * HEAD_DIM),
            phi_w1=phi_w1[l], phi_b1=phi_b1[l].reshape(2, 1, HEAD_DIM),
            cmp_w2t=_cmp_stage2_weights(phi_w2[l]), phi_b2=phi_b2[l].reshape(2, HEAD_DIM, 1),
            w_up_a=w_up_a[l].astype(BF16), w_up_b=w_up_b[l].astype(BF16), w_out=w_out[l].astype(BF16),
            ln_g=ln_g[l].reshape(1, d), ln_b=ln_b[l].reshape(1, d))
        shift, scale, gate = jnp.split(mod[l], 3, axis=-1)
        mod_p = tuple(a[:b, None, :] for a in (shift, scale, gate))
        mod_s = tuple(a[None, b:b + db, :] for a in (shift, scale, gate))
        yp, kv_p = _layer_prompt(yp, mod_p, lw, consts, l, depth, kv_p)
        ys, ns_ = _layer_sample(ys, mod_s, lw, consts, l, caches, page_table)
        new_s.append(ns_)

    def stack_s(key):
        return jnp.stack([it[key] for it in new_s], axis=1)

    cmp_p, slc_p, win_p, moba_p = kv_p
    win_p = jnp.moveaxis(_kv_output(win_p[..., t - min(WINDOW, t):], NSA_KV), 1, 0)
    return (yp, ys.reshape(db, 1, d),
            _kv_output(cmp_p, NSA_KV), stack_s("cmp"),
            _kv_output(slc_p, NSA_KV), stack_s("slc"),
            _kv_output(moba_p, MOBA_HEADS), stack_s("moba"),
            win_p, jnp.moveaxis(stack_s("win"), 1, 0))
```
